```python
import math
import jax, jax.numpy as jnp
from jax import lax
import numpy as np

D_MODEL = 1024
BATCH = 2
SEQ = 8192
DEPTH = 2

N_ATT_HEADS = 4
ATT_HEAD_DIM = D_MODEL // 16
ATT_WIDTH = N_ATT_HEADS * 2 * ATT_HEAD_DIM
N_SGU_GROUPS = 4
SGU_CHUNK = 128
SGU_WIDTH = D_MODEL - ATT_WIDTH
SGU_GROUP_DIM = SGU_WIDTH // N_SGU_GROUPS
D_MIX = ATT_WIDTH + SGU_WIDTH
D_IN_PROJ = 3 * ATT_WIDTH + 2 * SGU_WIDTH
D_FF = 2816
CONV_WIDTH = 3
Q_BLOCK = 128
NORM_EPS = 1e-6
SUBLN_EPS = 1e-5
LN_EPS = 1e-5
NEG_INF = -1e30

kernel_name = 'hybrid_diffattn_sgu_convffn'


def _rmsnorm(x, g, eps=NORM_EPS):
    xf = x.astype(jnp.float32)
    y = xf * lax.rsqrt(jnp.mean(xf * xf, axis=-1, keepdims=True) + eps)
    return (y * g.astype(jnp.float32)).astype(x.dtype)


def _layernorm(x, g, b, eps=LN_EPS):
    xf = x.astype(jnp.float32)
    mu = jnp.mean(xf, axis=-1, keepdims=True)
    xc = xf - mu
    y = xc * lax.rsqrt(jnp.mean(xc * xc, axis=-1, keepdims=True) + eps)
    return (y * g.astype(jnp.float32) + b.astype(jnp.float32)).astype(x.dtype)


def _alibi_slopes(n_heads):
    return jnp.asarray([2.0 ** (-8.0 * (h + 1) / n_heads) for h in range(n_heads)], dtype=jnp.float32)


def _lambda_init(layer_idx):
    return 0.8 - 0.6 * math.exp(-0.3 * layer_idx)


def _diff_attention(q1, q2, k1, k2, v, lam):
    b, h, s, d = q1.shape
    nb = s // Q_BLOCK
    scale = d ** -0.5
    slopes = _alibi_slopes(h)
    kpos = jnp.arange(s)
    qb1 = q1.reshape(b, h, nb, Q_BLOCK, d).transpose(2, 0, 1, 3, 4)
    qb2 = q2.reshape(b, h, nb, Q_BLOCK, d).transpose(2, 0, 1, 3, 4)

    def block(args):
        qa, qb, i = args
        qpos = i * Q_BLOCK + jnp.arange(Q_BLOCK)
        dist = (qpos[:, None] - kpos[None, :]).astype(jnp.float32)
        bias = -slopes[:, None, None] * dist
        causal = dist >= 0
        s1 = jnp.einsum('bhqd,bhkd->bhqk', qa, k1).astype(jnp.float32) * scale + bias
        s2 = jnp.einsum('bhqd,bhkd->bhqk', qb, k2).astype(jnp.float32) * scale + bias
        p1 = jax.nn.softmax(jnp.where(causal, s1, NEG_INF), axis=-1)
        p2 = jax.nn.softmax(jnp.where(causal, s2, NEG_INF), axis=-1)
        p = (p1 - lam * p2).astype(v.dtype)
        return jnp.einsum('bhqk,bhkd->bhqd', p, v)

    out = lax.map(block, (qb1, qb2, jnp.arange(nb)))
    return out.transpose(1, 2, 0, 3, 4).reshape(b, h, s, 2 * d)


def _spatial_gating(u, vg, ln_g, ln_b, w_s, b_s):
    b, s, _ = u.shape
    nc = s // SGU_CHUNK
    u = jax.nn.gelu(u)
    vg = jax.nn.gelu(vg).reshape(b, s, N_SGU_GROUPS, SGU_GROUP_DIM)
    vg = _layernorm(vg, ln_g.reshape(N_SGU_GROUPS, SGU_GROUP_DIM), ln_b.reshape(N_SGU_GROUPS, SGU_GROUP_DIM))
    vg = vg.reshape(b, nc, SGU_CHUNK, N_SGU_GROUPS, SGU_GROUP_DIM)
    tril = jnp.tril(jnp.ones((SGU_CHUNK, SGU_CHUNK), dtype=bool))
    w_m = jnp.where(tril[None], w_s, jnp.zeros_like(w_s))
    vmix = jnp.einsum('gts,bnsgc->bntgc', w_m, vg) + b_s.T[None, None, :, :, None]
    return u * vmix.reshape(b, s, SGU_WIDTH)


def _causal_dwconv(x, w, bias):
    s = x.shape[1]
    xp = jnp.pad(x, ((0, 0), (CONV_WIDTH - 1, 0), (0, 0)))
    out = bias
    for j in range(CONV_WIDTH):
        out = out + w[j] * xp[:, j:j + s]
    return out


def setup_inputs(seed: int = 0) -> dict:
    key = jax.random.key(seed)
    ks = jax.random.split(key, 20)
    f32 = jnp.float32
    nrm = lambda k, shp, sc: jax.random.normal(k, shp, f32) * sc
    d = ATT_HEAD_DIM
    return {
        'x': nrm(ks[0], (BATCH, SEQ, D_MODEL), 1.0),
        'norm1_g': 1.0 + nrm(ks[1], (DEPTH, D_MODEL), 0.02),
        'w_in': nrm(ks[2], (DEPTH, D_MODEL, D_IN_PROJ), D_MODEL ** -0.5),
        'lam_q1': nrm(ks[3], (DEPTH, d), 0.1),
        'lam_k1': nrm(ks[4], (DEPTH, d), 0.1),
        'lam_q2': nrm(ks[5], (DEPTH, d), 0.1),
        'lam_k2': nrm(ks[6], (DEPTH, d), 0.1),
        'subln_g': 1.0 + nrm(ks[7], (DEPTH, 2 * d), 0.02),
        'sgu_ln_g': 1.0 + nrm(ks[8], (DEPTH, SGU_WIDTH), 0.02),
        'sgu_ln_b': nrm(ks[9], (DEPTH, SGU_WIDTH), 0.02),
        'sgu_w': nrm(ks[10], (DEPTH, N_SGU_GROUPS, SGU_CHUNK, SGU_CHUNK), SGU_CHUNK ** -0.5),
        'sgu_b': 1.0 + nrm(ks[11], (DEPTH, N_SGU_GROUPS, SGU_CHUNK), 0.02),
        'w_out': nrm(ks[12], (DEPTH, D_MIX, D_MODEL), D_MIX ** -0.5),
        'norm2_g': 1.0 + nrm(ks[13], (DEPTH, D_MODEL), 0.02),
        'ffn_w_up': nrm(ks[14], (DEPTH, D_MODEL, 2 * D_FF), D_MODEL ** -0.5),
        'ffn_conv_w': nrm(ks[15], (DEPTH, CONV_WIDTH, 2 * D_FF), CONV_WIDTH ** -0.5),
        'ffn_conv_b': nrm(ks[16], (DEPTH, 2 * D_FF), 0.02),
        'ffn_w_down': nrm(ks[17], (DEPTH, D_FF, D_MODEL), D_FF ** -0.5),
        'final_g': 1.0 + nrm(ks[18], (D_MODEL,), 0.02),
    }


def reference(x, norm1_g, w_in, lam_q1, lam_k1, lam_q2, lam_k2, subln_g, sgu_ln_g, sgu_ln_b,
              sgu_w, sgu_b, w_out, norm2_g, ffn_w_up, ffn_conv_w, ffn_conv_b, ffn_w_down, final_g):
    b, s, _ = x.shape
    h_, d = N_ATT_HEADS, ATT_HEAD_DIM
    for l in range(DEPTH):
        lam_init = _lambda_init(l)
        h = _rmsnorm(x, norm1_g[l])
        z = h @ w_in[l]
        q, k, v, u, vg = jnp.split(z, [ATT_WIDTH, 2 * ATT_WIDTH, 3 * ATT_WIDTH, 3 * ATT_WIDTH + SGU_WIDTH], axis=-1)
        q = q.reshape(b, s, h_, 2, d).transpose(3, 0, 2, 1, 4)
        k = k.reshape(b, s, h_, 2, d).transpose(3, 0, 2, 1, 4)
        v = v.reshape(b, s, h_, 2 * d).transpose(0, 2, 1, 3)
        lam = (jnp.exp(jnp.sum(lam_q1[l].astype(jnp.float32) * lam_k1[l].astype(jnp.float32)))
               - jnp.exp(jnp.sum(lam_q2[l].astype(jnp.float32) * lam_k2[l].astype(jnp.float32)))
               + lam_init)
        att = _diff_attention(q[0], q[1], k[0], k[1], v, lam)
        att = _rmsnorm(att, subln_g[l], SUBLN_EPS) * (1.0 - lam_init)
        att = att.transpose(0, 2, 1, 3).reshape(b, s, ATT_WIDTH)
        sgu = _spatial_gating(u, vg, sgu_ln_g[l], sgu_ln_b[l], sgu_w[l], sgu_b[l])
        x = x + jnp.concatenate([att, sgu], axis=-1) @ w_out[l]
        h2 = _rmsnorm(x, norm2_g[l])
        up = _causal_dwconv(h2 @ ffn_w_up[l], ffn_conv_w[l], ffn_conv_b[l])
        gate, val = jnp.split(up, 2, axis=-1)
        x = x + (jax.nn.gelu(gate) * val) @ ffn_w_down[l]
    return _rmsnorm(x, final_g)
```

```python
import functools
import math

import jax
import jax.numpy as jnp
from jax import lax
from jax.experimental import pallas as pl
from jax.experimental.pallas import tpu as pltpu

F32 = jnp.float32
BF16 = jnp.bfloat16

D_MODEL = 1024
DEPTH = 2
N_HEADS = 4
HEAD_DIM = 64
HEAD_SLAB = 2 * HEAD_DIM
ATT_WIDTH = N_HEADS * HEAD_SLAB
N_GROUPS = 4
SGU_CHUNK = 128
GROUP_DIM = 128
SGU_WIDTH = N_GROUPS * GROUP_DIM
D_FF = 2816
CONV_WIDTH = 3
NORM_EPS = 1e-6
SUBLN_EPS = 1e-5
LN_EPS = 1e-5
NEG_INF = -1e30

SUBLANES = 8
VMEM_LIMIT_BYTES = 56 * 1024 * 1024

MIX_TM = 512
ATT_T = 256
FFN_TM = 512
FFN_FC = 256


def _gelu_tanh(x):
    c = math.sqrt(2.0 / math.pi)
    return 0.5 * x * (1.0 + jnp.tanh(c * (x + 0.044715 * (x * x * x))))


def _lambda_init(layer_idx):
    return 0.8 - 0.6 * math.exp(-0.3 * layer_idx)


def _mixin_kernel(x_ref, g_ref, w_ref, lng_ref, lnb_ref, sw_ref, sb_ref,
                  qa_ref, qb_ref, k_ref, v_ref, sgu_ref, *, tm):
    x = x_ref[...]
    ms = jnp.mean(x * x, axis=-1, keepdims=True)
    hb = (x * lax.rsqrt(ms + NORM_EPS) * g_ref[...]).astype(BF16)

    def proj(c0, width):
        return jnp.dot(hb, w_ref[:, c0:c0 + width], preferred_element_type=F32)

    zq = proj(0, ATT_WIDTH) * (HEAD_DIM ** -0.5)
    lane = lax.broadcasted_iota(jnp.int32, zq.shape, 1)
    first = (lane % HEAD_SLAB) < HEAD_DIM
    qa_ref[...] = jnp.where(first, zq, 0.0).astype(BF16)
    qb_ref[...] = jnp.where(first, 0.0, zq).astype(BF16)
    k_ref[...] = proj(ATT_WIDTH, ATT_WIDTH).astype(BF16)
    v_ref[...] = proj(2 * ATT_WIDTH, ATT_WIDTH).astype(BF16)

    u = _gelu_tanh(proj(3 * ATT_WIDTH, SGU_WIDTH))
    vg = _gelu_tanh(proj(3 * ATT_WIDTH + SGU_WIDTH, SGU_WIDTH))
    row = lax.broadcasted_iota(jnp.int32, (SGU_CHUNK, SGU_CHUNK), 0)
    col = lax.broadcasted_iota(jnp.int32, (SGU_CHUNK, SGU_CHUNK), 1)
    tril = col <= row
    for g in range(N_GROUPS):
        lo, hi = g * GROUP_DIM, (g + 1) * GROUP_DIM
        y = vg[:, lo:hi]
        mu = jnp.mean(y, axis=-1, keepdims=True)
        yc = y - mu
        var = jnp.mean(yc * yc, axis=-1, keepdims=True)
        yn = (yc * lax.rsqrt(var + LN_EPS) * lng_ref[:, lo:hi] + lnb_ref[:, lo:hi]).astype(BF16)
        wm = jnp.where(tril, sw_ref[g], 0.0).astype(BF16)
        bias = sb_ref[g]
        for c in range(tm // SGU_CHUNK):
            r0, r1 = c * SGU_CHUNK, (c + 1) * SGU_CHUNK
            vmix = jnp.dot(wm, yn[r0:r1, :], preferred_element_type=F32) + bias
            sgu_ref[r0:r1, lo:hi] = (u[r0:r1, lo:hi] * vmix).astype(BF16)


def _mix_in(x2d, g, w_in, ln_g, ln_b, sgu_w, sgu_b_full):
    rows = x2d.shape[0]
    tm = MIX_TM
    grid = (rows // tm,)
    row_spec = lambda width: pl.BlockSpec((tm, width), lambda i: (i, 0))
    whole = pl.BlockSpec(memory_space=pltpu.VMEM)
    out_sds = jax.ShapeDtypeStruct((rows, ATT_WIDTH), BF16)
    return pl.pallas_call(
        functools.partial(_mixin_kernel, tm=tm),
        grid=grid,
        in_specs=[row_spec(D_MODEL), whole, whole, whole, whole, whole, whole],
        out_specs=[row_spec(ATT_WIDTH)] * 5,
        out_shape=[out_sds] * 5,
        compiler_params=pltpu.CompilerParams(
            dimension_semantics=("arbitrary",), vmem_limit_bytes=VMEM_LIMIT_BYTES),
        name="mix_in",
    )(x2d, g, w_in, ln_g, ln_b, sgu_w, sgu_b_full)


def _softmax_block(s, c, v_blk, m_ref, l_ref, acc_ref, first):
    bm = jnp.max(s, axis=-1, keepdims=True) + c
    if first:
        m_new = bm
    else:
        m_old = m_ref[...]
        m_new = jnp.maximum(m_old, bm)
    p = jnp.exp(s - (m_new - c))
    ps = jnp.sum(p, axis=-1, keepdims=True)
    pv = jnp.dot(p.astype(BF16), v_blk, preferred_element_type=F32)
    if first:
        l_ref[...] = ps
        acc_ref[...] = pv
    else:
        alpha = jnp.exp(m_old - m_new)
        l_ref[...] = alpha * l_ref[...] + ps
        acc_ref[...] = alpha * acc_ref[...] + pv
    m_ref[...] = m_new


def _attn_kernel(slopes_ref, lq1_ref, lk1_ref, lq2_ref, lk2_ref, sg_ref,
                 qa_ref, qb_ref, k_ref, v_ref, o_ref,
                 m1_ref, l1_ref, acc1_ref, m2_ref, l2_ref, acc2_ref, *, t, lam_init):
    h = pl.program_id(1)
    i = pl.program_id(2)
    slope = slopes_ref[h]
    qa = qa_ref[...]
    qb = qb_ref[...]
    ii = lax.broadcasted_iota(jnp.int32, (t, t), 0)
    jj = lax.broadcasted_iota(jnp.int32, (t, t), 1)
    rel_bias = (jj - ii).astype(F32) * slope
    nt = (((1,), (1,)), ((), ()))

    def scores(q, k_blk):
        return lax.dot_general(q, k_blk, nt, preferred_element_type=F32) + rel_bias

    d0 = pl.multiple_of(i * t, t)
    k_blk = k_ref[pl.ds(d0, t), :]
    v_blk = v_ref[pl.ds(d0, t), :]
    causal = jj <= ii
    _softmax_block(jnp.where(causal, scores(qa, k_blk), NEG_INF), 0.0, v_blk,
                   m1_ref, l1_ref, acc1_ref, True)
    _softmax_block(jnp.where(causal, scores(qb, k_blk), NEG_INF), 0.0, v_blk,
                   m2_ref, l2_ref, acc2_ref, True)

    def body(j, carry):
        j0 = pl.multiple_of(j * t, t)
        kb = k_ref[pl.ds(j0, t), :]
        vb = v_ref[pl.ds(j0, t), :]
        c = ((j - i) * t).astype(F32) * slope
        _softmax_block(scores(qa, kb), c, vb, m1_ref, l1_ref, acc1_ref, False)
        _softmax_block(scores(qb, kb), c, vb, m2_ref, l2_ref, acc2_ref, False)
        return carry

    lax.fori_loop(0, i, body, 0)

    lam = (jnp.exp(jnp.sum(lq1_ref[...] * lk1_ref[...], axis=-1, keepdims=True))
           - jnp.exp(jnp.sum(lq2_ref[...] * lk2_ref[...], axis=-1, keepdims=True))
           + lam_init)
    o = acc1_ref[...] / l1_ref[...] - lam * (acc2_ref[...] / l2_ref[...])
    ms = jnp.mean(o * o, axis=-1, keepdims=True)
    o = o * lax.rsqrt(ms + SUBLN_EPS) * sg_ref[...] * (1.0 - lam_init)
    o_ref[...] = o.astype(BF16)


def _attention(slopes, lq1, lk1, lq2, lk2, subln_g, qa, qb, k, v, *, batch, seq, lam_init):
    t = ATT_T
    nq = seq // t
    grid = (batch, N_HEADS, nq)
    q_spec = pl.BlockSpec((t, HEAD_SLAB), lambda b, h, i: (b * nq + i, h))
    kv_spec = pl.BlockSpec((seq, HEAD_SLAB), lambda b, h, i: (b, h))
    whole = pl.BlockSpec(memory_space=pltpu.VMEM)
    smem = pl.BlockSpec(memory_space=pltpu.SMEM)
    stat = pltpu.VMEM((t, 1), F32)
    acc = pltpu.VMEM((t, HEAD_SLAB), F32)
    return pl.pallas_call(
        functools.partial(_attn_kernel, t=t, lam_init=lam_init),
        grid=grid,
        in_specs=[smem, whole, whole, whole, whole, whole, q_spec, q_spec, kv_spec, kv_spec],
        out_specs=q_spec,
        out_shape=jax.ShapeDtypeStruct((batch * seq, ATT_WIDTH), BF16),
        scratch_shapes=[stat, stat, acc, stat, stat, acc],
        compiler_params=pltpu.CompilerParams(
            dimension_semantics=("arbitrary", "arbitrary", "arbitrary"),
            vmem_limit_bytes=VMEM_LIMIT_BYTES),
        name="diff_attn",
    )(slopes, lq1, lk1, lq2, lk2, subln_g, qa, qb, k, v)


def _ffn_kernel(x_ref, att_ref, sgu_ref, wo_ref, g2_ref, wup_ref, cw_ref, cb_ref, wdn_ref, fg_ref,
                o_ref, carry_ref, buf_ref, act_ref, *, tm, tiles_per_batch, final):
    fc = FFN_FC
    halo = SUBLANES

    @pl.when(pl.program_id(0) % tiles_per_batch == 0)
    def _():
        carry_ref[...] = jnp.zeros_like(carry_ref)

    x1 = (x_ref[...]
          + jnp.dot(att_ref[...], wo_ref[0:ATT_WIDTH, :], preferred_element_type=F32)
          + jnp.dot(sgu_ref[...], wo_ref[ATT_WIDTH:ATT_WIDTH + SGU_WIDTH, :],
                    preferred_element_type=F32))
    ms = jnp.mean(x1 * x1, axis=-1, keepdims=True)
    h2 = (x1 * lax.rsqrt(ms + NORM_EPS) * g2_ref[...]).astype(BF16)

    for c in range(D_FF // fc):
        convs = []
        for half in range(2):
            col0 = half * D_FF + c * fc
            up = jnp.dot(h2, wup_ref[:, col0:col0 + fc], preferred_element_type=F32)
            buf_ref[half, 0:halo, :] = carry_ref[half, c]
            buf_ref[half, halo:halo + tm, :] = up
            carry_ref[half, c] = up[tm - halo:tm, :]
            w0 = cw_ref[0:1, col0:col0 + fc]
            w1 = cw_ref[1:2, col0:col0 + fc]
            w2 = cw_ref[2:3, col0:col0 + fc]
            convs.append(cb_ref[:, col0:col0 + fc]
                         + w0 * buf_ref[half, halo - 2:halo - 2 + tm, :]
                         + w1 * buf_ref[half, halo - 1:halo - 1 + tm, :]
                         + w2 * up)
        act_ref[:, c * fc:(c + 1) * fc] = (_gelu_tanh(convs[0]) * convs[1]).astype(BF16)

    out = x1 + jnp.dot(act_ref[...], wdn_ref[...], preferred_element_type=F32)
    if final:
        ms = jnp.mean(out * out, axis=-1, keepdims=True)
        out = out * lax.rsqrt(ms + NORM_EPS) * fg_ref[...]
    o_ref[...] = out


def _ffn(x2d, att, sgu, w_out, g2, w_up, conv_w, conv_b, w_down, final_g, *, seq, final):
    rows = x2d.shape[0]
    tm = FFN_TM
    grid = (rows // tm,)
    row_spec = lambda width: pl.BlockSpec((tm, width), lambda i: (i, 0))
    whole = pl.BlockSpec(memory_space=pltpu.VMEM)
    return pl.pallas_call(
        functools.partial(_ffn_kernel, tm=tm, tiles_per_batch=seq // tm, final=final),
        grid=grid,
        in_specs=[row_spec(D_MODEL), row_spec(ATT_WIDTH), row_spec(SGU_WIDTH),
                  whole, whole, whole, whole, whole, whole, whole],
        out_specs=row_spec(D_MODEL),
        out_shape=jax.ShapeDtypeStruct((rows, D_MODEL), F32),
        scratch_shapes=[
            pltpu.VMEM((2, D_FF // FFN_FC, SUBLANES, FFN_FC), F32),
            pltpu.VMEM((2, SUBLANES + tm, FFN_FC), F32),
            pltpu.VMEM((tm, D_FF), BF16),
        ],
        compiler_params=pltpu.CompilerParams(
            dimension_semantics=("arbitrary",), vmem_limit_bytes=VMEM_LIMIT_BYTES),
        name="ffn",
    )(x2d, att, sgu, w_out, g2, w_up, conv_w, conv_b, w_down, final_g)


def kernel(x, norm1_g, w_in, lam_q1, lam_k1, lam_q2, lam_k2, subln_g, sgu_ln_g, sgu_ln_b,
           sgu_w, sgu_b, w_out, norm2_g, ffn_w_up, ffn_conv_w, ffn_conv_b, ffn_w_down, final_g):
    batch, seq, _ = x.shape
    rows = batch * seq
    x2d = x.reshape(rows, D_MODEL)
    slopes = jnp.asarray([2.0 ** (-8.0 * (h + 1) / N_HEADS) for h in range(N_HEADS)], dtype=F32)
    row1 = lambda a: a.reshape(1, -1).astype(F32)
    for l in range(DEPTH):
        lam_init = _lambda_init(l)
        sgu_b_full = jnp.broadcast_to(sgu_b[l].astype(F32)[:, :, None],
                                      (N_GROUPS, SGU_CHUNK, GROUP_DIM))
        qa, qb, k, v, sgu = _mix_in(x2d, row1(norm1_g[l]), w_in[l].astype(BF16),
                                    row1(sgu_ln_g[l]), row1(sgu_ln_b[l]),
                                    sgu_w[l].astype(F32), sgu_b_full)
        att = _attention(slopes, row1(lam_q1[l]), row1(lam_k1[l]), row1(lam_q2[l]), row1(lam_k2[l]),
                         row1(subln_g[l]), qa, qb, k, v, batch=batch, seq=seq, lam_init=lam_init)
        x2d = _ffn(x2d, att, sgu, w_out[l].astype(BF16), row1(norm2_g[l]),
                   ffn_w_up[l].astype(BF16), ffn_conv_w[l].astype(F32), row1(ffn_conv_b[l]),
                   ffn_w_down[l].astype(BF16), row1(final_g), seq=seq, final=(l == DEPTH - 1))
    return x2d.reshape(batch, seq, D_MODEL)
```

```python
import functools
import math

import jax
import jax.numpy as jnp
from jax import lax
from jax.experimental import pallas as pl
from jax.experimental.pallas import tpu as pltpu

F32 = jnp.float32
BF16 = jnp.bfloat16

D_MODEL = 1024
DEPTH = 2
N_HEADS = 4
HEAD_DIM = 64
HEAD_SLAB = 2 * HEAD_DIM
ATT_WIDTH = N_HEADS * HEAD_SLAB
N_GROUPS = 4
SGU_CHUNK = 128
GROUP_DIM = 128
SGU_WIDTH = N_GROUPS * GROUP_DIM
D_FF = 2816
CONV_WIDTH = 3
NORM_EPS = 1e-6
SUBLN_EPS = 1e-5
LN_EPS = 1e-5
NEG_INF = -1e30

SUBLANES = 8
BF16_EXACT_INT = 256
VMEM_LIMIT_BYTES = 56 * 1024 * 1024

MIX_TM = 512
ATT_T = 256
FFN_TM = 512
FFN_FC = 256


def _gelu_tanh(x):
    c = math.sqrt(2.0 / math.pi)
    return 0.5 * x * (1.0 + jnp.tanh(c * (x + 0.044715 * (x * x * x))))


def _alibi_slope(head):
    return 2.0 ** (-8.0 * (head + 1) / N_HEADS)


def _lambda_init(layer_idx):
    return 0.8 - 0.6 * math.exp(-0.3 * layer_idx)


def _mixin_kernel(x_ref, g_ref, w_ref, lng_ref, lnb_ref, sw_ref, sb_ref,
                  qat_ref, qbt_ref, ka_ref, kb_ref, vt_ref, sgu_ref, *, tm, t):
    x = x_ref[...]
    ms = jnp.mean(x * x, axis=-1, keepdims=True)
    hb = (x * lax.rsqrt(ms + NORM_EPS) * g_ref[...]).astype(BF16)

    def proj(c0, width):
        return jnp.dot(hb, w_ref[:, c0:c0 + width], preferred_element_type=F32)

    zq = proj(0, ATT_WIDTH) * (HEAD_DIM ** -0.5)
    zk = proj(ATT_WIDTH, ATT_WIDTH)
    zv = proj(2 * ATT_WIDTH, ATT_WIDTH)
    lane = lax.broadcasted_iota(jnp.int32, (tm, HEAD_SLAB), 1)
    pos = lax.broadcasted_iota(jnp.int32, (tm, HEAD_SLAB), 0) % t
    pos_lo = (pos % BF16_EXACT_INT).astype(F32)
    pos_hi = (pos - pos % BF16_EXACT_INT).astype(F32)
    in_a = lane < HEAD_DIM
    for h in range(N_HEADS):
        lo, hi = h * HEAD_SLAB, (h + 1) * HEAD_SLAB
        slope = _alibi_slope(h)
        qs, ks = zq[:, lo:hi], zk[:, lo:hi]
        ones_a = jnp.where((lane == HEAD_DIM) | (lane == HEAD_DIM + 1), 1.0, 0.0)
        ones_b = jnp.where(lane < 2, 1.0, 0.0)
        bias_a = jnp.where(lane == HEAD_DIM, slope * pos_lo,
                           jnp.where(lane == HEAD_DIM + 1, slope * pos_hi, 0.0))
        bias_b = jnp.where(lane == 0, slope * pos_lo, jnp.where(lane == 1, slope * pos_hi, 0.0))
        ka_ref[:, lo:hi] = jnp.where(in_a, ks, bias_a).astype(BF16)
        kb_ref[:, lo:hi] = jnp.where(in_a, bias_b, ks).astype(BF16)
        qat = jnp.where(in_a, qs, ones_a).T.astype(BF16)
        qbt = jnp.where(in_a, ones_b, qs).T.astype(BF16)
        vt = zv[:, lo:hi].T.astype(BF16)
        for c in range(tm // t):
            qat_ref[c, lo:hi, :] = qat[:, c * t:(c + 1) * t]
            qbt_ref[c, lo:hi, :] = qbt[:, c * t:(c + 1) * t]
            vt_ref[c, lo:hi, :] = vt[:, c * t:(c + 1) * t]

    u = _gelu_tanh(proj(3 * ATT_WIDTH, SGU_WIDTH))
    vg = _gelu_tanh(proj(3 * ATT_WIDTH + SGU_WIDTH, SGU_WIDTH))
    row = lax.broadcasted_iota(jnp.int32, (SGU_CHUNK, SGU_CHUNK), 0)
    col = lax.broadcasted_iota(jnp.int32, (SGU_CHUNK, SGU_CHUNK), 1)
    tril = col <= row
    for g in range(N_GROUPS):
        lo, hi = g * GROUP_DIM, (g + 1) * GROUP_DIM
        y = vg[:, lo:hi]
        mu = jnp.mean(y, axis=-1, keepdims=True)
        yc = y - mu
        var = jnp.mean(yc * yc, axis=-1, keepdims=True)
        yn = (yc * lax.rsqrt(var + LN_EPS) * lng_ref[:, lo:hi] + lnb_ref[:, lo:hi]).astype(BF16)
        wm = jnp.where(tril, sw_ref[g], 0.0).astype(BF16)
        bias = sb_ref[g]
        for c in range(tm // SGU_CHUNK):
            r0, r1 = c * SGU_CHUNK, (c + 1) * SGU_CHUNK
            vmix = jnp.dot(wm, yn[r0:r1, :], preferred_element_type=F32) + bias
            sgu_ref[r0:r1, lo:hi] = (u[r0:r1, lo:hi] * vmix).astype(BF16)


def _mix_in(x2d, g, w_in, ln_g, ln_b, sgu_w, sgu_b_full):
    rows = x2d.shape[0]
    tm, t = MIX_TM, ATT_T
    grid = (rows // tm,)
    row_spec = lambda width: pl.BlockSpec((tm, width), lambda i: (i, 0))
    tr_spec = pl.BlockSpec((tm // t, ATT_WIDTH, t), lambda i: (i, 0, 0))
    whole = pl.BlockSpec(memory_space=pltpu.VMEM)
    row_sds = jax.ShapeDtypeStruct((rows, ATT_WIDTH), BF16)
    tr_sds = jax.ShapeDtypeStruct((rows // t, ATT_WIDTH, t), BF16)
    return pl.pallas_call(
        functools.partial(_mixin_kernel, tm=tm, t=t),
        grid=grid,
        in_specs=[row_spec(D_MODEL), whole, whole, whole, whole, whole, whole],
        out_specs=[tr_spec, tr_spec, row_spec(ATT_WIDTH), row_spec(ATT_WIDTH), tr_spec,
                   row_spec(SGU_WIDTH)],
        out_shape=[tr_sds, tr_sds, row_sds, row_sds, tr_sds, row_sds],
        compiler_params=pltpu.CompilerParams(
            dimension_semantics=("arbitrary",), vmem_limit_bytes=VMEM_LIMIT_BYTES),
        name="mix_in",
    )(x2d, g, w_in, ln_g, ln_b, sgu_w, sgu_b_full)


def _softmax_block(st, c, vt_blk, m_ref, l_ref, acc_ref, first):
    bm = jnp.max(st, axis=0, keepdims=True) + c
    if first:
        m_new = bm
    else:
        m_old = m_ref[...]
        m_new = jnp.maximum(m_old, bm)
    p = jnp.exp(st - (m_new - c))
    ps = jnp.sum(p, axis=0, keepdims=True)
    pv = jnp.dot(vt_blk, p.astype(BF16), preferred_element_type=F32)
    if first:
        l_ref[...] = ps
        acc_ref[...] = pv
    else:
        alpha = jnp.exp(m_old - m_new)
        l_ref[...] = alpha * l_ref[...] + ps
        acc_ref[...] = alpha * acc_ref[...] + pv
    m_ref[...] = m_new


def _attn_kernel(slopes_ref, lq1_ref, lk1_ref, lq2_ref, lk2_ref, sg_ref,
                 qat_ref, qbt_ref, ka_ref, kb_ref, vt_ref, o_ref,
                 m1_ref, l1_ref, acc1_ref, m2_ref, l2_ref, acc2_ref, *, t, lam_init):
    h = pl.program_id(1)
    i = pl.program_id(2)
    slope = slopes_ref[h]
    qat = qat_ref[...]
    qbt = qbt_ref[...]

    def scores(k_blk, qt):
        return jnp.dot(k_blk, qt, preferred_element_type=F32)

    d0 = pl.multiple_of(i * t, t)
    key = lax.broadcasted_iota(jnp.int32, (t, t), 0)
    qry = lax.broadcasted_iota(jnp.int32, (t, t), 1)
    causal = key <= qry
    vt_blk = vt_ref[i]
    _softmax_block(jnp.where(causal, scores(ka_ref[pl.ds(d0, t), :], qat), NEG_INF), 0.0, vt_blk,
                   m1_ref, l1_ref, acc1_ref, True)
    _softmax_block(jnp.where(causal, scores(kb_ref[pl.ds(d0, t), :], qbt), NEG_INF), 0.0, vt_blk,
                   m2_ref, l2_ref, acc2_ref, True)

    def body(j, carry):
        j0 = pl.multiple_of(j * t, t)
        vb = vt_ref[j]
        c = ((j - i) * t).astype(F32) * slope
        _softmax_block(scores(ka_ref[pl.ds(j0, t), :], qat), c, vb, m1_ref, l1_ref, acc1_ref, False)
        _softmax_block(scores(kb_ref[pl.ds(j0, t), :], qbt), c, vb, m2_ref, l2_ref, acc2_ref, False)
        return carry

    lax.fori_loop(0, i, body, 0)

    lam = (jnp.exp(jnp.sum(lq1_ref[...] * lk1_ref[...], axis=-1, keepdims=True))
           - jnp.exp(jnp.sum(lq2_ref[...] * lk2_ref[...], axis=-1, keepdims=True))
           + lam_init)
    o = acc1_ref[...] * (1.0 / l1_ref[...]) - lam * (acc2_ref[...] * (1.0 / l2_ref[...]))
    ms = jnp.mean(o * o, axis=0, keepdims=True)
    o = o * lax.rsqrt(ms + SUBLN_EPS) * (1.0 - lam_init)
    o_ref[...] = (o.T * sg_ref[...]).astype(BF16)


def _attention(slopes, lq1, lk1, lq2, lk2, subln_g, qat, qbt, ka, kb, vt, *, batch, seq, lam_init):
    t = ATT_T
    nq = seq // t
    grid = (batch, N_HEADS, nq)
    qt_spec = pl.BlockSpec((None, HEAD_SLAB, t), lambda b, h, i: (b * nq + i, h, 0))
    k_spec = pl.BlockSpec((seq, HEAD_SLAB), lambda b, h, i: (b, h))
    vt_spec = pl.BlockSpec((nq, HEAD_SLAB, t), lambda b, h, i: (b, h, 0))
    o_spec = pl.BlockSpec((t, HEAD_SLAB), lambda b, h, i: (b * nq + i, h))
    whole = pl.BlockSpec(memory_space=pltpu.VMEM)
    smem = pl.BlockSpec(memory_space=pltpu.SMEM)
    stat = pltpu.VMEM((1, t), F32)
    acc = pltpu.VMEM((HEAD_SLAB, t), F32)
    return pl.pallas_call(
        functools.partial(_attn_kernel, t=t, lam_init=lam_init),
        grid=grid,
        in_specs=[smem, whole, whole, whole, whole, whole, qt_spec, qt_spec, k_spec, k_spec, vt_spec],
        out_specs=o_spec,
        out_shape=jax.ShapeDtypeStruct((batch * seq, ATT_WIDTH), BF16),
        scratch_shapes=[stat, stat, acc, stat, stat, acc],
        compiler_params=pltpu.CompilerParams(
            dimension_semantics=("arbitrary", "arbitrary", "arbitrary"),
            vmem_limit_bytes=VMEM_LIMIT_BYTES),
        name="diff_attn",
    )(slopes, lq1, lk1, lq2, lk2, subln_g, qat, qbt, ka, kb, vt)


def _ffn_kernel(x_ref, att_ref, sgu_ref, wo_ref, g2_ref, wup_ref, cw_ref, cb_ref, wdn_ref, fg_ref,
                o_ref, carry_ref, buf_ref, act_ref, *, tm, tiles_per_batch, final):
    fc = FFN_FC
    halo = SUBLANES

    @pl.when(pl.program_id(0) % tiles_per_batch == 0)
    def _():
        carry_ref[...] = jnp.zeros_like(carry_ref)

    x1 = (x_ref[...]
          + jnp.dot(att_ref[...], wo_ref[0:ATT_WIDTH, :], preferred_element_type=F32)
          + jnp.dot(sgu_ref[...], wo_ref[ATT_WIDTH:ATT_WIDTH + SGU_WIDTH, :],
                    preferred_element_type=F32))
    ms = jnp.mean(x1 * x1, axis=-1, keepdims=True)
    h2 = (x1 * lax.rsqrt(ms + NORM_EPS) * g2_ref[...]).astype(BF16)

    for c in range(D_FF // fc):
        convs = []
        for half in range(2):
            col0 = half * D_FF + c * fc
            up = jnp.dot(h2, wup_ref[:, col0:col0 + fc], preferred_element_type=F32)
            buf_ref[half, 0:halo, :] = carry_ref[half, c]
            buf_ref[half, halo:halo + tm, :] = up
            carry_ref[half, c] = up[tm - halo:tm, :]
            w0 = cw_ref[0:1, col0:col0 + fc]
            w1 = cw_ref[1:2, col0:col0 + fc]
            w2 = cw_ref[2:3, col0:col0 + fc]
            convs.append(cb_ref[:, col0:col0 + fc]
                         + w0 * buf_ref[half, halo - 2:halo - 2 + tm, :]
                         + w1 * buf_ref[half, halo - 1:halo - 1 + tm, :]
                         + w2 * up)
        act_ref[:, c * fc:(c + 1) * fc] = (_gelu_tanh(convs[0]) * convs[1]).astype(BF16)

    out = x1 + jnp.dot(act_ref[...], wdn_ref[...], preferred_element_type=F32)
    if final:
        ms = jnp.mean(out * out, axis=-1, keepdims=True)
        out = out * lax.rsqrt(ms + NORM_EPS) * fg_ref[...]
    o_ref[...] = out


def _ffn(x2d, att, sgu, w_out, g2, w_up, conv_w, conv_b, w_down, final_g, *, seq, final):
    rows = x2d.shape[0]
    tm = FFN_TM
    grid = (rows // tm,)
    row_spec = lambda width: pl.BlockSpec((tm, width), lambda i: (i, 0))
    whole = pl.BlockSpec(memory_space=pltpu.VMEM)
    return pl.pallas_call(
        functools.partial(_ffn_kernel, tm=tm, tiles_per_batch=seq // tm, final=final),
        grid=grid,
        in_specs=[row_spec(D_MODEL), row_spec(ATT_WIDTH), row_spec(SGU_WIDTH),
                  whole, whole, whole, whole, whole, whole, whole],
        out_specs=row_spec(D_MODEL),
        out_shape=jax.ShapeDtypeStruct((rows, D_MODEL), F32),
        scratch_shapes=[
            pltpu.VMEM((2, D_FF // FFN_FC, SUBLANES, FFN_FC), F32),
            pltpu.VMEM((2, SUBLANES + tm, FFN_FC), F32),
            pltpu.VMEM((tm, D_FF), BF16),
        ],
        compiler_params=pltpu.CompilerParams(
            dimension_semantics=("arbitrary",), vmem_limit_bytes=VMEM_LIMIT_BYTES),
        name="ffn",
    )(x2d, att, sgu, w_out, g2, w_up, conv_w, conv_b, w_down, final_g)


def kernel(x, norm1_g, w_in, lam_q1, lam_k1, lam_q2, lam_k2, subln_g, sgu_ln_g, sgu_ln_b,
           sgu_w, sgu_b, w_out, norm2_g, ffn_w_up, ffn_conv_w, ffn_conv_b, ffn_w_down, final_g):
    batch, seq, _ = x.shape
    rows = batch * seq
    x2d = x.reshape(rows, D_MODEL)
    slopes = jnp.asarray([_alibi_slope(h) for h in range(N_HEADS)], dtype=F32)
    row1 = lambda a: a.reshape(1, -1).astype(F32)
    for l in range(DEPTH):
        lam_init = _lambda_init(l)
        sgu_b_full = jnp.broadcast_to(sgu_b[l].astype(F32)[:, :, None],
                                      (N_GROUPS, SGU_CHUNK, GROUP_DIM))
        qat, qbt, ka, kb, vt, sgu = _mix_in(x2d, row1(norm1_g[l]), w_in[l].astype(BF16),
                                            row1(sgu_ln_g[l]), row1(sgu_ln_b[l]),
                                            sgu_w[l].astype(F32), sgu_b_full)
        att = _attention(slopes, row1(lam_q1[l]), row1(lam_k1[l]), row1(lam_q2[l]), row1(lam_k2[l]),
                         row1(subln_g[l]), qat, qbt, ka, kb, vt,
                         batch=batch, seq=seq, lam_init=lam_init)
        x2d = _ffn(x2d, att, sgu, w_out[l].astype(BF16), row1(norm2_g[l]),
                   ffn_w_up[l].astype(BF16), ffn_conv_w[l].astype(F32), row1(ffn_conv_b[l]),
                   ffn_w_down[l].astype(BF16), row1(final_g), seq=seq, final=(l == DEPTH - 1))
    return x2d.reshape(batch, seq, D_MODEL)
```

```python
import functools
import math

import jax
import jax.numpy as jnp
import numpy as np
from jax import lax
from jax.experimental import pallas as pl
from jax.experimental.pallas import tpu as pltpu

F32 = jnp.float32
BF16 = jnp.bfloat16

D_MODEL = 1024
DEPTH = 2
N_HEADS = 4
HEAD_DIM = 64
HEAD_SLAB = 2 * HEAD_DIM
ATT_WIDTH = N_HEADS * HEAD_SLAB
N_GROUPS = 4
SGU_CHUNK = 128
GROUP_DIM = 128
SGU_WIDTH = N_GROUPS * GROUP_DIM
D_FF = 2816
CONV_WIDTH = 3
NORM_EPS = 1e-6
SUBLN_EPS = 1e-5
LN_EPS = 1e-5
NEG_INF = -1e30
LOG2_E = math.log2(math.e)

SUBLANES = 8
BF16_EXACT_INT = 256
VMEM_LIMIT_BYTES = 56 * 1024 * 1024

MIX_TM = 512
ATT_T = 512
FFN_TM = 512
FFN_FC = 256


def _gelu_tanh(x):
    c = math.sqrt(2.0 / math.pi)
    return 0.5 * x * (1.0 + jnp.tanh(c * (x + 0.044715 * (x * x * x))))


def _alibi_slope(head):
    return 2.0 ** (-8.0 * (head + 1) / N_HEADS)


def _bf16_terms(value, n=3):
    terms = []
    rest = value
    for _ in range(n):
        term = float(np.asarray(rest, dtype=BF16).astype(np.float32))
        terms.append(term)
        rest -= term
    return terms


def _lambda_init(layer_idx):
    return 0.8 - 0.6 * math.exp(-0.3 * layer_idx)


def _mixin_kernel(x_ref, g_ref, w_ref, lng_ref, lnb_ref, sw_ref, sb_ref,
                  qat_ref, qbt_ref, ka_ref, kb_ref, vt_ref, sgu_ref, *, tm, t):
    x = x_ref[...]
    ms = jnp.mean(x * x, axis=-1, keepdims=True)
    hb = (x * lax.rsqrt(ms + NORM_EPS) * g_ref[...]).astype(BF16)

    def proj(c0, width):
        return jnp.dot(hb, w_ref[:, c0:c0 + width], preferred_element_type=F32)

    zq = proj(0, ATT_WIDTH) * (HEAD_DIM ** -0.5 * LOG2_E)
    zk = proj(ATT_WIDTH, ATT_WIDTH)
    zv = proj(2 * ATT_WIDTH, ATT_WIDTH)
    lane = lax.broadcasted_iota(jnp.int32, (tm, HEAD_SLAB), 1)
    pos = lax.broadcasted_iota(jnp.int32, (tm, HEAD_SLAB), 0) % t
    pos_lo = (pos % BF16_EXACT_INT).astype(F32)
    pos_hi = (pos - pos % BF16_EXACT_INT).astype(F32)
    in_a = lane < HEAD_DIM

    def aug(first_lane, values):
        out = jnp.zeros((tm, HEAD_SLAB), F32)
        for n, val in enumerate(values):
            out = jnp.where(lane == first_lane + n, val, out)
        return out

    for h in range(N_HEADS):
        lo, hi = h * HEAD_SLAB, (h + 1) * HEAD_SLAB
        factor = _bf16_terms(_alibi_slope(h) * LOG2_E)
        positions = [pos_lo] * len(factor) + [pos_hi] * len(factor)
        qs, ks = zq[:, lo:hi], zk[:, lo:hi]
        ka_ref[:, lo:hi] = jnp.where(in_a, ks, aug(HEAD_DIM, positions)).astype(BF16)
        kb_ref[:, lo:hi] = jnp.where(in_a, aug(0, positions), ks).astype(BF16)
        qat = jnp.where(in_a, qs, aug(HEAD_DIM, factor + factor)).T.astype(BF16)
        qbt = jnp.where(in_a, aug(0, factor + factor), qs).T.astype(BF16)
        vt = zv[:, lo:hi].T.astype(BF16)
        for c in range(tm // t):
            qat_ref[c, lo:hi, :] = qat[:, c * t:(c + 1) * t]
            qbt_ref[c, lo:hi, :] = qbt[:, c * t:(c + 1) * t]
            vt_ref[c, lo:hi, :] = vt[:, c * t:(c + 1) * t]

    u = _gelu_tanh(proj(3 * ATT_WIDTH, SGU_WIDTH))
    vg = _gelu_tanh(proj(3 * ATT_WIDTH + SGU_WIDTH, SGU_WIDTH))
    row = lax.broadcasted_iota(jnp.int32, (SGU_CHUNK, SGU_CHUNK), 0)
    col = lax.broadcasted_iota(jnp.int32, (SGU_CHUNK, SGU_CHUNK), 1)
    tril = col <= row
    for g in range(N_GROUPS):
        lo, hi = g * GROUP_DIM, (g + 1) * GROUP_DIM
        y = vg[:, lo:hi]
        mu = jnp.mean(y, axis=-1, keepdims=True)
        yc = y - mu
        var = jnp.mean(yc * yc, axis=-1, keepdims=True)
        yn = (yc * lax.rsqrt(var + LN_EPS) * lng_ref[:, lo:hi] + lnb_ref[:, lo:hi]).astype(BF16)
        wm = jnp.where(tril, sw_ref[g], 0.0).astype(BF16)
        bias = sb_ref[g]
        for c in range(tm // SGU_CHUNK):
            r0, r1 = c * SGU_CHUNK, (c + 1) * SGU_CHUNK
            vmix = jnp.dot(wm, yn[r0:r1, :], preferred_element_type=F32) + bias
            sgu_ref[r0:r1, lo:hi] = (u[r0:r1, lo:hi] * vmix).astype(BF16)


def _mix_in(x2d, g, w_in, ln_g, ln_b, sgu_w, sgu_b_full):
    rows = x2d.shape[0]
    tm, t = MIX_TM, ATT_T
    grid = (rows // tm,)
    row_spec = lambda width: pl.BlockSpec((tm, width), lambda i: (i, 0))
    tr_spec = pl.BlockSpec((tm // t, ATT_WIDTH, t), lambda i: (i, 0, 0))
    whole = pl.BlockSpec(memory_space=pltpu.VMEM)
    row_sds = jax.ShapeDtypeStruct((rows, ATT_WIDTH), BF16)
    tr_sds = jax.ShapeDtypeStruct((rows // t, ATT_WIDTH, t), BF16)
    return pl.pallas_call(
        functools.partial(_mixin_kernel, tm=tm, t=t),
        grid=grid,
        in_specs=[row_spec(D_MODEL), whole, whole, whole, whole, whole, whole],
        out_specs=[tr_spec, tr_spec, row_spec(ATT_WIDTH), row_spec(ATT_WIDTH), tr_spec,
                   row_spec(SGU_WIDTH)],
        out_shape=[tr_sds, tr_sds, row_sds, row_sds, tr_sds, row_sds],
        compiler_params=pltpu.CompilerParams(
            dimension_semantics=("arbitrary",), vmem_limit_bytes=VMEM_LIMIT_BYTES),
        name="mix_in",
    )(x2d, g, w_in, ln_g, ln_b, sgu_w, sgu_b_full)


def _softmax_blocks(sts, c, vt_blk, states, first):
    stats = []
    for st, (m_ref, l_ref, acc_ref) in zip(sts, states):
        bm = jnp.max(st, axis=0, keepdims=True) + c
        m_new = bm if first else jnp.maximum(m_ref[...], bm)
        p = jnp.exp2(st - (m_new - c))
        stats.append((m_new, jnp.sum(p, axis=0, keepdims=True), p.astype(BF16)))
    for (m_new, ps, p), (m_ref, l_ref, acc_ref) in zip(stats, states):
        pv = jnp.dot(vt_blk, p, preferred_element_type=F32)
        if first:
            l_ref[...] = ps
            acc_ref[...] = pv
        else:
            alpha = jnp.exp2(m_ref[...] - m_new)
            l_ref[...] = alpha * l_ref[...] + ps
            acc_ref[...] = alpha * acc_ref[...] + pv
        m_ref[...] = m_new


def _attn_kernel(slopes_ref, lq1_ref, lk1_ref, lq2_ref, lk2_ref, sg_ref,
                 qat_ref, qbt_ref, ka_ref, kb_ref, vt_ref, o_ref,
                 m1_ref, l1_ref, acc1_ref, m2_ref, l2_ref, acc2_ref, *, t, lam_init):
    h = pl.program_id(1)
    i = pl.program_id(2)
    slope = slopes_ref[h]
    qat = qat_ref[...]
    qbt = qbt_ref[...]
    states = ((m1_ref, l1_ref, acc1_ref), (m2_ref, l2_ref, acc2_ref))

    def scores(j):
        j0 = pl.multiple_of(j * t, t)
        return (jnp.dot(ka_ref[pl.ds(j0, t), :], qat, preferred_element_type=F32),
                jnp.dot(kb_ref[pl.ds(j0, t), :], qbt, preferred_element_type=F32))

    key = lax.broadcasted_iota(jnp.int32, (t, t), 0)
    qry = lax.broadcasted_iota(jnp.int32, (t, t), 1)
    causal = key <= qry
    _softmax_blocks([jnp.where(causal, st, NEG_INF) for st in scores(i)], 0.0, vt_ref[i],
                    states, True)

    def body(j, carry):
        c = ((j - i) * t).astype(F32) * slope
        _softmax_blocks(scores(j), c, vt_ref[j], states, False)
        return carry

    lax.fori_loop(0, i, body, 0)

    lam = (jnp.exp(jnp.sum(lq1_ref[...] * lk1_ref[...], axis=-1, keepdims=True))
           - jnp.exp(jnp.sum(lq2_ref[...] * lk2_ref[...], axis=-1, keepdims=True))
           + lam_init)
    o = acc1_ref[...] * (1.0 / l1_ref[...]) - lam * (acc2_ref[...] * (1.0 / l2_ref[...]))
    ms = jnp.mean(o * o, axis=0, keepdims=True)
    o = o * lax.rsqrt(ms + SUBLN_EPS) * (1.0 - lam_init)
    o_ref[...] = (o.T * sg_ref[...]).astype(BF16)


def _attention(slopes, lq1, lk1, lq2, lk2, subln_g, qat, qbt, ka, kb, vt, *, batch, seq, lam_init):
    t = ATT_T
    nq = seq // t
    grid = (batch, N_HEADS, nq)
    qt_spec = pl.BlockSpec((None, HEAD_SLAB, t), lambda b, h, i: (b * nq + i, h, 0))
    k_spec = pl.BlockSpec((seq, HEAD_SLAB), lambda b, h, i: (b, h))
    vt_spec = pl.BlockSpec((nq, HEAD_SLAB, t), lambda b, h, i: (b, h, 0))
    o_spec = pl.BlockSpec((t, HEAD_SLAB), lambda b, h, i: (b * nq + i, h))
    whole = pl.BlockSpec(memory_space=pltpu.VMEM)
    smem = pl.BlockSpec(memory_space=pltpu.SMEM)
    stat = pltpu.VMEM((1, t), F32)
    acc = pltpu.VMEM((HEAD_SLAB, t), F32)
    return pl.pallas_call(
        functools.partial(_attn_kernel, t=t, lam_init=lam_init),
        grid=grid,
        in_specs=[smem, whole, whole, whole, whole, whole, qt_spec, qt_spec, k_spec, k_spec, vt_spec],
        out_specs=o_spec,
        out_shape=jax.ShapeDtypeStruct((batch * seq, ATT_WIDTH), BF16),
        scratch_shapes=[stat, stat, acc, stat, stat, acc],
        compiler_params=pltpu.CompilerParams(
            dimension_semantics=("arbitrary", "arbitrary", "arbitrary"),
            vmem_limit_bytes=VMEM_LIMIT_BYTES),
        name="diff_attn",
    )(slopes, lq1, lk1, lq2, lk2, subln_g, qat, qbt, ka, kb, vt)


def _ffn_kernel(x_ref, att_ref, sgu_ref, wo_ref, g2_ref, wup_ref, cw_ref, cb_ref, wdn_ref, fg_ref,
                o_ref, carry_ref, buf_ref, act_ref, *, tm, tiles_per_batch, final):
    fc = FFN_FC
    halo = SUBLANES

    @pl.when(pl.program_id(0) % tiles_per_batch == 0)
    def _():
        carry_ref[...] = jnp.zeros_like(carry_ref)

    x1 = (x_ref[...]
          + jnp.dot(att_ref[...], wo_ref[0:ATT_WIDTH, :], preferred_element_type=F32)
          + jnp.dot(sgu_ref[...], wo_ref[ATT_WIDTH:ATT_WIDTH + SGU_WIDTH, :],
                    preferred_element_type=F32))
    ms = jnp.mean(x1 * x1, axis=-1, keepdims=True)
    h2 = (x1 * lax.rsqrt(ms + NORM_EPS) * g2_ref[...]).astype(BF16)

    for c in range(D_FF // fc):
        convs = []
        for half in range(2):
            col0 = half * D_FF + c * fc
            up = jnp.dot(h2, wup_ref[:, col0:col0 + fc], preferred_element_type=F32)
            buf_ref[half, 0:halo, :] = carry_ref[half, c]
            buf_ref[half, halo:halo + tm, :] = up
            carry_ref[half, c] = up[tm - halo:tm, :]
            w0 = cw_ref[0:1, col0:col0 + fc]
            w1 = cw_ref[1:2, col0:col0 + fc]
            w2 = cw_ref[2:3, col0:col0 + fc]
            convs.append(cb_ref[:, col0:col0 + fc]
                         + w0 * buf_ref[half, halo - 2:halo - 2 + tm, :]
                         + w1 * buf_ref[half, halo - 1:halo - 1 + tm, :]
                         + w2 * up)
        act_ref[:, c * fc:(c + 1) * fc] = (_gelu_tanh(convs[0]) * convs[1]).astype(BF16)

    out = x1 + jnp.dot(act_ref[...], wdn_ref[...], preferred_element_type=F32)
    if final:
        ms = jnp.mean(out * out, axis=-1, keepdims=True)
        out = out * lax.rsqrt(ms + NORM_EPS) * fg_ref[...]
    o_ref[...] = out


def _ffn(x2d, att, sgu, w_out, g2, w_up, conv_w, conv_b, w_down, final_g, *, seq, final):
    rows = x2d.shape[0]
    tm = FFN_TM
    grid = (rows // tm,)
    row_spec = lambda width: pl.BlockSpec((tm, width), lambda i: (i, 0))
    whole = pl.BlockSpec(memory_space=pltpu.VMEM)
    return pl.pallas_call(
        functools.partial(_ffn_kernel, tm=tm, tiles_per_batch=seq // tm, final=final),
        grid=grid,
        in_specs=[row_spec(D_MODEL), row_spec(ATT_WIDTH), row_spec(SGU_WIDTH),
                  whole, whole, whole, whole, whole, whole, whole],
        out_specs=row_spec(D_MODEL),
        out_shape=jax.ShapeDtypeStruct((rows, D_MODEL), F32),
        scratch_shapes=[
            pltpu.VMEM((2, D_FF // FFN_FC, SUBLANES, FFN_FC), F32),
            pltpu.VMEM((2, SUBLANES + tm, FFN_FC), F32),
            pltpu.VMEM((tm, D_FF), BF16),
        ],
        compiler_params=pltpu.CompilerParams(
            dimension_semantics=("arbitrary",), vmem_limit_bytes=VMEM_LIMIT_BYTES),
        name="ffn",
    )(x2d, att, sgu, w_out, g2, w_up, conv_w, conv_b, w_down, final_g)


def kernel(x, norm1_g, w_in, lam_q1, lam_k1, lam_q2, lam_k2, subln_g, sgu_ln_g, sgu_ln_b,
           sgu_w, sgu_b, w_out, norm2_g, ffn_w_up, ffn_conv_w, ffn_conv_b, ffn_w_down, final_g):
    batch, seq, _ = x.shape
    rows = batch * seq
    x2d = x.reshape(rows, D_MODEL)
    slopes = jnp.asarray([_alibi_slope(h) * LOG2_E for h in range(N_HEADS)], dtype=F32)
    row1 = lambda a: a.reshape(1, -1).astype(F32)
    for l in range(DEPTH):
        lam_init = _lambda_init(l)
        sgu_b_full = jnp.broadcast_to(sgu_b[l].astype(F32)[:, :, None],
                                      (N_GROUPS, SGU_CHUNK, GROUP_DIM))
        qat, qbt, ka, kb, vt, sgu = _mix_in(x2d, row1(norm1_g[l]), w_in[l].astype(BF16),
                                            row1(sgu_ln_g[l]), row1(sgu_ln_b[l]),
                                            sgu_w[l].astype(F32), sgu_b_full)
        att = _attention(slopes, row1(lam_q1[l]), row1(lam_k1[l]), row1(lam_q2[l]), row1(lam_k2[l]),
                         row1(subln_g[l]), qat, qbt, ka, kb, vt,
                         batch=batch, seq=seq, lam_init=lam_init)
        x2d = _ffn(x2d, att, sgu, w_out[l].astype(BF16), row1(norm2_g[l]),
                   ffn_w_up[l].astype(BF16), ffn_conv_w[l].astype(F32), row1(ffn_conv_b[l]),
                   ffn_w_down[l].astype(BF16), row1(final_g), seq=seq, final=(l == DEPTH - 1))
    return x2d.reshape(batch, seq, D_MODEL)
```

```python
import functools
import math

import jax
import jax.numpy as jnp
import numpy as np
from jax import lax
from jax.experimental import pallas as pl
from jax.experimental.pallas import tpu as pltpu

F32 = jnp.float32
BF16 = jnp.bfloat16

D_MODEL = 1024
DEPTH = 2
N_HEADS = 4
HEAD_DIM = 64
HEAD_SLAB = 2 * HEAD_DIM
ATT_WIDTH = N_HEADS * HEAD_SLAB
N_GROUPS = 4
SGU_CHUNK = 128
GROUP_DIM = 128
SGU_WIDTH = N_GROUPS * GROUP_DIM
D_FF = 2816
CONV_WIDTH = 3
NORM_EPS = 1e-6
SUBLN_EPS = 1e-5
LN_EPS = 1e-5
NEG_INF = -1e30
LOG2_E = math.log2(math.e)

SUBLANES = 8
BF16_EXACT_INT = 256
VMEM_LIMIT_BYTES = 56 * 1024 * 1024

MIX_TM = 512
ATT_T = 512
FFN_TM = 512
FFN_FC = 256


def _gelu_tanh(x):
    c = math.sqrt(2.0 / math.pi)
    return 0.5 * x * (1.0 + jnp.tanh(c * (x + 0.044715 * (x * x * x))))


def _alibi_slope(head):
    return 2.0 ** (-8.0 * (head + 1) / N_HEADS)


def _bf16_terms(value, n=3):
    terms = []
    rest = value
    for _ in range(n):
        term = float(np.asarray(rest, dtype=BF16).astype(np.float32))
        terms.append(term)
        rest -= term
    return terms


def _lambda_init(layer_idx):
    return 0.8 - 0.6 * math.exp(-0.3 * layer_idx)


def _mixin_kernel(x_ref, g_ref, w_ref, lng_ref, lnb_ref, sw_ref, sb_ref,
                  qat_ref, qbt_ref, ka_ref, kb_ref, vt_ref, sgu_ref, *, tm, t):
    x = x_ref[...]
    ms = jnp.mean(x * x, axis=-1, keepdims=True)
    hb = (x * lax.rsqrt(ms + NORM_EPS) * g_ref[...]).astype(BF16)

    def proj(c0, width):
        return jnp.dot(hb, w_ref[:, c0:c0 + width], preferred_element_type=F32)

    zq = proj(0, ATT_WIDTH) * (HEAD_DIM ** -0.5 * LOG2_E)
    zk = proj(ATT_WIDTH, ATT_WIDTH)
    zv = proj(2 * ATT_WIDTH, ATT_WIDTH)
    lane = lax.broadcasted_iota(jnp.int32, (tm, HEAD_SLAB), 1)
    pos = lax.broadcasted_iota(jnp.int32, (tm, HEAD_SLAB), 0) % t
    pos_lo = (pos % BF16_EXACT_INT).astype(F32)
    pos_hi = (pos - pos % BF16_EXACT_INT).astype(F32)
    in_a = lane < HEAD_DIM

    def aug(first_lane, values):
        out = jnp.zeros((tm, HEAD_SLAB), F32)
        for n, val in enumerate(values):
            out = jnp.where(lane == first_lane + n, val, out)
        return out

    for h in range(N_HEADS):
        lo, hi = h * HEAD_SLAB, (h + 1) * HEAD_SLAB
        factor = _bf16_terms(_alibi_slope(h) * LOG2_E)
        positions = [pos_lo] * len(factor) + [pos_hi] * len(factor)
        qs, ks = zq[:, lo:hi], zk[:, lo:hi]
        ka_ref[:, lo:hi] = jnp.where(in_a, ks, aug(HEAD_DIM, positions)).astype(BF16)
        kb_ref[:, lo:hi] = jnp.where(in_a, aug(0, positions), ks).astype(BF16)
        qat = jnp.where(in_a, qs, aug(HEAD_DIM, factor + factor)).T.astype(BF16)
        qbt = jnp.where(in_a, aug(0, factor + factor), qs).T.astype(BF16)
        vt = zv[:, lo:hi].T.astype(BF16)
        for c in range(tm // t):
            qat_ref[c, lo:hi, :] = qat[:, c * t:(c + 1) * t]
            qbt_ref[c, lo:hi, :] = qbt[:, c * t:(c + 1) * t]
            vt_ref[c, lo:hi, :] = vt[:, c * t:(c + 1) * t]

    u = _gelu_tanh(proj(3 * ATT_WIDTH, SGU_WIDTH))
    vg = _gelu_tanh(proj(3 * ATT_WIDTH + SGU_WIDTH, SGU_WIDTH))
    row = lax.broadcasted_iota(jnp.int32, (SGU_CHUNK, SGU_CHUNK), 0)
    col = lax.broadcasted_iota(jnp.int32, (SGU_CHUNK, SGU_CHUNK), 1)
    tril = col <= row
    for g in range(N_GROUPS):
        lo, hi = g * GROUP_DIM, (g + 1) * GROUP_DIM
        y = vg[:, lo:hi]
        mu = jnp.mean(y, axis=-1, keepdims=True)
        yc = y - mu
        var = jnp.mean(yc * yc, axis=-1, keepdims=True)
        yn = (yc * lax.rsqrt(var + LN_EPS) * lng_ref[:, lo:hi] + lnb_ref[:, lo:hi]).astype(BF16)
        wm = jnp.where(tril, sw_ref[g], 0.0).astype(BF16)
        bias = sb_ref[g]
        for c in range(tm // SGU_CHUNK):
            r0, r1 = c * SGU_CHUNK, (c + 1) * SGU_CHUNK
            vmix = jnp.dot(wm, yn[r0:r1, :], preferred_element_type=F32) + bias
            sgu_ref[r0:r1, lo:hi] = (u[r0:r1, lo:hi] * vmix).astype(BF16)


def _mix_in(x2d, g, w_in, ln_g, ln_b, sgu_w, sgu_b_full):
    rows = x2d.shape[0]
    tm, t = MIX_TM, ATT_T
    grid = (rows // tm,)
    row_spec = lambda width: pl.BlockSpec((tm, width), lambda i: (i, 0))
    tr_spec = pl.BlockSpec((tm // t, ATT_WIDTH, t), lambda i: (i, 0, 0))
    whole = pl.BlockSpec(memory_space=pltpu.VMEM)
    row_sds = jax.ShapeDtypeStruct((rows, ATT_WIDTH), BF16)
    tr_sds = jax.ShapeDtypeStruct((rows // t, ATT_WIDTH, t), BF16)
    return pl.pallas_call(
        functools.partial(_mixin_kernel, tm=tm, t=t),
        grid=grid,
        in_specs=[row_spec(D_MODEL), whole, whole, whole, whole, whole, whole],
        out_specs=[tr_spec, tr_spec, row_spec(ATT_WIDTH), row_spec(ATT_WIDTH), tr_spec,
                   row_spec(SGU_WIDTH)],
        out_shape=[tr_sds, tr_sds, row_sds, row_sds, tr_sds, row_sds],
        compiler_params=pltpu.CompilerParams(
            dimension_semantics=("arbitrary",), vmem_limit_bytes=VMEM_LIMIT_BYTES),
        name="mix_in",
    )(x2d, g, w_in, ln_g, ln_b, sgu_w, sgu_b_full)


N_MAPS = 2


def _score_block(j, masked, q_refs, k_refs, s_ref, bm_ref, *, t):
    j0 = pl.multiple_of(j * t, t)
    if masked:
        key = lax.broadcasted_iota(jnp.int32, (t, t), 0)
        qry = lax.broadcasted_iota(jnp.int32, (t, t), 1)
        causal = key <= qry
    for mp in range(N_MAPS):
        st = jnp.dot(k_refs[mp][pl.ds(j0, t), :], q_refs[mp][...], preferred_element_type=F32)
        if masked:
            st = jnp.where(causal, st, NEG_INF)
        s_ref[mp] = st
        bm_ref[mp] = jnp.max(st, axis=0, keepdims=True)


def _softmax_block(c, vt_blk, s_ref, bm_ref, states):
    for mp, (m_ref, l_ref, acc_ref) in enumerate(states):
        m_old = m_ref[...]
        m_new = jnp.maximum(m_old, bm_ref[mp] + c)
        p = jnp.exp2(s_ref[mp] - (m_new - c))
        ps = jnp.sum(p, axis=0, keepdims=True)
        pv = jnp.dot(vt_blk, p.astype(BF16), preferred_element_type=F32)
        alpha = jnp.exp2(m_old - m_new)
        l_ref[...] = alpha * l_ref[...] + ps
        acc_ref[...] = alpha * acc_ref[...] + pv
        m_ref[...] = m_new


def _attn_kernel(slopes_ref, lq1_ref, lk1_ref, lq2_ref, lk2_ref, sg_ref,
                 qat_ref, qbt_ref, ka_ref, kb_ref, vt_ref, o_ref,
                 m1_ref, l1_ref, acc1_ref, m2_ref, l2_ref, acc2_ref,
                 sa_ref, bma_ref, sb_ref, bmb_ref, *, t, lam_init):
    h = pl.program_id(1)
    i = pl.program_id(2)
    slope = slopes_ref[h]
    states = ((m1_ref, l1_ref, acc1_ref), (m2_ref, l2_ref, acc2_ref))
    score = functools.partial(_score_block, q_refs=(qat_ref, qbt_ref), k_refs=(ka_ref, kb_ref), t=t)

    def softmax(j, s_ref, bm_ref):
        _softmax_block(((j - i) * t).astype(F32) * slope, vt_ref[j], s_ref, bm_ref, states)

    for m_ref, l_ref, acc_ref in states:
        m_ref[...] = jnp.full_like(m_ref, NEG_INF)
        l_ref[...] = jnp.zeros_like(l_ref)
        acc_ref[...] = jnp.zeros_like(acc_ref)

    score(i, True, s_ref=sa_ref, bm_ref=bma_ref)

    def pair(n, carry):
        j = 2 * n
        score(j, False, s_ref=sb_ref, bm_ref=bmb_ref)
        softmax(jnp.where(n == 0, i, j - 1), sa_ref, bma_ref)
        score(j + 1, False, s_ref=sa_ref, bm_ref=bma_ref)
        softmax(j, sb_ref, bmb_ref)
        return carry

    lax.fori_loop(0, i // 2, pair, 0)
    pending = jnp.where(i < 2, i, 2 * (i // 2) - 1)

    @pl.when(i % 2 == 1)
    def _():
        score(i - 1, False, s_ref=sb_ref, bm_ref=bmb_ref)
        softmax(pending, sa_ref, bma_ref)
        softmax(i - 1, sb_ref, bmb_ref)

    @pl.when(i % 2 == 0)
    def _():
        softmax(pending, sa_ref, bma_ref)

    lam = (jnp.exp(jnp.sum(lq1_ref[...] * lk1_ref[...], axis=-1, keepdims=True))
           - jnp.exp(jnp.sum(lq2_ref[...] * lk2_ref[...], axis=-1, keepdims=True))
           + lam_init)
    o = acc1_ref[...] * (1.0 / l1_ref[...]) - lam * (acc2_ref[...] * (1.0 / l2_ref[...]))
    ms = jnp.mean(o * o, axis=0, keepdims=True)
    o = o * lax.rsqrt(ms + SUBLN_EPS) * (1.0 - lam_init)
    o_ref[...] = (o.T * sg_ref[...]).astype(BF16)


def _attention(slopes, lq1, lk1, lq2, lk2, subln_g, qat, qbt, ka, kb, vt, *, batch, seq, lam_init):
    t = ATT_T
    nq = seq // t
    grid = (batch, N_HEADS, nq)
    qt_spec = pl.BlockSpec((None, HEAD_SLAB, t), lambda b, h, i: (b * nq + i, h, 0))
    k_spec = pl.BlockSpec((seq, HEAD_SLAB), lambda b, h, i: (b, h))
    vt_spec = pl.BlockSpec((nq, HEAD_SLAB, t), lambda b, h, i: (b, h, 0))
    o_spec = pl.BlockSpec((t, HEAD_SLAB), lambda b, h, i: (b * nq + i, h))
    whole = pl.BlockSpec(memory_space=pltpu.VMEM)
    smem = pl.BlockSpec(memory_space=pltpu.SMEM)
    stat = pltpu.VMEM((1, t), F32)
    acc = pltpu.VMEM((HEAD_SLAB, t), F32)
    score_buf = pltpu.VMEM((N_MAPS, t, t), F32)
    colmax_buf = pltpu.VMEM((N_MAPS, 1, t), F32)
    return pl.pallas_call(
        functools.partial(_attn_kernel, t=t, lam_init=lam_init),
        grid=grid,
        in_specs=[smem, whole, whole, whole, whole, whole, qt_spec, qt_spec, k_spec, k_spec, vt_spec],
        out_specs=o_spec,
        out_shape=jax.ShapeDtypeStruct((batch * seq, ATT_WIDTH), BF16),
        scratch_shapes=[stat, stat, acc, stat, stat, acc,
                        score_buf, colmax_buf, score_buf, colmax_buf],
        compiler_params=pltpu.CompilerParams(
            dimension_semantics=("arbitrary", "arbitrary", "arbitrary"),
            vmem_limit_bytes=VMEM_LIMIT_BYTES),
        name="diff_attn",
    )(slopes, lq1, lk1, lq2, lk2, subln_g, qat, qbt, ka, kb, vt)


def _ffn_kernel(x_ref, att_ref, sgu_ref, wo_ref, g2_ref, wup_ref, cw_ref, cb_ref, wdn_ref, fg_ref,
                o_ref, carry_ref, buf_ref, act_ref, *, tm, tiles_per_batch, final):
    fc = FFN_FC
    halo = SUBLANES

    @pl.when(pl.program_id(0) % tiles_per_batch == 0)
    def _():
        carry_ref[...] = jnp.zeros_like(carry_ref)

    x1 = (x_ref[...]
          + jnp.dot(att_ref[...], wo_ref[0:ATT_WIDTH, :], preferred_element_type=F32)
          + jnp.dot(sgu_ref[...], wo_ref[ATT_WIDTH:ATT_WIDTH + SGU_WIDTH, :],
                    preferred_element_type=F32))
    ms = jnp.mean(x1 * x1, axis=-1, keepdims=True)
    h2 = (x1 * lax.rsqrt(ms + NORM_EPS) * g2_ref[...]).astype(BF16)

    for c in range(D_FF // fc):
        convs = []
        for half in range(2):
            col0 = half * D_FF + c * fc
            up = jnp.dot(h2, wup_ref[:, col0:col0 + fc], preferred_element_type=F32)
            buf_ref[half, 0:halo, :] = carry_ref[half, c]
            buf_ref[half, halo:halo + tm, :] = up
            carry_ref[half, c] = up[tm - halo:tm, :]
            w0 = cw_ref[0:1, col0:col0 + fc]
            w1 = cw_ref[1:2, col0:col0 + fc]
            w2 = cw_ref[2:3, col0:col0 + fc]
            convs.append(cb_ref[:, col0:col0 + fc]
                         + w0 * buf_ref[half, halo - 2:halo - 2 + tm, :]
                         + w1 * buf_ref[half, halo - 1:halo - 1 + tm, :]
                         + w2 * up)
        act_ref[:, c * fc:(c + 1) * fc] = (_gelu_tanh(convs[0]) * convs[1]).astype(BF16)

    out = x1 + jnp.dot(act_ref[...], wdn_ref[...], preferred_element_type=F32)
    if final:
        ms = jnp.mean(out * out, axis=-1, keepdims=True)
        out = out * lax.rsqrt(ms + NORM_EPS) * fg_ref[...]
    o_ref[...] = out


def _ffn(x2d, att, sgu, w_out, g2, w_up, conv_w, conv_b, w_down, final_g, *, seq, final):
    rows = x2d.shape[0]
    tm = FFN_TM
    grid = (rows // tm,)
    row_spec = lambda width: pl.BlockSpec((tm, width), lambda i: (i, 0))
    whole = pl.BlockSpec(memory_space=pltpu.VMEM)
    return pl.pallas_call(
        functools.partial(_ffn_kernel, tm=tm, tiles_per_batch=seq // tm, final=final),
        grid=grid,
        in_specs=[row_spec(D_MODEL), row_spec(ATT_WIDTH), row_spec(SGU_WIDTH),
                  whole, whole, whole, whole, whole, whole, whole],
        out_specs=row_spec(D_MODEL),
        out_shape=jax.ShapeDtypeStruct((rows, D_MODEL), F32),
        scratch_shapes=[
            pltpu.VMEM((2, D_FF // FFN_FC, SUBLANES, FFN_FC), F32),
            pltpu.VMEM((2, SUBLANES + tm, FFN_FC), F32),
            pltpu.VMEM((tm, D_FF), BF16),
        ],
        compiler_params=pltpu.CompilerParams(
            dimension_semantics=("arbitrary",), vmem_limit_bytes=VMEM_LIMIT_BYTES),
        name="ffn",
    )(x2d, att, sgu, w_out, g2, w_up, conv_w, conv_b, w_down, final_g)


def kernel(x, norm1_g, w_in, lam_q1, lam_k1, lam_q2, lam_k2, subln_g, sgu_ln_g, sgu_ln_b,
           sgu_w, sgu_b, w_out, norm2_g, ffn_w_up, ffn_conv_w, ffn_conv_b, ffn_w_down, final_g):
    batch, seq, _ = x.shape
    rows = batch * seq
    x2d = x.reshape(rows, D_MODEL)
    slopes = jnp.asarray([_alibi_slope(h) * LOG2_E for h in range(N_HEADS)], dtype=F32)
    row1 = lambda a: a.reshape(1, -1).astype(F32)
    for l in range(DEPTH):
        lam_init = _lambda_init(l)
        sgu_b_full = jnp.broadcast_to(sgu_b[l].astype(F32)[:, :, None],
                                      (N_GROUPS, SGU_CHUNK, GROUP_DIM))
        qat, qbt, ka, kb, vt, sgu = _mix_in(x2d, row1(norm1_g[l]), w_in[l].astype(BF16),
                                            row1(sgu_ln_g[l]), row1(sgu_ln_b[l]),
                                            sgu_w[l].astype(F32), sgu_b_full)
        att = _attention(slopes, row1(lam_q1[l]), row1(lam_k1[l]), row1(lam_q2[l]), row1(lam_k2[l]),
                         row1(subln_g[l]), qat, qbt, ka, kb, vt,
                         batch=batch, seq=seq, lam_init=lam_init)
        x2d = _ffn(x2d, att, sgu, w_out[l].astype(BF16), row1(norm2_g[l]),
                   ffn_w_up[l].astype(BF16), ffn_conv_w[l].astype(F32), row1(ffn_conv_b[l]),
                   ffn_w_down[l].astype(BF16), row1(final_g), seq=seq, final=(l == DEPTH - 1))
    return x2d.reshape(batch, seq, D_MODEL)
```

```python
import functools
import math

import jax
import jax.numpy as jnp
import numpy as np
from jax import lax
from jax.experimental import pallas as pl
from jax.experimental.pallas import tpu as pltpu

F32 = jnp.float32
BF16 = jnp.bfloat16

D_MODEL = 1024
DEPTH = 2
N_HEADS = 4
HEAD_DIM = 64
HEAD_SLAB = 2 * HEAD_DIM
ATT_WIDTH = N_HEADS * HEAD_SLAB
N_GROUPS = 4
SGU_CHUNK = 128
GROUP_DIM = 128
SGU_WIDTH = N_GROUPS * GROUP_DIM
D_FF = 2816
CONV_WIDTH = 3
NORM_EPS = 1e-6
SUBLN_EPS = 1e-5
LN_EPS = 1e-5
NEG_INF = -1e30
LOG2_E = math.log2(math.e)

SUBLANES = 8
BF16_EXACT_INT = 256
VMEM_LIMIT_BYTES = 56 * 1024 * 1024

MIX_TM = 512
ATT_T = 512
FFN_TM = 512
FFN_FC = 256


def _gelu_tanh(x):
    c = math.sqrt(2.0 / math.pi)
    return 0.5 * x * (1.0 + jnp.tanh(c * (x + 0.044715 * (x * x * x))))


def _alibi_slope(head):
    return 2.0 ** (-8.0 * (head + 1) / N_HEADS)


def _bf16_terms(value, n=3):
    terms = []
    rest = value
    for _ in range(n):
        term = float(np.asarray(rest, dtype=BF16).astype(np.float32))
        terms.append(term)
        rest -= term
    return terms


def _lambda_init(layer_idx):
    return 0.8 - 0.6 * math.exp(-0.3 * layer_idx)


def _mixin_kernel(x_ref, g_ref, w_ref, lng_ref, lnb_ref, sw_ref, sb_ref,
                  qat_ref, qbt_ref, ka_ref, kb_ref, vt_ref, sgu_ref, *, tm, t):
    x = x_ref[...]
    ms = jnp.mean(x * x, axis=-1, keepdims=True)
    hb = (x * lax.rsqrt(ms + NORM_EPS) * g_ref[...]).astype(BF16)

    def proj(c0, width):
        return jnp.dot(hb, w_ref[:, c0:c0 + width], preferred_element_type=F32)

    zq = proj(0, ATT_WIDTH) * (HEAD_DIM ** -0.5 * LOG2_E)
    zk = proj(ATT_WIDTH, ATT_WIDTH)
    zv = proj(2 * ATT_WIDTH, ATT_WIDTH)
    lane = lax.broadcasted_iota(jnp.int32, (tm, HEAD_SLAB), 1)
    pos = lax.broadcasted_iota(jnp.int32, (tm, HEAD_SLAB), 0) % t
    pos_lo = (pos % BF16_EXACT_INT).astype(F32)
    pos_hi = (pos - pos % BF16_EXACT_INT).astype(F32)
    in_a = lane < HEAD_DIM

    def aug(first_lane, values):
        out = jnp.zeros((tm, HEAD_SLAB), F32)
        for n, val in enumerate(values):
            out = jnp.where(lane == first_lane + n, val, out)
        return out

    for h in range(N_HEADS):
        lo, hi = h * HEAD_SLAB, (h + 1) * HEAD_SLAB
        factor = _bf16_terms(_alibi_slope(h) * LOG2_E)
        positions = [pos_lo] * len(factor) + [pos_hi] * len(factor)
        qs, ks = zq[:, lo:hi], zk[:, lo:hi]
        ka_ref[:, lo:hi] = jnp.where(in_a, ks, aug(HEAD_DIM, positions)).astype(BF16)
        kb_ref[:, lo:hi] = jnp.where(in_a, aug(0, positions), ks).astype(BF16)
        qat = jnp.where(in_a, qs, aug(HEAD_DIM, factor + factor)).T.astype(BF16)
        qbt = jnp.where(in_a, aug(0, factor + factor), qs).T.astype(BF16)
        vt = zv[:, lo:hi].T.astype(BF16)
        for c in range(tm // t):
            qat_ref[c, lo:hi, :] = qat[:, c * t:(c + 1) * t]
            qbt_ref[c, lo:hi, :] = qbt[:, c * t:(c + 1) * t]
            vt_ref[c, lo:hi, :] = vt[:, c * t:(c + 1) * t]

    u = _gelu_tanh(proj(3 * ATT_WIDTH, SGU_WIDTH))
    vg = _gelu_tanh(proj(3 * ATT_WIDTH + SGU_WIDTH, SGU_WIDTH))
    row = lax.broadcasted_iota(jnp.int32, (SGU_CHUNK, SGU_CHUNK), 0)
    col = lax.broadcasted_iota(jnp.int32, (SGU_CHUNK, SGU_CHUNK), 1)
    tril = col <= row
    for g in range(N_GROUPS):
        lo, hi = g * GROUP_DIM, (g + 1) * GROUP_DIM
        y = vg[:, lo:hi]
        mu = jnp.mean(y, axis=-1, keepdims=True)
        yc = y - mu
        var = jnp.mean(yc * yc, axis=-1, keepdims=True)
        yn = (yc * lax.rsqrt(var + LN_EPS) * lng_ref[:, lo:hi] + lnb_ref[:, lo:hi]).astype(BF16)
        wm = jnp.where(tril, sw_ref[g], 0.0).astype(BF16)
        bias = sb_ref[g]
        for c in range(tm // SGU_CHUNK):
            r0, r1 = c * SGU_CHUNK, (c + 1) * SGU_CHUNK
            vmix = jnp.dot(wm, yn[r0:r1, :], preferred_element_type=F32) + bias
            sgu_ref[r0:r1, lo:hi] = (u[r0:r1, lo:hi] * vmix).astype(BF16)


def _mix_in(x2d, g, w_in, ln_g, ln_b, sgu_w, sgu_b_full):
    rows = x2d.shape[0]
    tm, t = MIX_TM, ATT_T
    grid = (rows // tm,)
    row_spec = lambda width: pl.BlockSpec((tm, width), lambda i: (i, 0))
    tr_spec = pl.BlockSpec((tm // t, ATT_WIDTH, t), lambda i: (i, 0, 0))
    whole = pl.BlockSpec(memory_space=pltpu.VMEM)
    row_sds = jax.ShapeDtypeStruct((rows, ATT_WIDTH), BF16)
    tr_sds = jax.ShapeDtypeStruct((rows // t, ATT_WIDTH, t), BF16)
    return pl.pallas_call(
        functools.partial(_mixin_kernel, tm=tm, t=t),
        grid=grid,
        in_specs=[row_spec(D_MODEL), whole, whole, whole, whole, whole, whole],
        out_specs=[tr_spec, tr_spec, row_spec(ATT_WIDTH), row_spec(ATT_WIDTH), tr_spec,
                   row_spec(SGU_WIDTH)],
        out_shape=[tr_sds, tr_sds, row_sds, row_sds, tr_sds, row_sds],
        compiler_params=pltpu.CompilerParams(
            dimension_semantics=("arbitrary",), vmem_limit_bytes=VMEM_LIMIT_BYTES),
        name="mix_in",
    )(x2d, g, w_in, ln_g, ln_b, sgu_w, sgu_b_full)


N_MAPS = 2


def _score_block(qi, kj, masked, q_refs, k_refs, s_ref, bm_ref, *, t):
    k0 = pl.multiple_of(kj * t, t)
    if masked:
        key = lax.broadcasted_iota(jnp.int32, (t, t), 0)
        qry = lax.broadcasted_iota(jnp.int32, (t, t), 1)
        causal = key <= qry
    for mp in range(N_MAPS):
        st = jnp.dot(k_refs[mp][pl.ds(k0, t), :], q_refs[mp][qi], preferred_element_type=F32)
        if masked:
            st = jnp.where(causal, st, NEG_INF)
        s_ref[mp] = st
        bm_ref[mp] = jnp.max(st, axis=0, keepdims=True)


def _softmax_block(qi, c, vt_blk, s_ref, bm_ref, m_ref, l_ref, acc_ref, first):
    for mp in range(N_MAPS):
        bm = bm_ref[mp] + c
        m_new = bm if first else jnp.maximum(m_ref[qi, mp], bm)
        p = jnp.exp2(s_ref[mp] - (m_new - c))
        ps = jnp.sum(p, axis=0, keepdims=True)
        pv = jnp.dot(vt_blk, p.astype(BF16), preferred_element_type=F32)
        if first:
            l_ref[qi, mp] = ps
            acc_ref[qi, mp] = pv
        else:
            alpha = jnp.exp2(m_ref[qi, mp] - m_new)
            l_ref[qi, mp] = alpha * l_ref[qi, mp] + ps
            acc_ref[qi, mp] = alpha * acc_ref[qi, mp] + pv
        m_ref[qi, mp] = m_new


def _attn_kernel(pair_q_ref, pair_k_ref, pair_n_ref,
                 slopes_ref, lq1_ref, lk1_ref, lq2_ref, lk2_ref, sg_ref,
                 qat_ref, qbt_ref, ka_ref, kb_ref, vt_ref, o_ref,
                 m_ref, l_ref, acc_ref, sa_ref, bma_ref, sb_ref, bmb_ref, *, t, nq, lam_init):
    bh = pl.program_id(0) * N_HEADS + pl.program_id(1)
    slope = slopes_ref[pl.program_id(1)]
    n_pairs = pair_n_ref[bh]
    score = functools.partial(_score_block, q_refs=(qat_ref, qbt_ref), k_refs=(ka_ref, kb_ref), t=t)
    softmax = functools.partial(_softmax_block, m_ref=m_ref, l_ref=l_ref, acc_ref=acc_ref)
    buf_a = dict(s_ref=sa_ref, bm_ref=bma_ref)
    buf_b = dict(s_ref=sb_ref, bm_ref=bmb_ref)

    score(0, 0, True, **buf_a)

    def diag_pair(n, carry):
        qi = 2 * n
        score(qi + 1, qi + 1, True, **buf_b)
        softmax(qi, 0.0, vt_ref[qi], first=True, **buf_a)
        nxt = jnp.minimum(qi + 2, nq - 1)
        score(nxt, nxt, True, **buf_a)
        softmax(qi + 1, 0.0, vt_ref[qi + 1], first=True, **buf_b)
        return carry

    lax.fori_loop(0, nq // 2, diag_pair, 0)

    def entry(e):
        qi, kj = pair_q_ref[bh, e], pair_k_ref[bh, e]
        c = jnp.where(e < n_pairs, ((kj - qi) * t).astype(F32) * slope, NEG_INF)
        return qi, kj, c

    def score_entry(e, buf):
        qi, kj, _ = entry(e)
        score(qi, kj, False, **buf)

    def softmax_entry(e, buf):
        qi, kj, c = entry(e)
        softmax(qi, c, vt_ref[kj], first=False, **buf)

    score_entry(0, buf_a)

    def off_pair(n, carry):
        e = 2 * n
        score_entry(e + 1, buf_b)
        softmax_entry(e, buf_a)
        score_entry(e + 2, buf_a)
        softmax_entry(e + 1, buf_b)
        return carry

    lax.fori_loop(0, (n_pairs + 1) // 2, off_pair, 0)

    lam = (jnp.exp(jnp.sum(lq1_ref[...] * lk1_ref[...], axis=-1, keepdims=True))
           - jnp.exp(jnp.sum(lq2_ref[...] * lk2_ref[...], axis=-1, keepdims=True))
           + lam_init)

    def finish(qi, carry):
        o = (acc_ref[qi, 0] * (1.0 / l_ref[qi, 0])
             - lam * (acc_ref[qi, 1] * (1.0 / l_ref[qi, 1])))
        ms = jnp.mean(o * o, axis=0, keepdims=True)
        o = o * lax.rsqrt(ms + SUBLN_EPS) * (1.0 - lam_init)
        o_ref[pl.ds(pl.multiple_of(qi * t, t), t), :] = (o.T * sg_ref[...]).astype(BF16)
        return carry

    lax.fori_loop(0, nq, finish, 0)


def _attention(pair_q, pair_k, pair_n, slopes, lq1, lk1, lq2, lk2, subln_g, qat, qbt, ka, kb, vt,
               *, batch, seq, lam_init):
    t = ATT_T
    nq = seq // t
    qt_spec = pl.BlockSpec((nq, HEAD_SLAB, t), lambda b, h, *_: (b, h, 0))
    row_spec = pl.BlockSpec((seq, HEAD_SLAB), lambda b, h, *_: (b, h))
    whole = pl.BlockSpec(memory_space=pltpu.VMEM)
    smem = pl.BlockSpec(memory_space=pltpu.SMEM)
    stat = pltpu.VMEM((nq, N_MAPS, 1, t), F32)
    acc = pltpu.VMEM((nq, N_MAPS, HEAD_SLAB, t), F32)
    score_buf = pltpu.VMEM((N_MAPS, t, t), F32)
    colmax_buf = pltpu.VMEM((N_MAPS, 1, t), F32)
    return pl.pallas_call(
        functools.partial(_attn_kernel, t=t, nq=nq, lam_init=lam_init),
        grid_spec=pltpu.PrefetchScalarGridSpec(
            num_scalar_prefetch=3,
            grid=(batch, N_HEADS),
            in_specs=[smem, whole, whole, whole, whole, whole,
                      qt_spec, qt_spec, row_spec, row_spec, qt_spec],
            out_specs=row_spec,
            scratch_shapes=[stat, stat, acc, score_buf, colmax_buf, score_buf, colmax_buf]),
        out_shape=jax.ShapeDtypeStruct((batch * seq, ATT_WIDTH), BF16),
        compiler_params=pltpu.CompilerParams(
            dimension_semantics=("arbitrary", "arbitrary"),
            vmem_limit_bytes=VMEM_LIMIT_BYTES),
        name="diff_attn",
    )(pair_q, pair_k, pair_n, slopes, lq1, lk1, lq2, lk2, subln_g, qat, qbt, ka, kb, vt)


def _off_diagonal_pairs(batch, seq):
    nq = seq // ATT_T
    pairs = [(i, j) for i in range(nq) for j in range(i)]
    pad = [(0, 0)] * 2
    pair_q = np.asarray([p[0] for p in pairs + pad], np.int32)
    pair_k = np.asarray([p[1] for p in pairs + pad], np.int32)
    tile = lambda a: jnp.asarray(np.tile(a[None], (batch * N_HEADS, 1)))
    return tile(pair_q), tile(pair_k), jnp.full((batch * N_HEADS,), len(pairs), jnp.int32)


def _ffn_kernel(x_ref, att_ref, sgu_ref, wo_ref, g2_ref, wup_ref, cw_ref, cb_ref, wdn_ref, fg_ref,
                o_ref, carry_ref, buf_ref, act_ref, *, tm, tiles_per_batch, final):
    fc = FFN_FC
    halo = SUBLANES

    @pl.when(pl.program_id(0) % tiles_per_batch == 0)
    def _():
        carry_ref[...] = jnp.zeros_like(carry_ref)

    x1 = (x_ref[...]
          + jnp.dot(att_ref[...], wo_ref[0:ATT_WIDTH, :], preferred_element_type=F32)
          + jnp.dot(sgu_ref[...], wo_ref[ATT_WIDTH:ATT_WIDTH + SGU_WIDTH, :],
                    preferred_element_type=F32))
    ms = jnp.mean(x1 * x1, axis=-1, keepdims=True)
    h2 = (x1 * lax.rsqrt(ms + NORM_EPS) * g2_ref[...]).astype(BF16)

    for c in range(D_FF // fc):
        convs = []
        for half in range(2):
            col0 = half * D_FF + c * fc
            up = jnp.dot(h2, wup_ref[:, col0:col0 + fc], preferred_element_type=F32)
            buf_ref[half, 0:halo, :] = carry_ref[half, c]
            buf_ref[half, halo:halo + tm, :] = up
            carry_ref[half, c] = up[tm - halo:tm, :]
            w0 = cw_ref[0:1, col0:col0 + fc]
            w1 = cw_ref[1:2, col0:col0 + fc]
            w2 = cw_ref[2:3, col0:col0 + fc]
            convs.append(cb_ref[:, col0:col0 + fc]
                         + w0 * buf_ref[half, halo - 2:halo - 2 + tm, :]
                         + w1 * buf_ref[half, halo - 1:halo - 1 + tm, :]
                         + w2 * up)
        act_ref[:, c * fc:(c + 1) * fc] = (_gelu_tanh(convs[0]) * convs[1]).astype(BF16)

    out = x1 + jnp.dot(act_ref[...], wdn_ref[...], preferred_element_type=F32)
    if final:
        ms = jnp.mean(out * out, axis=-1, keepdims=True)
        out = out * lax.rsqrt(ms + NORM_EPS) * fg_ref[...]
    o_ref[...] = out


def _ffn(x2d, att, sgu, w_out, g2, w_up, conv_w, conv_b, w_down, final_g, *, seq, final):
    rows = x2d.shape[0]
    tm = FFN_TM
    grid = (rows // tm,)
    row_spec = lambda width: pl.BlockSpec((tm, width), lambda i: (i, 0))
    whole = pl.BlockSpec(memory_space=pltpu.VMEM)
    return pl.pallas_call(
        functools.partial(_ffn_kernel, tm=tm, tiles_per_batch=seq // tm, final=final),
        grid=grid,
        in_specs=[row_spec(D_MODEL), row_spec(ATT_WIDTH), row_spec(SGU_WIDTH),
                  whole, whole, whole, whole, whole, whole, whole],
        out_specs=row_spec(D_MODEL),
        out_shape=jax.ShapeDtypeStruct((rows, D_MODEL), F32),
        scratch_shapes=[
            pltpu.VMEM((2, D_FF // FFN_FC, SUBLANES, FFN_FC), F32),
            pltpu.VMEM((2, SUBLANES + tm, FFN_FC), F32),
            pltpu.VMEM((tm, D_FF), BF16),
        ],
        compiler_params=pltpu.CompilerParams(
            dimension_semantics=("arbitrary",), vmem_limit_bytes=VMEM_LIMIT_BYTES),
        name="ffn",
    )(x2d, att, sgu, w_out, g2, w_up, conv_w, conv_b, w_down, final_g)


def kernel(x, norm1_g, w_in, lam_q1, lam_k1, lam_q2, lam_k2, subln_g, sgu_ln_g, sgu_ln_b,
           sgu_w, sgu_b, w_out, norm2_g, ffn_w_up, ffn_conv_w, ffn_conv_b, ffn_w_down, final_g):
    batch, seq, _ = x.shape
    rows = batch * seq
    x2d = x.reshape(rows, D_MODEL)
    slopes = jnp.asarray([_alibi_slope(h) * LOG2_E for h in range(N_HEADS)], dtype=F32)
    row1 = lambda a: a.reshape(1, -1).astype(F32)
    pair_q, pair_k, pair_n = _off_diagonal_pairs(batch, seq)
    for l in range(DEPTH):
        lam_init = _lambda_init(l)
        sgu_b_full = jnp.broadcast_to(sgu_b[l].astype(F32)[:, :, None],
                                      (N_GROUPS, SGU_CHUNK, GROUP_DIM))
        qat, qbt, ka, kb, vt, sgu = _mix_in(x2d, row1(norm1_g[l]), w_in[l].astype(BF16),
                                            row1(sgu_ln_g[l]), row1(sgu_ln_b[l]),
                                            sgu_w[l].astype(F32), sgu_b_full)
        att = _attention(pair_q, pair_k, pair_n, slopes, row1(lam_q1[l]), row1(lam_k1[l]), row1(lam_q2[l]), row1(lam_k2[l]),
                         row1(subln_g[l]), qat, qbt, ka, kb, vt,
                         batch=batch, seq=seq, lam_init=lam_init)
        x2d = _ffn(x2d, att, sgu, w_out[l].astype(BF16), row1(norm2_g[l]),
                   ffn_w_up[l].astype(BF16), ffn_conv_w[l].astype(F32), row1(ffn_conv_b[l]),
                   ffn_w_down[l].astype(BF16), row1(final_g), seq=seq, final=(l == DEPTH - 1))
    return x2d.reshape(batch, seq, D_MODEL)
```

```python
import functools
import math

import jax
import jax.numpy as jnp
import numpy as np
from jax import lax
from jax.experimental import pallas as pl
from jax.experimental.pallas import tpu as pltpu

F32 = jnp.float32
BF16 = jnp.bfloat16

D_MODEL = 1024
DEPTH = 2
N_HEADS = 4
HEAD_DIM = 64
HEAD_SLAB = 2 * HEAD_DIM
ATT_WIDTH = N_HEADS * HEAD_SLAB
N_GROUPS = 4
SGU_CHUNK = 128
GROUP_DIM = 128
SGU_WIDTH = N_GROUPS * GROUP_DIM
D_FF = 2816
CONV_WIDTH = 3
NORM_EPS = 1e-6
SUBLN_EPS = 1e-5
LN_EPS = 1e-5
NEG_INF = -1e30
LOG2_E = math.log2(math.e)

SUBLANES = 8
BF16_EXACT_INT = 256
UNDERFLOW_LOG2 = 150.0
NORM_SAFETY = 1.001
VMEM_LIMIT_BYTES = 56 * 1024 * 1024

MIX_TM = 512
ATT_T = 512
FFN_TM = 512
FFN_FC = 256


def _gelu_tanh(x):
    c = math.sqrt(2.0 / math.pi)
    return 0.5 * x * (1.0 + jnp.tanh(c * (x + 0.044715 * (x * x * x))))


def _alibi_slope(head):
    return 2.0 ** (-8.0 * (head + 1) / N_HEADS)


def _bf16_terms(value, n=3):
    terms = []
    rest = value
    for _ in range(n):
        term = float(np.asarray(rest, dtype=BF16).astype(np.float32))
        terms.append(term)
        rest -= term
    return terms


def _lambda_init(layer_idx):
    return 0.8 - 0.6 * math.exp(-0.3 * layer_idx)


def _mixin_kernel(x_ref, g_ref, w_ref, lng_ref, lnb_ref, sw_ref, sb_ref,
                  qat_ref, qbt_ref, ka_ref, kb_ref, vt_ref, norm_ref, sgu_ref, *, tm, t):
    x = x_ref[...]
    ms = jnp.mean(x * x, axis=-1, keepdims=True)
    hb = (x * lax.rsqrt(ms + NORM_EPS) * g_ref[...]).astype(BF16)

    def proj(c0, width):
        return jnp.dot(hb, w_ref[:, c0:c0 + width], preferred_element_type=F32)

    zq = proj(0, ATT_WIDTH) * (HEAD_DIM ** -0.5 * LOG2_E)
    zk = proj(ATT_WIDTH, ATT_WIDTH)
    zv = proj(2 * ATT_WIDTH, ATT_WIDTH)
    lane = lax.broadcasted_iota(jnp.int32, (tm, HEAD_SLAB), 1)
    pos = lax.broadcasted_iota(jnp.int32, (tm, HEAD_SLAB), 0) % t
    pos_lo = (pos % BF16_EXACT_INT).astype(F32)
    pos_hi = (pos - pos % BF16_EXACT_INT).astype(F32)
    in_a = lane < HEAD_DIM

    def aug(first_lane, values):
        out = jnp.zeros((tm, HEAD_SLAB), F32)
        for n, val in enumerate(values):
            out = jnp.where(lane == first_lane + n, val, out)
        return out

    def max_sq_norm(slab):
        sq = slab.astype(BF16).astype(F32) ** 2
        na = jnp.sum(jnp.where(in_a, sq, 0.0), axis=-1, keepdims=True)
        nb = jnp.sum(jnp.where(in_a, 0.0, sq), axis=-1, keepdims=True)
        return jnp.max(jnp.maximum(na, nb), axis=0, keepdims=True)

    stat_row = lax.broadcasted_iota(jnp.int32, (SUBLANES, HEAD_SLAB), 0)
    stat_lane = lax.broadcasted_iota(jnp.int32, (SUBLANES, HEAD_SLAB), 1)
    norms = jnp.zeros((SUBLANES, HEAD_SLAB), F32)
    for h in range(N_HEADS):
        lo, hi = h * HEAD_SLAB, (h + 1) * HEAD_SLAB
        factor = _bf16_terms(_alibi_slope(h) * LOG2_E)
        positions = [pos_lo] * len(factor) + [pos_hi] * len(factor)
        qs, ks = zq[:, lo:hi], zk[:, lo:hi]
        norms = jnp.where((stat_row == 0) & (stat_lane == h), max_sq_norm(qs), norms)
        norms = jnp.where((stat_row == 1) & (stat_lane == h), max_sq_norm(ks), norms)
        ka_ref[:, lo:hi] = jnp.where(in_a, ks, aug(HEAD_DIM, positions)).astype(BF16)
        kb_ref[:, lo:hi] = jnp.where(in_a, aug(0, positions), ks).astype(BF16)
        qat = jnp.where(in_a, qs, aug(HEAD_DIM, factor + factor)).T.astype(BF16)
        qbt = jnp.where(in_a, aug(0, factor + factor), qs).T.astype(BF16)
        vt = zv[:, lo:hi].T.astype(BF16)
        for c in range(tm // t):
            qat_ref[c, lo:hi, :] = qat[:, c * t:(c + 1) * t]
            qbt_ref[c, lo:hi, :] = qbt[:, c * t:(c + 1) * t]
            vt_ref[c, lo:hi, :] = vt[:, c * t:(c + 1) * t]
    norm_ref[0] = norms

    u = _gelu_tanh(proj(3 * ATT_WIDTH, SGU_WIDTH))
    vg = _gelu_tanh(proj(3 * ATT_WIDTH + SGU_WIDTH, SGU_WIDTH))
    row = lax.broadcasted_iota(jnp.int32, (SGU_CHUNK, SGU_CHUNK), 0)
    col = lax.broadcasted_iota(jnp.int32, (SGU_CHUNK, SGU_CHUNK), 1)
    tril = col <= row
    for g in range(N_GROUPS):
        lo, hi = g * GROUP_DIM, (g + 1) * GROUP_DIM
        y = vg[:, lo:hi]
        mu = jnp.mean(y, axis=-1, keepdims=True)
        yc = y - mu
        var = jnp.mean(yc * yc, axis=-1, keepdims=True)
        yn = (yc * lax.rsqrt(var + LN_EPS) * lng_ref[:, lo:hi] + lnb_ref[:, lo:hi]).astype(BF16)
        wm = jnp.where(tril, sw_ref[g], 0.0).astype(BF16)
        bias = sb_ref[g]
        for c in range(tm // SGU_CHUNK):
            r0, r1 = c * SGU_CHUNK, (c + 1) * SGU_CHUNK
            vmix = jnp.dot(wm, yn[r0:r1, :], preferred_element_type=F32) + bias
            sgu_ref[r0:r1, lo:hi] = (u[r0:r1, lo:hi] * vmix).astype(BF16)


def _mix_in(x2d, g, w_in, ln_g, ln_b, sgu_w, sgu_b_full):
    rows = x2d.shape[0]
    tm, t = MIX_TM, ATT_T
    grid = (rows // tm,)
    row_spec = lambda width: pl.BlockSpec((tm, width), lambda i: (i, 0))
    tr_spec = pl.BlockSpec((tm // t, ATT_WIDTH, t), lambda i: (i, 0, 0))
    whole = pl.BlockSpec(memory_space=pltpu.VMEM)
    row_sds = jax.ShapeDtypeStruct((rows, ATT_WIDTH), BF16)
    tr_sds = jax.ShapeDtypeStruct((rows // t, ATT_WIDTH, t), BF16)
    norm_spec = pl.BlockSpec((1, SUBLANES, HEAD_SLAB), lambda i: (i, 0, 0))
    norm_sds = jax.ShapeDtypeStruct((rows // tm, SUBLANES, HEAD_SLAB), F32)
    return pl.pallas_call(
        functools.partial(_mixin_kernel, tm=tm, t=t),
        grid=grid,
        in_specs=[row_spec(D_MODEL), whole, whole, whole, whole, whole, whole],
        out_specs=[tr_spec, tr_spec, row_spec(ATT_WIDTH), row_spec(ATT_WIDTH), tr_spec, norm_spec,
                   row_spec(SGU_WIDTH)],
        out_shape=[tr_sds, tr_sds, row_sds, row_sds, tr_sds, norm_sds, row_sds],
        compiler_params=pltpu.CompilerParams(
            dimension_semantics=("arbitrary",), vmem_limit_bytes=VMEM_LIMIT_BYTES),
        name="mix_in",
    )(x2d, g, w_in, ln_g, ln_b, sgu_w, sgu_b_full)


N_MAPS = 2


def _score_block(qi, kj, masked, q_refs, k_refs, s_ref, bm_ref, *, t):
    k0 = pl.multiple_of(kj * t, t)
    if masked:
        key = lax.broadcasted_iota(jnp.int32, (t, t), 0)
        qry = lax.broadcasted_iota(jnp.int32, (t, t), 1)
        causal = key <= qry
    for mp in range(N_MAPS):
        st = jnp.dot(k_refs[mp][pl.ds(k0, t), :], q_refs[mp][qi], preferred_element_type=F32)
        if masked:
            st = jnp.where(causal, st, NEG_INF)
        s_ref[mp] = st
        bm_ref[mp] = jnp.max(st, axis=0, keepdims=True)


def _softmax_block(qi, c, vt_blk, s_ref, bm_ref, m_ref, l_ref, acc_ref, first):
    for mp in range(N_MAPS):
        bm = bm_ref[mp] + c
        m_new = bm if first else jnp.maximum(m_ref[qi, mp], bm)
        p = jnp.exp2(s_ref[mp] - (m_new - c))
        ps = jnp.sum(p, axis=0, keepdims=True)
        pv = jnp.dot(vt_blk, p.astype(BF16), preferred_element_type=F32)
        if first:
            l_ref[qi, mp] = ps
            acc_ref[qi, mp] = pv
        else:
            alpha = jnp.exp2(m_ref[qi, mp] - m_new)
            l_ref[qi, mp] = alpha * l_ref[qi, mp] + ps
            acc_ref[qi, mp] = alpha * acc_ref[qi, mp] + pv
        m_ref[qi, mp] = m_new


def _attn_kernel(pair_q_ref, pair_k_ref, pair_n_ref,
                 slopes_ref, lq1_ref, lk1_ref, lq2_ref, lk2_ref, sg_ref,
                 qat_ref, qbt_ref, ka_ref, kb_ref, vt_ref, o_ref,
                 m_ref, l_ref, acc_ref, sa_ref, bma_ref, sb_ref, bmb_ref, *, t, nq, lam_init):
    bh = pl.program_id(0) * N_HEADS + pl.program_id(1)
    slope = slopes_ref[pl.program_id(1)]
    n_pairs = pair_n_ref[bh]
    score = functools.partial(_score_block, q_refs=(qat_ref, qbt_ref), k_refs=(ka_ref, kb_ref), t=t)
    softmax = functools.partial(_softmax_block, m_ref=m_ref, l_ref=l_ref, acc_ref=acc_ref)
    buf_a = dict(s_ref=sa_ref, bm_ref=bma_ref)
    buf_b = dict(s_ref=sb_ref, bm_ref=bmb_ref)

    score(0, 0, True, **buf_a)

    def diag_pair(n, carry):
        qi = 2 * n
        score(qi + 1, qi + 1, True, **buf_b)
        softmax(qi, 0.0, vt_ref[qi], first=True, **buf_a)
        nxt = jnp.minimum(qi + 2, nq - 1)
        score(nxt, nxt, True, **buf_a)
        softmax(qi + 1, 0.0, vt_ref[qi + 1], first=True, **buf_b)
        return carry

    lax.fori_loop(0, nq // 2, diag_pair, 0)

    def entry(e):
        qi, kj = pair_q_ref[bh, e], pair_k_ref[bh, e]
        c = jnp.where(e < n_pairs, ((kj - qi) * t).astype(F32) * slope, NEG_INF)
        return qi, kj, c

    def score_entry(e, buf):
        qi, kj, _ = entry(e)
        score(qi, kj, False, **buf)

    def softmax_entry(e, buf):
        qi, kj, c = entry(e)
        softmax(qi, c, vt_ref[kj], first=False, **buf)

    score_entry(0, buf_a)

    def off_pair(n, carry):
        e = 2 * n
        score_entry(e + 1, buf_b)
        softmax_entry(e, buf_a)
        score_entry(e + 2, buf_a)
        softmax_entry(e + 1, buf_b)
        return carry

    lax.fori_loop(0, (n_pairs + 1) // 2, off_pair, 0)

    lam = (jnp.exp(jnp.sum(lq1_ref[...] * lk1_ref[...], axis=-1, keepdims=True))
           - jnp.exp(jnp.sum(lq2_ref[...] * lk2_ref[...], axis=-1, keepdims=True))
           + lam_init)

    def finish(qi, carry):
        o = (acc_ref[qi, 0] * (1.0 / l_ref[qi, 0])
             - lam * (acc_ref[qi, 1] * (1.0 / l_ref[qi, 1])))
        ms = jnp.mean(o * o, axis=0, keepdims=True)
        o = o * lax.rsqrt(ms + SUBLN_EPS) * (1.0 - lam_init)
        o_ref[pl.ds(pl.multiple_of(qi * t, t), t), :] = (o.T * sg_ref[...]).astype(BF16)
        return carry

    lax.fori_loop(0, nq, finish, 0)


def _attention(pair_q, pair_k, pair_n, slopes, lq1, lk1, lq2, lk2, subln_g, qat, qbt, ka, kb, vt,
               *, batch, seq, lam_init):
    t = ATT_T
    nq = seq // t
    qt_spec = pl.BlockSpec((nq, HEAD_SLAB, t), lambda b, h, *_: (b, h, 0))
    row_spec = pl.BlockSpec((seq, HEAD_SLAB), lambda b, h, *_: (b, h))
    whole = pl.BlockSpec(memory_space=pltpu.VMEM)
    smem = pl.BlockSpec(memory_space=pltpu.SMEM)
    stat = pltpu.VMEM((nq, N_MAPS, 1, t), F32)
    acc = pltpu.VMEM((nq, N_MAPS, HEAD_SLAB, t), F32)
    score_buf = pltpu.VMEM((N_MAPS, t, t), F32)
    colmax_buf = pltpu.VMEM((N_MAPS, 1, t), F32)
    return pl.pallas_call(
        functools.partial(_attn_kernel, t=t, nq=nq, lam_init=lam_init),
        grid_spec=pltpu.PrefetchScalarGridSpec(
            num_scalar_prefetch=3,
            grid=(batch, N_HEADS),
            in_specs=[smem, whole, whole, whole, whole, whole,
                      qt_spec, qt_spec, row_spec, row_spec, qt_spec],
            out_specs=row_spec,
            scratch_shapes=[stat, stat, acc, score_buf, colmax_buf, score_buf, colmax_buf]),
        out_shape=jax.ShapeDtypeStruct((batch * seq, ATT_WIDTH), BF16),
        compiler_params=pltpu.CompilerParams(
            dimension_semantics=("arbitrary", "arbitrary"),
            vmem_limit_bytes=VMEM_LIMIT_BYTES),
        name="diff_attn",
    )(pair_q, pair_k, pair_n, slopes, lq1, lk1, lq2, lk2, subln_g, qat, qbt, ka, kb, vt)


def _off_diagonal_pairs(norms, slopes, batch, seq):
    nq = seq // ATT_T
    assert MIX_TM == ATT_T
    cand = [(i, j) for i in range(nq) for j in range(i)]
    ci = np.asarray([p[0] for p in cand], np.int32)
    cj = np.asarray([p[1] for p in cand], np.int32)
    min_dist = ((ci - cj - 1) * ATT_T + 1).astype(np.float32)
    norms = norms.reshape(batch, nq, SUBLANES, HEAD_SLAB)
    qn = jnp.sqrt(norms[:, :, 0, :N_HEADS]) * NORM_SAFETY
    kn = jnp.sqrt(norms[:, :, 1, :N_HEADS]) * NORM_SAFETY
    bound = qn[:, ci] * (kn[:, cj] + kn[:, ci]) - slopes[None, None, :] * min_dist[None, :, None]
    keep = ~(bound < -UNDERFLOW_LOG2)
    keep = keep.transpose(0, 2, 1).reshape(batch * N_HEADS, len(cand))
    order = jnp.argsort(~keep, axis=-1, stable=True)
    pad = jnp.zeros((batch * N_HEADS, 2), jnp.int32)
    pair_q = jnp.concatenate([jnp.asarray(ci)[order], pad], axis=-1)
    pair_k = jnp.concatenate([jnp.asarray(cj)[order], pad], axis=-1)
    return pair_q, pair_k, jnp.sum(keep, axis=-1, dtype=jnp.int32)


def _ffn_kernel(x_ref, att_ref, sgu_ref, wo_ref, g2_ref, wup_ref, cw_ref, cb_ref, wdn_ref, fg_ref,
                o_ref, carry_ref, buf_ref, act_ref, *, tm, tiles_per_batch, final):
    fc = FFN_FC
    halo = SUBLANES

    @pl.when(pl.program_id(0) % tiles_per_batch == 0)
    def _():
        carry_ref[...] = jnp.zeros_like(carry_ref)

    x1 = (x_ref[...]
          + jnp.dot(att_ref[...], wo_ref[0:ATT_WIDTH, :], preferred_element_type=F32)
          + jnp.dot(sgu_ref[...], wo_ref[ATT_WIDTH:ATT_WIDTH + SGU_WIDTH, :],
                    preferred_element_type=F32))
    ms = jnp.mean(x1 * x1, axis=-1, keepdims=True)
    h2 = (x1 * lax.rsqrt(ms + NORM_EPS) * g2_ref[...]).astype(BF16)

    for c in range(D_FF // fc):
        convs = []
        for half in range(2):
            col0 = half * D_FF + c * fc
            up = jnp.dot(h2, wup_ref[:, col0:col0 + fc], preferred_element_type=F32)
            buf_ref[half, 0:halo, :] = carry_ref[half, c]
            buf_ref[half, halo:halo + tm, :] = up
            carry_ref[half, c] = up[tm - halo:tm, :]
            w0 = cw_ref[0:1, col0:col0 + fc]
            w1 = cw_ref[1:2, col0:col0 + fc]
            w2 = cw_ref[2:3, col0:col0 + fc]
            convs.append(cb_ref[:, col0:col0 + fc]
                         + w0 * buf_ref[half, halo - 2:halo - 2 + tm, :]
                         + w1 * buf_ref[half, halo - 1:halo - 1 + tm, :]
                         + w2 * up)
        act_ref[:, c * fc:(c + 1) * fc] = (_gelu_tanh(convs[0]) * convs[1]).astype(BF16)

    out = x1 + jnp.dot(act_ref[...], wdn_ref[...], preferred_element_type=F32)
    if final:
        ms = jnp.mean(out * out, axis=-1, keepdims=True)
        out = out * lax.rsqrt(ms + NORM_EPS) * fg_ref[...]
    o_ref[...] = out


def _ffn(x2d, att, sgu, w_out, g2, w_up, conv_w, conv_b, w_down, final_g, *, seq, final):
    rows = x2d.shape[0]
    tm = FFN_TM
    grid = (rows // tm,)
    row_spec = lambda width: pl.BlockSpec((tm, width), lambda i: (i, 0))
    whole = pl.BlockSpec(memory_space=pltpu.VMEM)
    return pl.pallas_call(
        functools.partial(_ffn_kernel, tm=tm, tiles_per_batch=seq // tm, final=final),
        grid=grid,
        in_specs=[row_spec(D_MODEL), row_spec(ATT_WIDTH), row_spec(SGU_WIDTH),
                  whole, whole, whole, whole, whole, whole, whole],
        out_specs=row_spec(D_MODEL),
        out_shape=jax.ShapeDtypeStruct((rows, D_MODEL), F32),
        scratch_shapes=[
            pltpu.VMEM((2, D_FF // FFN_FC, SUBLANES, FFN_FC), F32),
            pltpu.VMEM((2, SUBLANES + tm, FFN_FC), F32),
            pltpu.VMEM((tm, D_FF), BF16),
        ],
        compiler_params=pltpu.CompilerParams(
            dimension_semantics=("arbitrary",), vmem_limit_bytes=VMEM_LIMIT_BYTES),
        name="ffn",
    )(x2d, att, sgu, w_out, g2, w_up, conv_w, conv_b, w_down, final_g)


def kernel(x, norm1_g, w_in, lam_q1, lam_k1, lam_q2, lam_k2, subln_g, sgu_ln_g, sgu_ln_b,
           sgu_w, sgu_b, w_out, norm2_g, ffn_w_up, ffn_conv_w, ffn_conv_b, ffn_w_down, final_g):
    batch, seq, _ = x.shape
    rows = batch * seq
    x2d = x.reshape(rows, D_MODEL)
    slopes = jnp.asarray([_alibi_slope(h) * LOG2_E for h in range(N_HEADS)], dtype=F32)
    row1 = lambda a: a.reshape(1, -1).astype(F32)
    for l in range(DEPTH):
        lam_init = _lambda_init(l)
        sgu_b_full = jnp.broadcast_to(sgu_b[l].astype(F32)[:, :, None],
                                      (N_GROUPS, SGU_CHUNK, GROUP_DIM))
        qat, qbt, ka, kb, vt, norms, sgu = _mix_in(x2d, row1(norm1_g[l]), w_in[l].astype(BF16),
                                                   row1(sgu_ln_g[l]), row1(sgu_ln_b[l]),
                                                   sgu_w[l].astype(F32), sgu_b_full)
        pair_q, pair_k, pair_n = _off_diagonal_pairs(norms, slopes, batch, seq)
        att = _attention(pair_q, pair_k, pair_n, slopes,
                         row1(lam_q1[l]), row1(lam_k1[l]), row1(lam_q2[l]), row1(lam_k2[l]),
                         row1(subln_g[l]), qat, qbt, ka, kb, vt,
                         batch=batch, seq=seq, lam_init=lam_init)
        x2d = _ffn(x2d, att, sgu, w_out[l].astype(BF16), row1(norm2_g[l]),
                   ffn_w_up[l].astype(BF16), ffn_conv_w[l].astype(F32), row1(ffn_conv_b[l]),
                   ffn_w_down[l].astype(BF16), row1(final_g), seq=seq, final=(l == DEPTH - 1))
    return x2d.reshape(batch, seq, D_MODEL)
```

```python
import functools
import math

import jax
import jax.numpy as jnp
import numpy as np
from jax import lax
from jax.experimental import pallas as pl
from jax.experimental.pallas import tpu as pltpu

F32 = jnp.float32
BF16 = jnp.bfloat16

D_MODEL = 1024
DEPTH = 2
N_HEADS = 4
HEAD_DIM = 64
HEAD_SLAB = 2 * HEAD_DIM
ATT_WIDTH = N_HEADS * HEAD_SLAB
N_GROUPS = 4
SGU_CHUNK = 128
GROUP_DIM = 128
SGU_WIDTH = N_GROUPS * GROUP_DIM
D_FF = 2816
CONV_WIDTH = 3
NORM_EPS = 1e-6
SUBLN_EPS = 1e-5
LN_EPS = 1e-5
NEG_INF = -1e30
LOG2_E = math.log2(math.e)

SUBLANES = 8
BF16_EXACT_INT = 256
UNDERFLOW_LOG2 = 150.0
NORM_SAFETY = 1.001
VMEM_LIMIT_BYTES = 56 * 1024 * 1024

MIX_TM = 512
ATT_T = 512
FFN_TM = 512
FFN_FC = 256


def _gelu_tanh(x):
    c = math.sqrt(2.0 / math.pi)
    return 0.5 * x * (1.0 + jnp.tanh(c * (x + 0.044715 * (x * x * x))))


def _alibi_slope(head):
    return 2.0 ** (-8.0 * (head + 1) / N_HEADS)


def _bf16_terms(value, n=3):
    terms = []
    rest = value
    for _ in range(n):
        term = float(np.asarray(rest, dtype=BF16).astype(np.float32))
        terms.append(term)
        rest -= term
    return terms


def _lambda_init(layer_idx):
    return 0.8 - 0.6 * math.exp(-0.3 * layer_idx)


def _layer_spec(layer, shape):
    zeros = (0,) * len(shape)
    return pl.BlockSpec((None,) + tuple(shape), lambda *_: (layer,) + zeros,
                        pipeline_mode=pl.Buffered(1))


def _mixin_kernel(x_ref, g_ref, w_ref, lng_ref, lnb_ref, sw_ref, sb_ref,
                  qat_ref, qbt_ref, ka_ref, kb_ref, vt_ref, norm_ref, sgu_ref, *, tm, t):
    x = x_ref[...]
    ms = jnp.mean(x * x, axis=-1, keepdims=True)
    hb = (x * lax.rsqrt(ms + NORM_EPS) * g_ref[...]).astype(BF16)

    def proj(c0, width):
        return jnp.dot(hb, w_ref[:, c0:c0 + width], preferred_element_type=F32)

    zq = proj(0, ATT_WIDTH) * (HEAD_DIM ** -0.5 * LOG2_E)
    zk = proj(ATT_WIDTH, ATT_WIDTH)
    zv = proj(2 * ATT_WIDTH, ATT_WIDTH)
    lane = lax.broadcasted_iota(jnp.int32, (tm, HEAD_SLAB), 1)
    pos = lax.broadcasted_iota(jnp.int32, (tm, HEAD_SLAB), 0) % t
    pos_lo = (pos % BF16_EXACT_INT).astype(F32)
    pos_hi = (pos - pos % BF16_EXACT_INT).astype(F32)
    in_a = lane < HEAD_DIM

    def aug(first_lane, values):
        out = jnp.zeros((tm, HEAD_SLAB), F32)
        for n, val in enumerate(values):
            out = jnp.where(lane == first_lane + n, val, out)
        return out

    def max_sq_norm(slab):
        sq = slab.astype(BF16).astype(F32) ** 2
        na = jnp.sum(jnp.where(in_a, sq, 0.0), axis=-1, keepdims=True)
        nb = jnp.sum(jnp.where(in_a, 0.0, sq), axis=-1, keepdims=True)
        return jnp.max(jnp.maximum(na, nb), axis=0, keepdims=True)

    stat_row = lax.broadcasted_iota(jnp.int32, (SUBLANES, HEAD_SLAB), 0)
    stat_lane = lax.broadcasted_iota(jnp.int32, (SUBLANES, HEAD_SLAB), 1)
    norms = jnp.zeros((SUBLANES, HEAD_SLAB), F32)
    for h in range(N_HEADS):
        lo, hi = h * HEAD_SLAB, (h + 1) * HEAD_SLAB
        factor = _bf16_terms(_alibi_slope(h) * LOG2_E)
        positions = [pos_lo] * len(factor) + [pos_hi] * len(factor)
        qs, ks = zq[:, lo:hi], zk[:, lo:hi]
        norms = jnp.where((stat_row == 0) & (stat_lane == h), max_sq_norm(qs), norms)
        norms = jnp.where((stat_row == 1) & (stat_lane == h), max_sq_norm(ks), norms)
        ka_ref[:, lo:hi] = jnp.where(in_a, ks, aug(HEAD_DIM, positions)).astype(BF16)
        kb_ref[:, lo:hi] = jnp.where(in_a, aug(0, positions), ks).astype(BF16)
        qat = jnp.where(in_a, qs, aug(HEAD_DIM, factor + factor)).T.astype(BF16)
        qbt = jnp.where(in_a, aug(0, factor + factor), qs).T.astype(BF16)
        vt = zv[:, lo:hi].T.astype(BF16)
        for c in range(tm // t):
            qat_ref[c, lo:hi, :] = qat[:, c * t:(c + 1) * t]
            qbt_ref[c, lo:hi, :] = qbt[:, c * t:(c + 1) * t]
            vt_ref[c, lo:hi, :] = vt[:, c * t:(c + 1) * t]
    norm_ref[0] = norms

    u = _gelu_tanh(proj(3 * ATT_WIDTH, SGU_WIDTH))
    vg = _gelu_tanh(proj(3 * ATT_WIDTH + SGU_WIDTH, SGU_WIDTH))
    row = lax.broadcasted_iota(jnp.int32, (SGU_CHUNK, SGU_CHUNK), 0)
    col = lax.broadcasted_iota(jnp.int32, (SGU_CHUNK, SGU_CHUNK), 1)
    tril = col <= row
    for g in range(N_GROUPS):
        lo, hi = g * GROUP_DIM, (g + 1) * GROUP_DIM
        y = vg[:, lo:hi]
        mu = jnp.mean(y, axis=-1, keepdims=True)
        yc = y - mu
        var = jnp.mean(yc * yc, axis=-1, keepdims=True)
        yn = (yc * lax.rsqrt(var + LN_EPS) * lng_ref[:, lo:hi] + lnb_ref[:, lo:hi]).astype(BF16)
        wm = jnp.where(tril, sw_ref[g], 0.0).astype(BF16)
        bias = sb_ref[g]
        for c in range(tm // SGU_CHUNK):
            r0, r1 = c * SGU_CHUNK, (c + 1) * SGU_CHUNK
            vmix = jnp.dot(wm, yn[r0:r1, :], preferred_element_type=F32) + bias
            sgu_ref[r0:r1, lo:hi] = (u[r0:r1, lo:hi] * vmix).astype(BF16)


def _mix_in(layer, x2d, g, w_in, ln_g, ln_b, sgu_w, sgu_b_full):
    rows = x2d.shape[0]
    tm, t = MIX_TM, ATT_T
    grid = (rows // tm,)
    row_spec = lambda width: pl.BlockSpec((tm, width), lambda i: (i, 0))
    tr_spec = pl.BlockSpec((tm // t, ATT_WIDTH, t), lambda i: (i, 0, 0))
    whole = pl.BlockSpec(memory_space=pltpu.VMEM)
    row_sds = jax.ShapeDtypeStruct((rows, ATT_WIDTH), BF16)
    tr_sds = jax.ShapeDtypeStruct((rows // t, ATT_WIDTH, t), BF16)
    norm_spec = pl.BlockSpec((1, SUBLANES, HEAD_SLAB), lambda i: (i, 0, 0))
    norm_sds = jax.ShapeDtypeStruct((rows // tm, SUBLANES, HEAD_SLAB), F32)
    return pl.pallas_call(
        functools.partial(_mixin_kernel, tm=tm, t=t),
        grid=grid,
        in_specs=[row_spec(D_MODEL), whole, _layer_spec(layer, w_in.shape[1:]), whole, whole,
                  _layer_spec(layer, sgu_w.shape[1:]), whole],
        out_specs=[tr_spec, tr_spec, row_spec(ATT_WIDTH), row_spec(ATT_WIDTH), tr_spec, norm_spec,
                   row_spec(SGU_WIDTH)],
        out_shape=[tr_sds, tr_sds, row_sds, row_sds, tr_sds, norm_sds, row_sds],
        compiler_params=pltpu.CompilerParams(
            dimension_semantics=("arbitrary",), vmem_limit_bytes=VMEM_LIMIT_BYTES),
        name="mix_in",
    )(x2d, g, w_in, ln_g, ln_b, sgu_w, sgu_b_full)


N_MAPS = 2


def _score_block(qi, kj, masked, q_refs, k_refs, s_ref, bm_ref, *, t):
    k0 = pl.multiple_of(kj * t, t)
    if masked:
        key = lax.broadcasted_iota(jnp.int32, (t, t), 0)
        qry = lax.broadcasted_iota(jnp.int32, (t, t), 1)
        causal = key <= qry
    for mp in range(N_MAPS):
        st = jnp.dot(k_refs[mp][pl.ds(k0, t), :], q_refs[mp][qi], preferred_element_type=F32)
        if masked:
            st = jnp.where(causal, st, NEG_INF)
        s_ref[mp] = st
        bm_ref[mp] = jnp.max(st, axis=0, keepdims=True)


def _softmax_block(qi, c, vt_blk, s_ref, bm_ref, m_ref, l_ref, acc_ref, first):
    for mp in range(N_MAPS):
        bm = bm_ref[mp] + c
        m_new = bm if first else jnp.maximum(m_ref[qi, mp], bm)
        p = jnp.exp2(s_ref[mp] - (m_new - c))
        ps = jnp.sum(p, axis=0, keepdims=True)
        pv = jnp.dot(vt_blk, p.astype(BF16), preferred_element_type=F32)
        if first:
            l_ref[qi, mp] = ps
            acc_ref[qi, mp] = pv
        else:
            alpha = jnp.exp2(m_ref[qi, mp] - m_new)
            l_ref[qi, mp] = alpha * l_ref[qi, mp] + ps
            acc_ref[qi, mp] = alpha * acc_ref[qi, mp] + pv
        m_ref[qi, mp] = m_new


def _attn_kernel(pair_q_ref, pair_k_ref, pair_n_ref,
                 slopes_ref, lq1_ref, lk1_ref, lq2_ref, lk2_ref, sg_ref,
                 qat_ref, qbt_ref, ka_ref, kb_ref, vt_ref, o_ref,
                 m_ref, l_ref, acc_ref, sa_ref, bma_ref, sb_ref, bmb_ref, *, t, nq, lam_init):
    bh = pl.program_id(0) * N_HEADS + pl.program_id(1)
    slope = slopes_ref[pl.program_id(1)]
    n_pairs = pair_n_ref[bh]
    score = functools.partial(_score_block, q_refs=(qat_ref, qbt_ref), k_refs=(ka_ref, kb_ref), t=t)
    softmax = functools.partial(_softmax_block, m_ref=m_ref, l_ref=l_ref, acc_ref=acc_ref)
    buf_a = dict(s_ref=sa_ref, bm_ref=bma_ref)
    buf_b = dict(s_ref=sb_ref, bm_ref=bmb_ref)

    score(0, 0, True, **buf_a)

    def diag_pair(n, carry):
        qi = 2 * n
        score(qi + 1, qi + 1, True, **buf_b)
        softmax(qi, 0.0, vt_ref[qi], first=True, **buf_a)
        nxt = jnp.minimum(qi + 2, nq - 1)
        score(nxt, nxt, True, **buf_a)
        softmax(qi + 1, 0.0, vt_ref[qi + 1], first=True, **buf_b)
        return carry

    lax.fori_loop(0, nq // 2, diag_pair, 0)

    def entry(e):
        qi, kj = pair_q_ref[bh, e], pair_k_ref[bh, e]
        c = jnp.where(e < n_pairs, ((kj - qi) * t).astype(F32) * slope, NEG_INF)
        return qi, kj, c

    def score_entry(e, buf):
        qi, kj, _ = entry(e)
        score(qi, kj, False, **buf)

    def softmax_entry(e, buf):
        qi, kj, c = entry(e)
        softmax(qi, c, vt_ref[kj], first=False, **buf)

    score_entry(0, buf_a)

    def off_pair(n, carry):
        e = 2 * n
        score_entry(e + 1, buf_b)
        softmax_entry(e, buf_a)
        score_entry(e + 2, buf_a)
        softmax_entry(e + 1, buf_b)
        return carry

    lax.fori_loop(0, (n_pairs + 1) // 2, off_pair, 0)

    lam = (jnp.exp(jnp.sum(lq1_ref[...] * lk1_ref[...], axis=-1, keepdims=True))
           - jnp.exp(jnp.sum(lq2_ref[...] * lk2_ref[...], axis=-1, keepdims=True))
           + lam_init)

    def finish(qi, carry):
        o = (acc_ref[qi, 0] * (1.0 / l_ref[qi, 0])
             - lam * (acc_ref[qi, 1] * (1.0 / l_ref[qi, 1])))
        ms = jnp.mean(o * o, axis=0, keepdims=True)
        o = o * lax.rsqrt(ms + SUBLN_EPS) * (1.0 - lam_init)
        o_ref[pl.ds(pl.multiple_of(qi * t, t), t), :] = (o.T * sg_ref[...]).astype(BF16)
        return carry

    lax.fori_loop(0, nq, finish, 0)


def _attention(pair_q, pair_k, pair_n, slopes, lq1, lk1, lq2, lk2, subln_g, qat, qbt, ka, kb, vt,
               *, batch, seq, lam_init):
    t = ATT_T
    nq = seq // t
    qt_spec = pl.BlockSpec((nq, HEAD_SLAB, t), lambda b, h, *_: (b, h, 0))
    row_spec = pl.BlockSpec((seq, HEAD_SLAB), lambda b, h, *_: (b, h))
    whole = pl.BlockSpec(memory_space=pltpu.VMEM)
    smem = pl.BlockSpec(memory_space=pltpu.SMEM)
    stat = pltpu.VMEM((nq, N_MAPS, 1, t), F32)
    acc = pltpu.VMEM((nq, N_MAPS, HEAD_SLAB, t), F32)
    score_buf = pltpu.VMEM((N_MAPS, t, t), F32)
    colmax_buf = pltpu.VMEM((N_MAPS, 1, t), F32)
    return pl.pallas_call(
        functools.partial(_attn_kernel, t=t, nq=nq, lam_init=lam_init),
        grid_spec=pltpu.PrefetchScalarGridSpec(
            num_scalar_prefetch=3,
            grid=(batch, N_HEADS),
            in_specs=[smem, whole, whole, whole, whole, whole,
                      qt_spec, qt_spec, row_spec, row_spec, qt_spec],
            out_specs=row_spec,
            scratch_shapes=[stat, stat, acc, score_buf, colmax_buf, score_buf, colmax_buf]),
        out_shape=jax.ShapeDtypeStruct((batch * seq, ATT_WIDTH), BF16),
        compiler_params=pltpu.CompilerParams(
            dimension_semantics=("arbitrary", "arbitrary"),
            vmem_limit_bytes=VMEM_LIMIT_BYTES),
        name="diff_attn",
    )(pair_q, pair_k, pair_n, slopes, lq1, lk1, lq2, lk2, subln_g, qat, qbt, ka, kb, vt)


def _off_diagonal_pairs(norms, slopes, batch, seq):
    nq = seq // ATT_T
    assert MIX_TM == ATT_T
    tile = np.arange(nq)
    below = tile[:, None] > tile[None, :]
    min_dist = ((tile[:, None] - tile[None, :] - 1) * ATT_T + 1).astype(np.float32)
    norms = norms.reshape(batch, nq, SUBLANES, HEAD_SLAB)
    qn = (jnp.sqrt(norms[:, :, 0, :N_HEADS]) * NORM_SAFETY).transpose(0, 2, 1)
    kn = (jnp.sqrt(norms[:, :, 1, :N_HEADS]) * NORM_SAFETY).transpose(0, 2, 1)
    bound = (qn[..., :, None] * (kn[..., None, :] + kn[..., :, None])
             - slopes[None, :, None, None] * min_dist)
    keep = below & ~(bound < -UNDERFLOW_LOG2)
    keep = keep.reshape(batch * N_HEADS, nq * nq)
    order = jnp.argsort(~keep, axis=-1, stable=True)[:, :nq * (nq - 1) // 2 + 2].astype(jnp.int32)
    return order // nq, order % nq, jnp.sum(keep, axis=-1, dtype=jnp.int32)


def _ffn_kernel(x_ref, att_ref, sgu_ref, wo_ref, g2_ref, wup_ref, cw_ref, cb_ref, wdn_ref, fg_ref,
                o_ref, carry_ref, buf_ref, act_ref, *, tm, tiles_per_batch, final):
    fc = FFN_FC
    halo = SUBLANES

    @pl.when(pl.program_id(0) % tiles_per_batch == 0)
    def _():
        carry_ref[...] = jnp.zeros_like(carry_ref)

    x1 = (x_ref[...]
          + jnp.dot(att_ref[...], wo_ref[0:ATT_WIDTH, :], preferred_element_type=F32)
          + jnp.dot(sgu_ref[...], wo_ref[ATT_WIDTH:ATT_WIDTH + SGU_WIDTH, :],
                    preferred_element_type=F32))
    ms = jnp.mean(x1 * x1, axis=-1, keepdims=True)
    h2 = (x1 * lax.rsqrt(ms + NORM_EPS) * g2_ref[...]).astype(BF16)

    for c in range(D_FF // fc):
        convs = []
        for half in range(2):
            col0 = half * D_FF + c * fc
            up = jnp.dot(h2, wup_ref[:, col0:col0 + fc], preferred_element_type=F32)
            buf_ref[half, 0:halo, :] = carry_ref[half, c]
            buf_ref[half, halo:halo + tm, :] = up
            carry_ref[half, c] = up[tm - halo:tm, :]
            w0 = cw_ref[0:1, col0:col0 + fc]
            w1 = cw_ref[1:2, col0:col0 + fc]
            w2 = cw_ref[2:3, col0:col0 + fc]
            convs.append(cb_ref[:, col0:col0 + fc]
                         + w0 * buf_ref[half, halo - 2:halo - 2 + tm, :]
                         + w1 * buf_ref[half, halo - 1:halo - 1 + tm, :]
                         + w2 * up)
        act_ref[:, c * fc:(c + 1) * fc] = (_gelu_tanh(convs[0]) * convs[1]).astype(BF16)

    out = x1 + jnp.dot(act_ref[...], wdn_ref[...], preferred_element_type=F32)
    if final:
        ms = jnp.mean(out * out, axis=-1, keepdims=True)
        out = out * lax.rsqrt(ms + NORM_EPS) * fg_ref[...]
    o_ref[...] = out


def _ffn(layer, x2d, att, sgu, w_out, g2, w_up, conv_w, conv_b, w_down, final_g, *, seq, final):
    rows = x2d.shape[0]
    tm = FFN_TM
    grid = (rows // tm,)
    row_spec = lambda width: pl.BlockSpec((tm, width), lambda i: (i, 0))
    whole = pl.BlockSpec(memory_space=pltpu.VMEM)
    return pl.pallas_call(
        functools.partial(_ffn_kernel, tm=tm, tiles_per_batch=seq // tm, final=final),
        grid=grid,
        in_specs=[row_spec(D_MODEL), row_spec(ATT_WIDTH), row_spec(SGU_WIDTH),
                  _layer_spec(layer, w_out.shape[1:]), whole, _layer_spec(layer, w_up.shape[1:]),
                  _layer_spec(layer, conv_w.shape[1:]), whole, _layer_spec(layer, w_down.shape[1:]),
                  whole],
        out_specs=row_spec(D_MODEL),
        out_shape=jax.ShapeDtypeStruct((rows, D_MODEL), F32),
        scratch_shapes=[
            pltpu.VMEM((2, D_FF // FFN_FC, SUBLANES, FFN_FC), F32),
            pltpu.VMEM((2, SUBLANES + tm, FFN_FC), F32),
            pltpu.VMEM((tm, D_FF), BF16),
        ],
        compiler_params=pltpu.CompilerParams(
            dimension_semantics=("arbitrary",), vmem_limit_bytes=VMEM_LIMIT_BYTES),
        name="ffn",
    )(x2d, att, sgu, w_out, g2, w_up, conv_w, conv_b, w_down, final_g)


def kernel(x, norm1_g, w_in, lam_q1, lam_k1, lam_q2, lam_k2, subln_g, sgu_ln_g, sgu_ln_b,
           sgu_w, sgu_b, w_out, norm2_g, ffn_w_up, ffn_conv_w, ffn_conv_b, ffn_w_down, final_g):
    batch, seq, _ = x.shape
    rows = batch * seq
    x2d = x.reshape(rows, D_MODEL)
    slopes = jnp.asarray([_alibi_slope(h) * LOG2_E for h in range(N_HEADS)], dtype=F32)
    row1 = lambda a: a.reshape(1, -1).astype(F32)
    w_in, w_out, w_up, w_down = (w.astype(BF16) for w in (w_in, w_out, ffn_w_up, ffn_w_down))
    sgu_w, conv_w = sgu_w.astype(F32), ffn_conv_w.astype(F32)
    for l in range(DEPTH):
        lam_init = _lambda_init(l)
        sgu_b_full = jnp.broadcast_to(sgu_b[l].astype(F32)[:, :, None],
                                      (N_GROUPS, SGU_CHUNK, GROUP_DIM))
        qat, qbt, ka, kb, vt, norms, sgu = _mix_in(l, x2d, row1(norm1_g[l]), w_in,
                                                   row1(sgu_ln_g[l]), row1(sgu_ln_b[l]),
                                                   sgu_w, sgu_b_full)
        pair_q, pair_k, pair_n = _off_diagonal_pairs(norms, slopes, batch, seq)
        att = _attention(pair_q, pair_k, pair_n, slopes,
                         row1(lam_q1[l]), row1(lam_k1[l]), row1(lam_q2[l]), row1(lam_k2[l]),
                         row1(subln_g[l]), qat, qbt, ka, kb, vt,
                         batch=batch, seq=seq, lam_init=lam_init)
        x2d = _ffn(l, x2d, att, sgu, w_out, row1(norm2_g[l]), w_up, conv_w, row1(ffn_conv_b[l]),
                   w_down, row1(final_g), seq=seq, final=(l == DEPTH - 1))
    return x2d.reshape(batch, seq, D_MODEL)
```

```python
import functools
import math

import jax
import jax.numpy as jnp
import numpy as np
from jax import lax
from jax.experimental import pallas as pl
from jax.experimental.pallas import tpu as pltpu

F32 = jnp.float32
BF16 = jnp.bfloat16

D_MODEL = 1024
DEPTH = 2
N_HEADS = 4
HEAD_DIM = 64
HEAD_SLAB = 2 * HEAD_DIM
ATT_WIDTH = N_HEADS * HEAD_SLAB
N_GROUPS = 4
SGU_CHUNK = 128
GROUP_DIM = 128
SGU_WIDTH = N_GROUPS * GROUP_DIM
D_FF = 2816
CONV_WIDTH = 3
NORM_EPS = 1e-6
SUBLN_EPS = 1e-5
LN_EPS = 1e-5
NEG_INF = -1e30
LOG2_E = math.log2(math.e)

SUBLANES = 8
BF16_EXACT_INT = 256
UNDERFLOW_LOG2 = 150.0
NORM_SAFETY = 1.001
VMEM_LIMIT_BYTES = 56 * 1024 * 1024

MIX_TM = 512
ATT_T = 512
ATT_UNROLL = 4
FFN_TM = 512
FFN_FC = 256


def _gelu_tanh(x):
    c = math.sqrt(2.0 / math.pi)
    return 0.5 * x * (1.0 + jnp.tanh(c * (x + 0.044715 * (x * x * x))))


def _alibi_slope(head):
    return 2.0 ** (-8.0 * (head + 1) / N_HEADS)


def _bf16_terms(value, n=3):
    terms = []
    rest = value
    for _ in range(n):
        term = float(np.asarray(rest, dtype=BF16).astype(np.float32))
        terms.append(term)
        rest -= term
    return terms


def _lambda_init(layer_idx):
    return 0.8 - 0.6 * math.exp(-0.3 * layer_idx)


def _layer_spec(layer, shape):
    zeros = (0,) * len(shape)
    return pl.BlockSpec((None,) + tuple(shape), lambda *_: (layer,) + zeros,
                        pipeline_mode=pl.Buffered(1))


def _mixin_kernel(x_ref, g_ref, w_ref, lng_ref, lnb_ref, sw_ref, sb_ref,
                  qat_ref, qbt_ref, ka_ref, kb_ref, vt_ref, norm_ref, sgu_ref, *, tm, t):
    x = x_ref[...]
    ms = jnp.mean(x * x, axis=-1, keepdims=True)
    hb = (x * lax.rsqrt(ms + NORM_EPS) * g_ref[...]).astype(BF16)

    def proj(c0, width):
        return jnp.dot(hb, w_ref[:, c0:c0 + width], preferred_element_type=F32)

    zq = proj(0, ATT_WIDTH) * (HEAD_DIM ** -0.5 * LOG2_E)
    zk = proj(ATT_WIDTH, ATT_WIDTH)
    zv = proj(2 * ATT_WIDTH, ATT_WIDTH)
    lane = lax.broadcasted_iota(jnp.int32, (tm, HEAD_SLAB), 1)
    pos = lax.broadcasted_iota(jnp.int32, (tm, HEAD_SLAB), 0) % t
    pos_lo = (pos % BF16_EXACT_INT).astype(F32)
    pos_hi = (pos - pos % BF16_EXACT_INT).astype(F32)
    in_a = lane < HEAD_DIM

    def aug(first_lane, values):
        out = jnp.zeros((tm, HEAD_SLAB), F32)
        for n, val in enumerate(values):
            out = jnp.where(lane == first_lane + n, val, out)
        return out

    def max_sq_norm(slab):
        sq = slab.astype(BF16).astype(F32) ** 2
        na = jnp.sum(jnp.where(in_a, sq, 0.0), axis=-1, keepdims=True)
        nb = jnp.sum(jnp.where(in_a, 0.0, sq), axis=-1, keepdims=True)
        return jnp.max(jnp.maximum(na, nb), axis=0, keepdims=True)

    stat_row = lax.broadcasted_iota(jnp.int32, (SUBLANES, HEAD_SLAB), 0)
    stat_lane = lax.broadcasted_iota(jnp.int32, (SUBLANES, HEAD_SLAB), 1)
    norms = jnp.zeros((SUBLANES, HEAD_SLAB), F32)
    for h in range(N_HEADS):
        lo, hi = h * HEAD_SLAB, (h + 1) * HEAD_SLAB
        factor = _bf16_terms(_alibi_slope(h) * LOG2_E)
        positions = [pos_lo] * len(factor) + [pos_hi] * len(factor)
        qs, ks = zq[:, lo:hi], zk[:, lo:hi]
        norms = jnp.where((stat_row == 0) & (stat_lane == h), max_sq_norm(qs), norms)
        norms = jnp.where((stat_row == 1) & (stat_lane == h), max_sq_norm(ks), norms)
        ka_ref[:, lo:hi] = jnp.where(in_a, ks, aug(HEAD_DIM, positions)).astype(BF16)
        kb_ref[:, lo:hi] = jnp.where(in_a, aug(0, positions), ks).astype(BF16)
        qat = jnp.where(in_a, qs, aug(HEAD_DIM, factor + factor)).T.astype(BF16)
        qbt = jnp.where(in_a, aug(0, factor + factor), qs).T.astype(BF16)
        vt = zv[:, lo:hi].T.astype(BF16)
        for c in range(tm // t):
            qat_ref[c, lo:hi, :] = qat[:, c * t:(c + 1) * t]
            qbt_ref[c, lo:hi, :] = qbt[:, c * t:(c + 1) * t]
            vt_ref[c, lo:hi, :] = vt[:, c * t:(c + 1) * t]
    norm_ref[0] = norms

    u = _gelu_tanh(proj(3 * ATT_WIDTH, SGU_WIDTH))
    vg = _gelu_tanh(proj(3 * ATT_WIDTH + SGU_WIDTH, SGU_WIDTH))
    row = lax.broadcasted_iota(jnp.int32, (SGU_CHUNK, SGU_CHUNK), 0)
    col = lax.broadcasted_iota(jnp.int32, (SGU_CHUNK, SGU_CHUNK), 1)
    tril = col <= row
    for g in range(N_GROUPS):
        lo, hi = g * GROUP_DIM, (g + 1) * GROUP_DIM
        y = vg[:, lo:hi]
        mu = jnp.mean(y, axis=-1, keepdims=True)
        yc = y - mu
        var = jnp.mean(yc * yc, axis=-1, keepdims=True)
        yn = (yc * lax.rsqrt(var + LN_EPS) * lng_ref[:, lo:hi] + lnb_ref[:, lo:hi]).astype(BF16)
        wm = jnp.where(tril, sw_ref[g], 0.0).astype(BF16)
        bias = sb_ref[g]
        n_chunks = tm // SGU_CHUNK
        chunks = jnp.concatenate([yn[c * SGU_CHUNK:(c + 1) * SGU_CHUNK, :]
                                  for c in range(n_chunks)], axis=1)
        vmix = jnp.dot(wm, chunks, preferred_element_type=F32)
        for c in range(n_chunks):
            r0, r1 = c * SGU_CHUNK, (c + 1) * SGU_CHUNK
            mixed = vmix[:, c * GROUP_DIM:(c + 1) * GROUP_DIM] + bias
            sgu_ref[r0:r1, lo:hi] = (u[r0:r1, lo:hi] * mixed).astype(BF16)


def _mix_in(layer, x2d, g, w_in, ln_g, ln_b, sgu_w, sgu_b_full):
    rows = x2d.shape[0]
    tm, t = MIX_TM, ATT_T
    grid = (rows // tm,)
    row_spec = lambda width: pl.BlockSpec((tm, width), lambda i: (i, 0))
    tr_spec = pl.BlockSpec((tm // t, ATT_WIDTH, t), lambda i: (i, 0, 0))
    whole = pl.BlockSpec(memory_space=pltpu.VMEM)
    row_sds = jax.ShapeDtypeStruct((rows, ATT_WIDTH), BF16)
    tr_sds = jax.ShapeDtypeStruct((rows // t, ATT_WIDTH, t), BF16)
    norm_spec = pl.BlockSpec((1, SUBLANES, HEAD_SLAB), lambda i: (i, 0, 0))
    norm_sds = jax.ShapeDtypeStruct((rows // tm, SUBLANES, HEAD_SLAB), F32)
    return pl.pallas_call(
        functools.partial(_mixin_kernel, tm=tm, t=t),
        grid=grid,
        in_specs=[row_spec(D_MODEL), whole, _layer_spec(layer, w_in.shape[1:]), whole, whole,
                  _layer_spec(layer, sgu_w.shape[1:]), whole],
        out_specs=[tr_spec, tr_spec, row_spec(ATT_WIDTH), row_spec(ATT_WIDTH), tr_spec, norm_spec,
                   row_spec(SGU_WIDTH)],
        out_shape=[tr_sds, tr_sds, row_sds, row_sds, tr_sds, norm_sds, row_sds],
        compiler_params=pltpu.CompilerParams(
            dimension_semantics=("arbitrary",), vmem_limit_bytes=VMEM_LIMIT_BYTES),
        name="mix_in",
    )(x2d, g, w_in, ln_g, ln_b, sgu_w, sgu_b_full)


N_MAPS = 2


def _score_block(qi, kj, masked, q_refs, k_refs, s_ref, bm_ref, *, t):
    k0 = pl.multiple_of(kj * t, t)
    if masked:
        key = lax.broadcasted_iota(jnp.int32, (t, t), 0)
        qry = lax.broadcasted_iota(jnp.int32, (t, t), 1)
        causal = key <= qry
    for mp in range(N_MAPS):
        st = jnp.dot(k_refs[mp][pl.ds(k0, t), :], q_refs[mp][qi], preferred_element_type=F32)
        if masked:
            st = jnp.where(causal, st, NEG_INF)
        s_ref[mp] = st
        bm_ref[mp] = jnp.max(st, axis=0, keepdims=True)


def _softmax_block(qi, c, vt_blk, s_ref, bm_ref, m_ref, l_ref, acc_ref, first):
    for mp in range(N_MAPS):
        bm = bm_ref[mp] + c
        m_new = bm if first else jnp.maximum(m_ref[qi, mp], bm)
        p = jnp.exp2(s_ref[mp] - (m_new - c))
        ps = jnp.sum(p, axis=0, keepdims=True)
        pv = jnp.dot(vt_blk, p.astype(BF16), preferred_element_type=F32)
        if first:
            l_ref[qi, mp] = ps
            acc_ref[qi, mp] = pv
        else:
            alpha = jnp.exp2(m_ref[qi, mp] - m_new)
            l_ref[qi, mp] = alpha * l_ref[qi, mp] + ps
            acc_ref[qi, mp] = alpha * acc_ref[qi, mp] + pv
        m_ref[qi, mp] = m_new


def _attn_kernel(pair_q_ref, pair_k_ref, pair_n_ref,
                 slopes_ref, lq1_ref, lk1_ref, lq2_ref, lk2_ref, sg_ref,
                 qat_ref, qbt_ref, ka_ref, kb_ref, vt_ref, o_ref,
                 m_ref, l_ref, acc_ref, sa_ref, bma_ref, sb_ref, bmb_ref, *, t, nq, lam_init):
    bh = pl.program_id(0) * N_HEADS + pl.program_id(1)
    slope = slopes_ref[pl.program_id(1)]
    n_pairs = pair_n_ref[bh]
    score = functools.partial(_score_block, q_refs=(qat_ref, qbt_ref), k_refs=(ka_ref, kb_ref), t=t)
    softmax = functools.partial(_softmax_block, m_ref=m_ref, l_ref=l_ref, acc_ref=acc_ref)
    buf_a = dict(s_ref=sa_ref, bm_ref=bma_ref)
    buf_b = dict(s_ref=sb_ref, bm_ref=bmb_ref)

    bufs = (buf_a, buf_b)

    score(0, 0, True, **buf_a)

    def diag_group(n, carry):
        for u in range(ATT_UNROLL):
            qi = ATT_UNROLL * n + u
            nxt = jnp.minimum(qi + 1, nq - 1)
            score(nxt, nxt, True, **bufs[(u + 1) % 2])
            softmax(qi, 0.0, vt_ref[qi], first=True, **bufs[u % 2])
        return carry

    lax.fori_loop(0, nq // ATT_UNROLL, diag_group, 0)

    def entry(e):
        qi, kj = pair_q_ref[bh, e], pair_k_ref[bh, e]
        c = jnp.where(e < n_pairs, ((kj - qi) * t).astype(F32) * slope, NEG_INF)
        return qi, kj, c

    def score_entry(e, buf):
        qi, kj, _ = entry(e)
        score(qi, kj, False, **buf)

    def softmax_entry(e, buf):
        qi, kj, c = entry(e)
        softmax(qi, c, vt_ref[kj], first=False, **buf)

    score_entry(0, buf_a)

    def off_group(n, carry):
        for u in range(ATT_UNROLL):
            e = ATT_UNROLL * n + u
            score_entry(e + 1, bufs[(u + 1) % 2])
            softmax_entry(e, bufs[u % 2])
        return carry

    lax.fori_loop(0, (n_pairs + ATT_UNROLL - 1) // ATT_UNROLL, off_group, 0)

    lam = (jnp.exp(jnp.sum(lq1_ref[...] * lk1_ref[...], axis=-1, keepdims=True))
           - jnp.exp(jnp.sum(lq2_ref[...] * lk2_ref[...], axis=-1, keepdims=True))
           + lam_init)

    def finish(qi, carry):
        o = (acc_ref[qi, 0] * (1.0 / l_ref[qi, 0])
             - lam * (acc_ref[qi, 1] * (1.0 / l_ref[qi, 1])))
        ms = jnp.mean(o * o, axis=0, keepdims=True)
        o = o * lax.rsqrt(ms + SUBLN_EPS) * (1.0 - lam_init)
        o_ref[pl.ds(pl.multiple_of(qi * t, t), t), :] = (o.T * sg_ref[...]).astype(BF16)
        return carry

    lax.fori_loop(0, nq, finish, 0)


def _attention(pair_q, pair_k, pair_n, slopes, lq1, lk1, lq2, lk2, subln_g, qat, qbt, ka, kb, vt,
               *, batch, seq, lam_init):
    t = ATT_T
    nq = seq // t
    qt_spec = pl.BlockSpec((nq, HEAD_SLAB, t), lambda b, h, *_: (b, h, 0))
    row_spec = pl.BlockSpec((seq, HEAD_SLAB), lambda b, h, *_: (b, h))
    whole = pl.BlockSpec(memory_space=pltpu.VMEM)
    smem = pl.BlockSpec(memory_space=pltpu.SMEM)
    stat = pltpu.VMEM((nq, N_MAPS, 1, t), F32)
    acc = pltpu.VMEM((nq, N_MAPS, HEAD_SLAB, t), F32)
    score_buf = pltpu.VMEM((N_MAPS, t, t), F32)
    colmax_buf = pltpu.VMEM((N_MAPS, 1, t), F32)
    return pl.pallas_call(
        functools.partial(_attn_kernel, t=t, nq=nq, lam_init=lam_init),
        grid_spec=pltpu.PrefetchScalarGridSpec(
            num_scalar_prefetch=3,
            grid=(batch, N_HEADS),
            in_specs=[smem, whole, whole, whole, whole, whole,
                      qt_spec, qt_spec, row_spec, row_spec, qt_spec],
            out_specs=row_spec,
            scratch_shapes=[stat, stat, acc, score_buf, colmax_buf, score_buf, colmax_buf]),
        out_shape=jax.ShapeDtypeStruct((batch * seq, ATT_WIDTH), BF16),
        compiler_params=pltpu.CompilerParams(
            dimension_semantics=("arbitrary", "arbitrary"),
            vmem_limit_bytes=VMEM_LIMIT_BYTES),
        name="diff_attn",
    )(pair_q, pair_k, pair_n, slopes, lq1, lk1, lq2, lk2, subln_g, qat, qbt, ka, kb, vt)


def _off_diagonal_pairs(norms, slopes, batch, seq):
    nq = seq // ATT_T
    assert MIX_TM == ATT_T
    tile = np.arange(nq)
    below = tile[:, None] > tile[None, :]
    min_dist = ((tile[:, None] - tile[None, :] - 1) * ATT_T + 1).astype(np.float32)
    norms = norms.reshape(batch, nq, SUBLANES, HEAD_SLAB)
    qn = (jnp.sqrt(norms[:, :, 0, :N_HEADS]) * NORM_SAFETY).transpose(0, 2, 1)
    kn = (jnp.sqrt(norms[:, :, 1, :N_HEADS]) * NORM_SAFETY).transpose(0, 2, 1)
    bound = (qn[..., :, None] * (kn[..., None, :] + kn[..., :, None])
             - slopes[None, :, None, None] * min_dist)
    keep = below & ~(bound < -UNDERFLOW_LOG2)
    keep = keep.reshape(batch * N_HEADS, nq * nq)
    n_entries = nq * (nq - 1) // 2 + ATT_UNROLL
    order = jnp.argsort(~keep, axis=-1, stable=True)[:, :n_entries].astype(jnp.int32)
    return order // nq, order % nq, jnp.sum(keep, axis=-1, dtype=jnp.int32)


def _ffn_kernel(x_ref, att_ref, sgu_ref, wo_ref, g2_ref, wup_ref, cw_ref, cb_ref, wdn_ref, fg_ref,
                o_ref, carry_ref, buf_ref, act_ref, *, tm, tiles_per_batch, final):
    fc = FFN_FC
    halo = SUBLANES

    @pl.when(pl.program_id(0) % tiles_per_batch == 0)
    def _():
        carry_ref[...] = jnp.zeros_like(carry_ref)

    x1 = (x_ref[...]
          + jnp.dot(att_ref[...], wo_ref[0:ATT_WIDTH, :], preferred_element_type=F32)
          + jnp.dot(sgu_ref[...], wo_ref[ATT_WIDTH:ATT_WIDTH + SGU_WIDTH, :],
                    preferred_element_type=F32))
    ms = jnp.mean(x1 * x1, axis=-1, keepdims=True)
    h2 = (x1 * lax.rsqrt(ms + NORM_EPS) * g2_ref[...]).astype(BF16)

    for c in range(D_FF // fc):
        convs = []
        for half in range(2):
            col0 = half * D_FF + c * fc
            up = jnp.dot(h2, wup_ref[:, col0:col0 + fc], preferred_element_type=F32)
            buf_ref[half, 0:halo, :] = carry_ref[half, c]
            buf_ref[half, halo:halo + tm, :] = up
            carry_ref[half, c] = up[tm - halo:tm, :]
            w0 = cw_ref[0:1, col0:col0 + fc]
            w1 = cw_ref[1:2, col0:col0 + fc]
            w2 = cw_ref[2:3, col0:col0 + fc]
            convs.append(cb_ref[:, col0:col0 + fc]
                         + w0 * buf_ref[half, halo - 2:halo - 2 + tm, :]
                         + w1 * buf_ref[half, halo - 1:halo - 1 + tm, :]
                         + w2 * up)
        act_ref[:, c * fc:(c + 1) * fc] = (_gelu_tanh(convs[0]) * convs[1]).astype(BF16)

    out = x1 + jnp.dot(act_ref[...], wdn_ref[...], preferred_element_type=F32)
    if final:
        ms = jnp.mean(out * out, axis=-1, keepdims=True)
        out = out * lax.rsqrt(ms + NORM_EPS) * fg_ref[...]
    o_ref[...] = out


def _ffn(layer, x2d, att, sgu, w_out, g2, w_up, conv_w, conv_b, w_down, final_g, *, seq, final):
    rows = x2d.shape[0]
    tm = FFN_TM
    grid = (rows // tm,)
    row_spec = lambda width: pl.BlockSpec((tm, width), lambda i: (i, 0))
    whole = pl.BlockSpec(memory_space=pltpu.VMEM)
    return pl.pallas_call(
        functools.partial(_ffn_kernel, tm=tm, tiles_per_batch=seq // tm, final=final),
        grid=grid,
        in_specs=[row_spec(D_MODEL), row_spec(ATT_WIDTH), row_spec(SGU_WIDTH),
                  _layer_spec(layer, w_out.shape[1:]), whole, _layer_spec(layer, w_up.shape[1:]),
                  _layer_spec(layer, conv_w.shape[1:]), whole, _layer_spec(layer, w_down.shape[1:]),
                  whole],
        out_specs=row_spec(D_MODEL),
        out_shape=jax.ShapeDtypeStruct((rows, D_MODEL), F32),
        scratch_shapes=[
            pltpu.VMEM((2, D_FF // FFN_FC, SUBLANES, FFN_FC), F32),
            pltpu.VMEM((2, SUBLANES + tm, FFN_FC), F32),
            pltpu.VMEM((tm, D_FF), BF16),
        ],
        compiler_params=pltpu.CompilerParams(
            dimension_semantics=("arbitrary",), vmem_limit_bytes=VMEM_LIMIT_BYTES),
        name="ffn",
    )(x2d, att, sgu, w_out, g2, w_up, conv_w, conv_b, w_down, final_g)


def kernel(x, norm1_g, w_in, lam_q1, lam_k1, lam_q2, lam_k2, subln_g, sgu_ln_g, sgu_ln_b,
           sgu_w, sgu_b, w_out, norm2_g, ffn_w_up, ffn_conv_w, ffn_conv_b, ffn_w_down, final_g):
    batch, seq, _ = x.shape
    rows = batch * seq
    x2d = x.reshape(rows, D_MODEL)
    slopes = jnp.asarray([_alibi_slope(h) * LOG2_E for h in range(N_HEADS)], dtype=F32)
    row1 = lambda a: a.reshape(1, -1).astype(F32)
    w_in, w_out, w_up, w_down = (w.astype(BF16) for w in (w_in, w_out, ffn_w_up, ffn_w_down))
    sgu_w, conv_w = sgu_w.astype(F32), ffn_conv_w.astype(F32)
    for l in range(DEPTH):
        lam_init = _lambda_init(l)
        sgu_b_full = jnp.broadcast_to(sgu_b[l].astype(F32)[:, :, None],
                                      (N_GROUPS, SGU_CHUNK, GROUP_DIM))
        qat, qbt, ka, kb, vt, norms, sgu = _mix_in(l, x2d, row1(norm1_g[l]), w_in,
                                                   row1(sgu_ln_g[l]), row1(sgu_ln_b[l]),
                                                   sgu_w, sgu_b_full)
        pair_q, pair_k, pair_n = _off_diagonal_pairs(norms, slopes, batch, seq)
        att = _attention(pair_q, pair_k, pair_n, slopes,
                         row1(lam_q1[l]), row1(lam_k1[l]), row1(lam_q2[l]), row1(lam_k2[l]),
                         row1(subln_g[l]), qat, qbt, ka, kb, vt,
                         batch=batch, seq=seq, lam_init=lam_init)
        x2d = _ffn(l, x2d, att, sgu, w_out, row1(norm2_g[l]), w_up, conv_w, row1(ffn_conv_b[l]),
                   w_down, row1(final_g), seq=seq, final=(l == DEPTH - 1))
    return x2d.reshape(batch, seq, D_MODEL)
```

```python
import functools
import math

import jax
import jax.numpy as jnp
import numpy as np
from jax import lax
from jax.experimental import pallas as pl
from jax.experimental.pallas import tpu as pltpu

F32 = jnp.float32
BF16 = jnp.bfloat16

D_MODEL = 1024
DEPTH = 2
N_HEADS = 4
HEAD_DIM = 64
HEAD_SLAB = 2 * HEAD_DIM
ATT_WIDTH = N_HEADS * HEAD_SLAB
N_GROUPS = 4
SGU_CHUNK = 128
GROUP_DIM = 128
SGU_WIDTH = N_GROUPS * GROUP_DIM
D_FF = 2816
CONV_WIDTH = 3
NORM_EPS = 1e-6
SUBLN_EPS = 1e-5
LN_EPS = 1e-5
NEG_INF = -1e30
LOG2_E = math.log2(math.e)

SUBLANES = 8
BF16_EXACT_INT = 256
UNDERFLOW_LOG2 = 150.0
NORM_SAFETY = 1.001
VMEM_LIMIT_BYTES = 56 * 1024 * 1024

MIX_TM = 512
ATT_T = 512
ATT_UNROLL = 4
FFN_TM = 512
FFN_FC = 256


def _gelu_tanh(x):
    c = math.sqrt(2.0 / math.pi)
    return 0.5 * x * (1.0 + jnp.tanh(c * (x + 0.044715 * (x * x * x))))


def _gelu_tanh_x2(x):
    c = math.sqrt(2.0 / math.pi)
    return x * (1.0 + jnp.tanh(x * (c + (c * 0.044715) * (x * x))))


def _alibi_slope(head):
    return 2.0 ** (-8.0 * (head + 1) / N_HEADS)


def _bf16_terms(value, n=3):
    terms = []
    rest = value
    for _ in range(n):
        term = float(np.asarray(rest, dtype=BF16).astype(np.float32))
        terms.append(term)
        rest -= term
    return terms


def _lambda_init(layer_idx):
    return 0.8 - 0.6 * math.exp(-0.3 * layer_idx)


def _layer_spec(layer, shape):
    zeros = (0,) * len(shape)
    return pl.BlockSpec((None,) + tuple(shape), lambda *_: (layer,) + zeros,
                        pipeline_mode=pl.Buffered(1))


def _mixin_kernel(x_ref, g_ref, w_ref, lng_ref, lnb_ref, sw_ref, sb_ref,
                  qat_ref, qbt_ref, ka_ref, kb_ref, vt_ref, norm_ref, sgu_ref, *, tm, t):
    x = x_ref[...]
    ms = jnp.mean(x * x, axis=-1, keepdims=True)
    hb = (x * lax.rsqrt(ms + NORM_EPS) * g_ref[...]).astype(BF16)

    def proj(c0, width):
        return jnp.dot(hb, w_ref[:, c0:c0 + width], preferred_element_type=F32)

    zq = proj(0, ATT_WIDTH) * (HEAD_DIM ** -0.5 * LOG2_E)
    zk = proj(ATT_WIDTH, ATT_WIDTH)
    zv = proj(2 * ATT_WIDTH, ATT_WIDTH)
    lane = lax.broadcasted_iota(jnp.int32, (tm, HEAD_SLAB), 1)
    pos = lax.broadcasted_iota(jnp.int32, (tm, HEAD_SLAB), 0) % t
    pos_lo = (pos % BF16_EXACT_INT).astype(F32)
    pos_hi = (pos - pos % BF16_EXACT_INT).astype(F32)
    in_a = lane < HEAD_DIM

    def aug(first_lane, values):
        out = jnp.zeros((tm, HEAD_SLAB), F32)
        for n, val in enumerate(values):
            out = jnp.where(lane == first_lane + n, val, out)
        return out

    def max_sq_norm(slab):
        sq = slab.astype(BF16).astype(F32) ** 2
        na = jnp.sum(jnp.where(in_a, sq, 0.0), axis=-1, keepdims=True)
        nb = jnp.sum(jnp.where(in_a, 0.0, sq), axis=-1, keepdims=True)
        return jnp.max(jnp.maximum(na, nb), axis=0, keepdims=True)

    stat_row = lax.broadcasted_iota(jnp.int32, (SUBLANES, HEAD_SLAB), 0)
    stat_lane = lax.broadcasted_iota(jnp.int32, (SUBLANES, HEAD_SLAB), 1)
    norms = jnp.zeros((SUBLANES, HEAD_SLAB), F32)
    for h in range(N_HEADS):
        lo, hi = h * HEAD_SLAB, (h + 1) * HEAD_SLAB
        factor = _bf16_terms(_alibi_slope(h) * LOG2_E)
        positions = [pos_lo] * len(factor) + [pos_hi] * len(factor)
        qs, ks = zq[:, lo:hi], zk[:, lo:hi]
        norms = jnp.where((stat_row == 0) & (stat_lane == h), max_sq_norm(qs), norms)
        norms = jnp.where((stat_row == 1) & (stat_lane == h), max_sq_norm(ks), norms)
        ka_ref[:, lo:hi] = jnp.where(in_a, ks, aug(HEAD_DIM, positions)).astype(BF16)
        kb_ref[:, lo:hi] = jnp.where(in_a, aug(0, positions), ks).astype(BF16)
        qat = jnp.where(in_a, qs, aug(HEAD_DIM, factor + factor)).T.astype(BF16)
        qbt = jnp.where(in_a, aug(0, factor + factor), qs).T.astype(BF16)
        vt = zv[:, lo:hi].T.astype(BF16)
        for c in range(tm // t):
            qat_ref[c, lo:hi, :] = qat[:, c * t:(c + 1) * t]
            qbt_ref[c, lo:hi, :] = qbt[:, c * t:(c + 1) * t]
            vt_ref[c, lo:hi, :] = vt[:, c * t:(c + 1) * t]
    norm_ref[0] = norms

    u = _gelu_tanh(proj(3 * ATT_WIDTH, SGU_WIDTH))
    vg = _gelu_tanh(proj(3 * ATT_WIDTH + SGU_WIDTH, SGU_WIDTH))
    row = lax.broadcasted_iota(jnp.int32, (SGU_CHUNK, SGU_CHUNK), 0)
    col = lax.broadcasted_iota(jnp.int32, (SGU_CHUNK, SGU_CHUNK), 1)
    tril = col <= row
    for g in range(N_GROUPS):
        lo, hi = g * GROUP_DIM, (g + 1) * GROUP_DIM
        y = vg[:, lo:hi]
        mu = jnp.mean(y, axis=-1, keepdims=True)
        yc = y - mu
        var = jnp.mean(yc * yc, axis=-1, keepdims=True)
        yn = (yc * lax.rsqrt(var + LN_EPS) * lng_ref[:, lo:hi] + lnb_ref[:, lo:hi]).astype(BF16)
        wm = jnp.where(tril, sw_ref[g], 0.0).astype(BF16)
        bias = sb_ref[g]
        n_chunks = tm // SGU_CHUNK
        chunks = jnp.concatenate([yn[c * SGU_CHUNK:(c + 1) * SGU_CHUNK, :]
                                  for c in range(n_chunks)], axis=1)
        vmix = jnp.dot(wm, chunks, preferred_element_type=F32)
        for c in range(n_chunks):
            r0, r1 = c * SGU_CHUNK, (c + 1) * SGU_CHUNK
            mixed = vmix[:, c * GROUP_DIM:(c + 1) * GROUP_DIM] + bias
            sgu_ref[r0:r1, lo:hi] = (u[r0:r1, lo:hi] * mixed).astype(BF16)


def _mix_in(layer, x2d, g, w_in, ln_g, ln_b, sgu_w, sgu_b_full):
    rows = x2d.shape[0]
    tm, t = MIX_TM, ATT_T
    grid = (rows // tm,)
    row_spec = lambda width: pl.BlockSpec((tm, width), lambda i: (i, 0))
    tr_spec = pl.BlockSpec((tm // t, ATT_WIDTH, t), lambda i: (i, 0, 0))
    whole = pl.BlockSpec(memory_space=pltpu.VMEM)
    row_sds = jax.ShapeDtypeStruct((rows, ATT_WIDTH), BF16)
    tr_sds = jax.ShapeDtypeStruct((rows // t, ATT_WIDTH, t), BF16)
    norm_spec = pl.BlockSpec((1, SUBLANES, HEAD_SLAB), lambda i: (i, 0, 0))
    norm_sds = jax.ShapeDtypeStruct((rows // tm, SUBLANES, HEAD_SLAB), F32)
    return pl.pallas_call(
        functools.partial(_mixin_kernel, tm=tm, t=t),
        grid=grid,
        in_specs=[row_spec(D_MODEL), whole, _layer_spec(layer, w_in.shape[1:]), whole, whole,
                  _layer_spec(layer, sgu_w.shape[1:]), whole],
        out_specs=[tr_spec, tr_spec, row_spec(ATT_WIDTH), row_spec(ATT_WIDTH), tr_spec, norm_spec,
                   row_spec(SGU_WIDTH)],
        out_shape=[tr_sds, tr_sds, row_sds, row_sds, tr_sds, norm_sds, row_sds],
        compiler_params=pltpu.CompilerParams(
            dimension_semantics=("arbitrary",), vmem_limit_bytes=VMEM_LIMIT_BYTES),
        name="mix_in",
    )(x2d, g, w_in, ln_g, ln_b, sgu_w, sgu_b_full)


N_MAPS = 2


def _score_block(qi, kj, masked, q_refs, k_refs, s_ref, bm_ref, *, t):
    k0 = pl.multiple_of(kj * t, t)
    if masked:
        key = lax.broadcasted_iota(jnp.int32, (t, t), 0)
        qry = lax.broadcasted_iota(jnp.int32, (t, t), 1)
        causal = key <= qry
    for mp in range(N_MAPS):
        st = jnp.dot(k_refs[mp][pl.ds(k0, t), :], q_refs[mp][qi], preferred_element_type=F32)
        if masked:
            st = jnp.where(causal, st, NEG_INF)
        s_ref[mp] = st
        bm_ref[mp] = jnp.max(st, axis=0, keepdims=True)


def _softmax_block(qi, c, vt_blk, s_ref, bm_ref, m_ref, l_ref, acc_ref, first):
    for mp in range(N_MAPS):
        bm = bm_ref[mp] + c
        m_new = bm if first else jnp.maximum(m_ref[qi, mp], bm)
        p = jnp.exp2(s_ref[mp] - (m_new - c))
        ps = jnp.sum(p, axis=0, keepdims=True)
        pv = jnp.dot(vt_blk, p.astype(BF16), preferred_element_type=F32)
        if first:
            l_ref[qi, mp] = ps
            acc_ref[qi, mp] = pv
        else:
            alpha = jnp.exp2(m_ref[qi, mp] - m_new)
            l_ref[qi, mp] = alpha * l_ref[qi, mp] + ps
            acc_ref[qi, mp] = alpha * acc_ref[qi, mp] + pv
        m_ref[qi, mp] = m_new


def _attn_kernel(pair_q_ref, pair_k_ref, pair_n_ref,
                 slopes_ref, lq1_ref, lk1_ref, lq2_ref, lk2_ref, sg_ref,
                 qat_ref, qbt_ref, ka_ref, kb_ref, vt_ref, o_ref,
                 m_ref, l_ref, acc_ref, sa_ref, bma_ref, sb_ref, bmb_ref, *, t, nq, lam_init):
    bh = pl.program_id(0) * N_HEADS + pl.program_id(1)
    slope = slopes_ref[pl.program_id(1)]
    n_pairs = pair_n_ref[bh]
    score = functools.partial(_score_block, q_refs=(qat_ref, qbt_ref), k_refs=(ka_ref, kb_ref), t=t)
    softmax = functools.partial(_softmax_block, m_ref=m_ref, l_ref=l_ref, acc_ref=acc_ref)
    buf_a = dict(s_ref=sa_ref, bm_ref=bma_ref)
    buf_b = dict(s_ref=sb_ref, bm_ref=bmb_ref)

    bufs = (buf_a, buf_b)

    score(0, 0, True, **buf_a)

    def diag_group(n, carry):
        for u in range(ATT_UNROLL):
            qi = ATT_UNROLL * n + u
            nxt = jnp.minimum(qi + 1, nq - 1)
            score(nxt, nxt, True, **bufs[(u + 1) % 2])
            softmax(qi, 0.0, vt_ref[qi], first=True, **bufs[u % 2])
        return carry

    lax.fori_loop(0, nq // ATT_UNROLL, diag_group, 0)

    def entry(e):
        qi, kj = pair_q_ref[bh, e], pair_k_ref[bh, e]
        c = jnp.where(e < n_pairs, ((kj - qi) * t).astype(F32) * slope, NEG_INF)
        return qi, kj, c

    def score_entry(e, buf):
        qi, kj, _ = entry(e)
        score(qi, kj, False, **buf)

    def softmax_entry(e, buf):
        qi, kj, c = entry(e)
        softmax(qi, c, vt_ref[kj], first=False, **buf)

    score_entry(0, buf_a)

    def off_group(n, carry):
        for u in range(ATT_UNROLL):
            e = ATT_UNROLL * n + u
            score_entry(e + 1, bufs[(u + 1) % 2])
            softmax_entry(e, bufs[u % 2])
        return carry

    lax.fori_loop(0, (n_pairs + ATT_UNROLL - 1) // ATT_UNROLL, off_group, 0)

    lam = (jnp.exp(jnp.sum(lq1_ref[...] * lk1_ref[...], axis=-1, keepdims=True))
           - jnp.exp(jnp.sum(lq2_ref[...] * lk2_ref[...], axis=-1, keepdims=True))
           + lam_init)

    def finish(qi, carry):
        o = (acc_ref[qi, 0] * (1.0 / l_ref[qi, 0])
             - lam * (acc_ref[qi, 1] * (1.0 / l_ref[qi, 1])))
        ms = jnp.mean(o * o, axis=0, keepdims=True)
        o = o * lax.rsqrt(ms + SUBLN_EPS) * (1.0 - lam_init)
        o_ref[pl.ds(pl.multiple_of(qi * t, t), t), :] = (o.T * sg_ref[...]).astype(BF16)
        return carry

    lax.fori_loop(0, nq, finish, 0)


def _attention(pair_q, pair_k, pair_n, slopes, lq1, lk1, lq2, lk2, subln_g, qat, qbt, ka, kb, vt,
               *, batch, seq, lam_init):
    t = ATT_T
    nq = seq // t
    qt_spec = pl.BlockSpec((nq, HEAD_SLAB, t), lambda b, h, *_: (b, h, 0))
    row_spec = pl.BlockSpec((seq, HEAD_SLAB), lambda b, h, *_: (b, h))
    whole = pl.BlockSpec(memory_space=pltpu.VMEM)
    smem = pl.BlockSpec(memory_space=pltpu.SMEM)
    stat = pltpu.VMEM((nq, N_MAPS, 1, t), F32)
    acc = pltpu.VMEM((nq, N_MAPS, HEAD_SLAB, t), F32)
    score_buf = pltpu.VMEM((N_MAPS, t, t), F32)
    colmax_buf = pltpu.VMEM((N_MAPS, 1, t), F32)
    return pl.pallas_call(
        functools.partial(_attn_kernel, t=t, nq=nq, lam_init=lam_init),
        grid_spec=pltpu.PrefetchScalarGridSpec(
            num_scalar_prefetch=3,
            grid=(batch, N_HEADS),
            in_specs=[smem, whole, whole, whole, whole, whole,
                      qt_spec, qt_spec, row_spec, row_spec, qt_spec],
            out_specs=row_spec,
            scratch_shapes=[stat, stat, acc, score_buf, colmax_buf, score_buf, colmax_buf]),
        out_shape=jax.ShapeDtypeStruct((batch * seq, ATT_WIDTH), BF16),
        compiler_params=pltpu.CompilerParams(
            dimension_semantics=("arbitrary", "arbitrary"),
            vmem_limit_bytes=VMEM_LIMIT_BYTES),
        name="diff_attn",
    )(pair_q, pair_k, pair_n, slopes, lq1, lk1, lq2, lk2, subln_g, qat, qbt, ka, kb, vt)


def _off_diagonal_pairs(norms, slopes, batch, seq):
    nq = seq // ATT_T
    assert MIX_TM == ATT_T
    tile = np.arange(nq)
    below = tile[:, None] > tile[None, :]
    min_dist = ((tile[:, None] - tile[None, :] - 1) * ATT_T + 1).astype(np.float32)
    norms = norms.reshape(batch, nq, SUBLANES, HEAD_SLAB)
    qn = (jnp.sqrt(norms[:, :, 0, :N_HEADS]) * NORM_SAFETY).transpose(0, 2, 1)
    kn = (jnp.sqrt(norms[:, :, 1, :N_HEADS]) * NORM_SAFETY).transpose(0, 2, 1)
    bound = (qn[..., :, None] * (kn[..., None, :] + kn[..., :, None])
             - slopes[None, :, None, None] * min_dist)
    keep = below & ~(bound < -UNDERFLOW_LOG2)
    keep = keep.reshape(batch * N_HEADS, nq * nq)
    n_entries = nq * (nq - 1) // 2 + ATT_UNROLL
    order = jnp.argsort(~keep, axis=-1, stable=True)[:, :n_entries].astype(jnp.int32)
    return order // nq, order % nq, jnp.sum(keep, axis=-1, dtype=jnp.int32)


def _ffn_kernel(x_ref, att_ref, sgu_ref, wo_ref, g2_ref, wup_ref, cw_ref, cb_ref, wdn_ref, fg_ref,
                o_ref, carry_ref, act_ref, *, tm, tiles_per_batch, final):
    fc = FFN_FC
    halo = SUBLANES

    @pl.when(pl.program_id(0) % tiles_per_batch == 0)
    def _():
        carry_ref[...] = jnp.zeros_like(carry_ref)

    x1 = (x_ref[...]
          + jnp.dot(att_ref[...], wo_ref[0:ATT_WIDTH, :], preferred_element_type=F32)
          + jnp.dot(sgu_ref[...], wo_ref[ATT_WIDTH:ATT_WIDTH + SGU_WIDTH, :],
                    preferred_element_type=F32))
    ms = jnp.mean(x1 * x1, axis=-1, keepdims=True)
    h2 = (x1 * lax.rsqrt(ms + NORM_EPS) * g2_ref[...]).astype(BF16)

    rows8 = lax.broadcasted_iota(jnp.int32, (halo, fc), 0)
    for c in range(D_FF // fc):
        convs = []
        for half in range(2):
            col0 = half * D_FF + c * fc
            up = jnp.dot(h2, wup_ref[:, col0:col0 + fc], preferred_element_type=F32)
            prev = carry_ref[half, c]
            carry_ref[half, c] = up[tm - halo:tm, :]
            scale = 0.5 if half == 1 else 1.0
            w0 = cw_ref[0:1, col0:col0 + fc] * scale
            w1 = cw_ref[1:2, col0:col0 + fc] * scale
            w2 = cw_ref[2:3, col0:col0 + fc] * scale
            bias = cb_ref[:, col0:col0 + fc] * scale

            def conv(cur, back1, back2):
                return bias + w0 * back2 + w1 * back1 + w2 * cur

            body = conv(up, pltpu.roll(up, 1, axis=0), pltpu.roll(up, 2, axis=0))
            head = up[0:halo, :]
            head1 = jnp.where(rows8 < 1, pltpu.roll(prev, 1, axis=0), pltpu.roll(head, 1, axis=0))
            head2 = jnp.where(rows8 < 2, pltpu.roll(prev, 2, axis=0), pltpu.roll(head, 2, axis=0))
            convs.append(jnp.concatenate([conv(head, head1, head2), body[halo:, :]], axis=0))
        act_ref[:, c * fc:(c + 1) * fc] = (_gelu_tanh_x2(convs[0]) * convs[1]).astype(BF16)

    out = x1 + jnp.dot(act_ref[...], wdn_ref[...], preferred_element_type=F32)
    if final:
        ms = jnp.mean(out * out, axis=-1, keepdims=True)
        out = out * lax.rsqrt(ms + NORM_EPS) * fg_ref[...]
    o_ref[...] = out


def _ffn(layer, x2d, att, sgu, w_out, g2, w_up, conv_w, conv_b, w_down, final_g, *, seq, final):
    rows = x2d.shape[0]
    tm = FFN_TM
    grid = (rows // tm,)
    row_spec = lambda width: pl.BlockSpec((tm, width), lambda i: (i, 0))
    whole = pl.BlockSpec(memory_space=pltpu.VMEM)
    return pl.pallas_call(
        functools.partial(_ffn_kernel, tm=tm, tiles_per_batch=seq // tm, final=final),
        grid=grid,
        in_specs=[row_spec(D_MODEL), row_spec(ATT_WIDTH), row_spec(SGU_WIDTH),
                  _layer_spec(layer, w_out.shape[1:]), whole, _layer_spec(layer, w_up.shape[1:]),
                  _layer_spec(layer, conv_w.shape[1:]), whole, _layer_spec(layer, w_down.shape[1:]),
                  whole],
        out_specs=row_spec(D_MODEL),
        out_shape=jax.ShapeDtypeStruct((rows, D_MODEL), F32),
        scratch_shapes=[
            pltpu.VMEM((2, D_FF // FFN_FC, SUBLANES, FFN_FC), F32),
            pltpu.VMEM((tm, D_FF), BF16),
        ],
        compiler_params=pltpu.CompilerParams(
            dimension_semantics=("arbitrary",), vmem_limit_bytes=VMEM_LIMIT_BYTES),
        name="ffn",
    )(x2d, att, sgu, w_out, g2, w_up, conv_w, conv_b, w_down, final_g)


def kernel(x, norm1_g, w_in, lam_q1, lam_k1, lam_q2, lam_k2, subln_g, sgu_ln_g, sgu_ln_b,
           sgu_w, sgu_b, w_out, norm2_g, ffn_w_up, ffn_conv_w, ffn_conv_b, ffn_w_down, final_g):
    batch, seq, _ = x.shape
    rows = batch * seq
    x2d = x.reshape(rows, D_MODEL)
    slopes = jnp.asarray([_alibi_slope(h) * LOG2_E for h in range(N_HEADS)], dtype=F32)
    row1 = lambda a: a.reshape(1, -1).astype(F32)
    w_in, w_out, w_up, w_down = (w.astype(BF16) for w in (w_in, w_out, ffn_w_up, ffn_w_down))
    sgu_w, conv_w = sgu_w.astype(F32), ffn_conv_w.astype(F32)
    for l in range(DEPTH):
        lam_init = _lambda_init(l)
        sgu_b_full = jnp.broadcast_to(sgu_b[l].astype(F32)[:, :, None],
                                      (N_GROUPS, SGU_CHUNK, GROUP_DIM))
        qat, qbt, ka, kb, vt, norms, sgu = _mix_in(l, x2d, row1(norm1_g[l]), w_in,
                                                   row1(sgu_ln_g[l]), row1(sgu_ln_b[l]),
                                                   sgu_w, sgu_b_full)
        pair_q, pair_k, pair_n = _off_diagonal_pairs(norms, slopes, batch, seq)
        att = _attention(pair_q, pair_k, pair_n, slopes,
                         row1(lam_q1[l]), row1(lam_k1[l]), row1(lam_q2[l]), row1(lam_k2[l]),
                         row1(subln_g[l]), qat, qbt, ka, kb, vt,
                         batch=batch, seq=seq, lam_init=lam_init)
        x2d = _ffn(l, x2d, att, sgu, w_out, row1(norm2_g[l]), w_up, conv_w, row1(ffn_conv_b[l]),
                   w_down, row1(final_g), seq=seq, final=(l == DEPTH - 1))
    return x2d.reshape(batch, seq, D_MODEL)
```

```python
import functools
import math

import jax
import jax.numpy as jnp
import numpy as np
from jax import lax
from jax.experimental import pallas as pl
from jax.experimental.pallas import tpu as pltpu

F32 = jnp.float32
BF16 = jnp.bfloat16

D_MODEL = 1024
DEPTH = 2
N_HEADS = 4
N_MAPS = 2
HEAD_DIM = 64
HEAD_SLAB = 2 * HEAD_DIM
ATT_WIDTH = N_HEADS * HEAD_SLAB
N_GROUPS = 4
SGU_CHUNK = 128
GROUP_DIM = 128
SGU_WIDTH = N_GROUPS * GROUP_DIM
D_FF = 2816
CONV_WIDTH = 3
NORM_EPS = 1e-6
SUBLN_EPS = 1e-5
LN_EPS = 1e-5
NEG_INF = -1e30
LOG2_E = math.log2(math.e)

SUBLANES = 8
BF16_EXACT_INT = 256
UNDERFLOW_LOG2 = 150.0
NORM_SAFETY = 1.001
VMEM_LIMIT_BYTES = 56 * 1024 * 1024

MIX_TM = 512
ATT_T = 512
ATT_UNROLL = 4
FFN_TM = 512
FFN_FC = 256


def _gelu_tanh(x):
    c = math.sqrt(2.0 / math.pi)
    return 0.5 * x * (1.0 + jnp.tanh(c * (x + 0.044715 * (x * x * x))))


def _gelu_tanh_x2(x):
    c = math.sqrt(2.0 / math.pi)
    return x * (1.0 + jnp.tanh(x * (c + (c * 0.044715) * (x * x))))


def _alibi_slope(head):
    return 2.0 ** (-8.0 * (head + 1) / N_HEADS)


def _bf16_terms(value, n=3):
    terms = []
    rest = value
    for _ in range(n):
        term = float(np.asarray(rest, dtype=BF16).astype(np.float32))
        terms.append(term)
        rest -= term
    return terms


def _lambda_init(layer_idx):
    return 0.8 - 0.6 * math.exp(-0.3 * layer_idx)


def _layer_spec(layer, shape):
    zeros = (0,) * len(shape)
    return pl.BlockSpec((None,) + tuple(shape), lambda *_: (layer,) + zeros,
                        pipeline_mode=pl.Buffered(1))


def _alibi_operand_constants(t):
    pos = np.arange(t)
    n_terms = len(_bf16_terms(1.0))
    parts = [pos % BF16_EXACT_INT] * n_terms + [pos - pos % BF16_EXACT_INT] * n_terms
    parts = np.stack(parts, axis=1).astype(np.float32)
    first_lane = (HEAD_DIM, 0)
    k_aug = np.zeros((N_MAPS, t, HEAD_SLAB), np.float32)
    q_aug = np.zeros((N_HEADS * N_MAPS, HEAD_SLAB), np.float32)
    for mp, lane0 in enumerate(first_lane):
        k_aug[mp, :, lane0:lane0 + 2 * n_terms] = parts
        for h in range(N_HEADS):
            factor = _bf16_terms(_alibi_slope(h) * LOG2_E)
            q_aug[N_MAPS * h + mp, lane0:lane0 + 2 * n_terms] = factor + factor
    return jnp.asarray(k_aug), jnp.asarray(q_aug)


def _mixin_kernel(x_ref, g_ref, w_ref, lng_ref, lnb_ref, sw_ref, sb_ref, kaug_ref, qaug_ref,
                  qat_ref, qbt_ref, ka_ref, kb_ref, vt_ref, norm_ref, sgu_ref, *, tm, t):
    x = x_ref[...]
    ms = jnp.mean(x * x, axis=-1, keepdims=True)
    hb = (x * lax.rsqrt(ms + NORM_EPS) * g_ref[...]).astype(BF16)

    def proj(c0, width):
        return jnp.dot(hb, w_ref[:, c0:c0 + width], preferred_element_type=F32)

    zq = proj(0, ATT_WIDTH) * (HEAD_DIM ** -0.5 * LOG2_E)
    zk = proj(ATT_WIDTH, ATT_WIDTH)
    zv = proj(2 * ATT_WIDTH, ATT_WIDTH)
    in_a = lax.broadcasted_iota(jnp.int32, (tm, HEAD_SLAB), 1) < HEAD_DIM

    def max_sq_norm(slab):
        sq = slab.astype(BF16).astype(F32) ** 2
        return jnp.max(jnp.sum(sq, axis=-1, keepdims=True), axis=0, keepdims=True)

    stat_row = lax.broadcasted_iota(jnp.int32, (SUBLANES, HEAD_SLAB), 0)
    stat_lane = lax.broadcasted_iota(jnp.int32, (SUBLANES, HEAD_SLAB), 1)
    norms = jnp.zeros((SUBLANES, HEAD_SLAB), F32)
    for h in range(N_HEADS):
        lo, hi = h * HEAD_SLAB, (h + 1) * HEAD_SLAB
        qs, ks = zq[:, lo:hi], zk[:, lo:hi]
        norms = jnp.where((stat_row == 0) & (stat_lane == h), max_sq_norm(qs), norms)
        norms = jnp.where((stat_row == 1) & (stat_lane == h), max_sq_norm(ks), norms)
        ka_ref[:, lo:hi] = jnp.where(in_a, ks, kaug_ref[0]).astype(BF16)
        kb_ref[:, lo:hi] = jnp.where(in_a, kaug_ref[1], ks).astype(BF16)
        q_rows = qaug_ref[N_MAPS * h:N_MAPS * (h + 1), :]
        qat_ref[0, lo:hi, :] = jnp.where(in_a, qs, q_rows[0:1, :]).T.astype(BF16)
        qbt_ref[0, lo:hi, :] = jnp.where(in_a, q_rows[1:2, :], qs).T.astype(BF16)
        vt_ref[0, lo:hi, :] = zv[:, lo:hi].T.astype(BF16)
    norm_ref[0] = norms

    u = _gelu_tanh(proj(3 * ATT_WIDTH, SGU_WIDTH))
    vg = _gelu_tanh(proj(3 * ATT_WIDTH + SGU_WIDTH, SGU_WIDTH))
    row = lax.broadcasted_iota(jnp.int32, (SGU_CHUNK, SGU_CHUNK), 0)
    col = lax.broadcasted_iota(jnp.int32, (SGU_CHUNK, SGU_CHUNK), 1)
    tril = col <= row
    for g in range(N_GROUPS):
        lo, hi = g * GROUP_DIM, (g + 1) * GROUP_DIM
        y = vg[:, lo:hi]
        mu = jnp.mean(y, axis=-1, keepdims=True)
        yc = y - mu
        var = jnp.mean(yc * yc, axis=-1, keepdims=True)
        yn = (yc * lax.rsqrt(var + LN_EPS) * lng_ref[:, lo:hi] + lnb_ref[:, lo:hi]).astype(BF16)
        wm = jnp.where(tril, sw_ref[g], 0.0).astype(BF16)
        bias = sb_ref[g]
        n_chunks = tm // SGU_CHUNK
        chunks = jnp.concatenate([yn[c * SGU_CHUNK:(c + 1) * SGU_CHUNK, :]
                                  for c in range(n_chunks)], axis=1)
        vmix = jnp.dot(wm, chunks, preferred_element_type=F32)
        for c in range(n_chunks):
            r0, r1 = c * SGU_CHUNK, (c + 1) * SGU_CHUNK
            mixed = vmix[:, c * GROUP_DIM:(c + 1) * GROUP_DIM] + bias
            sgu_ref[r0:r1, lo:hi] = (u[r0:r1, lo:hi] * mixed).astype(BF16)


def _mix_in(layer, x2d, g, w_in, ln_g, ln_b, sgu_w, sgu_b_full):
    rows = x2d.shape[0]
    tm, t = MIX_TM, ATT_T
    assert tm == t
    k_aug, q_aug = _alibi_operand_constants(t)
    grid = (rows // tm,)
    row_spec = lambda width: pl.BlockSpec((tm, width), lambda i: (i, 0))
    tr_spec = pl.BlockSpec((tm // t, ATT_WIDTH, t), lambda i: (i, 0, 0))
    whole = pl.BlockSpec(memory_space=pltpu.VMEM)
    row_sds = jax.ShapeDtypeStruct((rows, ATT_WIDTH), BF16)
    tr_sds = jax.ShapeDtypeStruct((rows // t, ATT_WIDTH, t), BF16)
    norm_spec = pl.BlockSpec((1, SUBLANES, HEAD_SLAB), lambda i: (i, 0, 0))
    norm_sds = jax.ShapeDtypeStruct((rows // tm, SUBLANES, HEAD_SLAB), F32)
    return pl.pallas_call(
        functools.partial(_mixin_kernel, tm=tm, t=t),
        grid=grid,
        in_specs=[row_spec(D_MODEL), whole, _layer_spec(layer, w_in.shape[1:]), whole, whole,
                  _layer_spec(layer, sgu_w.shape[1:]), whole, whole, whole],
        out_specs=[tr_spec, tr_spec, row_spec(ATT_WIDTH), row_spec(ATT_WIDTH), tr_spec, norm_spec,
                   row_spec(SGU_WIDTH)],
        out_shape=[tr_sds, tr_sds, row_sds, row_sds, tr_sds, norm_sds, row_sds],
        compiler_params=pltpu.CompilerParams(
            dimension_semantics=("arbitrary",), vmem_limit_bytes=VMEM_LIMIT_BYTES),
        name="mix_in",
    )(x2d, g, w_in, ln_g, ln_b, sgu_w, sgu_b_full, k_aug, q_aug)


def _score_block(qi, kj, masked, q_refs, k_refs, s_ref, bm_ref, *, t):
    k0 = pl.multiple_of(kj * t, t)
    if masked:
        key = lax.broadcasted_iota(jnp.int32, (t, t), 0)
        qry = lax.broadcasted_iota(jnp.int32, (t, t), 1)
        causal = key <= qry
    for mp in range(N_MAPS):
        st = jnp.dot(k_refs[mp][pl.ds(k0, t), :], q_refs[mp][qi], preferred_element_type=F32)
        if masked:
            st = jnp.where(causal, st, NEG_INF)
        s_ref[mp] = st
        bm_ref[mp] = jnp.max(st, axis=0, keepdims=True)


def _softmax_block(qi, c, vt_blk, s_ref, bm_ref, m_ref, l_ref, acc_ref, first):
    for mp in range(N_MAPS):
        bm = bm_ref[mp] + c
        m_new = bm if first else jnp.maximum(m_ref[qi, mp], bm)
        p = jnp.exp2(s_ref[mp] - (m_new - c))
        ps = jnp.sum(p, axis=0, keepdims=True)
        pv = jnp.dot(vt_blk, p.astype(BF16), preferred_element_type=F32)
        if first:
            l_ref[qi, mp] = ps
            acc_ref[qi, mp] = pv
        else:
            alpha = jnp.exp2(m_ref[qi, mp] - m_new)
            l_ref[qi, mp] = alpha * l_ref[qi, mp] + ps
            acc_ref[qi, mp] = alpha * acc_ref[qi, mp] + pv
        m_ref[qi, mp] = m_new


def _attn_kernel(pair_q_ref, pair_k_ref, pair_n_ref,
                 slopes_ref, lq1_ref, lk1_ref, lq2_ref, lk2_ref, sg_ref,
                 qat_ref, qbt_ref, ka_ref, kb_ref, vt_ref, o_ref,
                 m_ref, l_ref, acc_ref, sa_ref, bma_ref, sb_ref, bmb_ref, *, t, nq, lam_init):
    bh = pl.program_id(0) * N_HEADS + pl.program_id(1)
    slope = slopes_ref[pl.program_id(1)]
    n_pairs = pair_n_ref[bh]
    score = functools.partial(_score_block, q_refs=(qat_ref, qbt_ref), k_refs=(ka_ref, kb_ref), t=t)
    softmax = functools.partial(_softmax_block, m_ref=m_ref, l_ref=l_ref, acc_ref=acc_ref)
    buf_a = dict(s_ref=sa_ref, bm_ref=bma_ref)
    buf_b = dict(s_ref=sb_ref, bm_ref=bmb_ref)

    bufs = (buf_a, buf_b)

    score(0, 0, True, **buf_a)

    def diag_group(n, carry):
        for u in range(ATT_UNROLL):
            qi = ATT_UNROLL * n + u
            nxt = jnp.minimum(qi + 1, nq - 1)
            score(nxt, nxt, True, **bufs[(u + 1) % 2])
            softmax(qi, 0.0, vt_ref[qi], first=True, **bufs[u % 2])
        return carry

    lax.fori_loop(0, nq // ATT_UNROLL, diag_group, 0)

    def entry(e):
        qi, kj = pair_q_ref[bh, e], pair_k_ref[bh, e]
        c = jnp.where(e < n_pairs, ((kj - qi) * t).astype(F32) * slope, NEG_INF)
        return qi, kj, c

    def score_entry(e, buf):
        qi, kj, _ = entry(e)
        score(qi, kj, False, **buf)

    def softmax_entry(e, buf):
        qi, kj, c = entry(e)
        softmax(qi, c, vt_ref[kj], first=False, **buf)

    score_entry(0, buf_a)

    def off_group(n, carry):
        for u in range(ATT_UNROLL):
            e = ATT_UNROLL * n + u
            score_entry(e + 1, bufs[(u + 1) % 2])
            softmax_entry(e, bufs[u % 2])
        return carry

    lax.fori_loop(0, (n_pairs + ATT_UNROLL - 1) // ATT_UNROLL, off_group, 0)

    lam = (jnp.exp(jnp.sum(lq1_ref[...] * lk1_ref[...], axis=-1, keepdims=True))
           - jnp.exp(jnp.sum(lq2_ref[...] * lk2_ref[...], axis=-1, keepdims=True))
           + lam_init)

    def finish(qi, carry):
        o = (acc_ref[qi, 0] * (1.0 / l_ref[qi, 0])
             - lam * (acc_ref[qi, 1] * (1.0 / l_ref[qi, 1])))
        ms = jnp.mean(o * o, axis=0, keepdims=True)
        o = o * lax.rsqrt(ms + SUBLN_EPS) * (1.0 - lam_init)
        o_ref[pl.ds(pl.multiple_of(qi * t, t), t), :] = (o.T * sg_ref[...]).astype(BF16)
        return carry

    lax.fori_loop(0, nq, finish, 0)


def _attention(pair_q, pair_k, pair_n, slopes, lq1, lk1, lq2, lk2, subln_g, qat, qbt, ka, kb, vt,
               *, batch, seq, lam_init):
    t = ATT_T
    nq = seq // t
    qt_spec = pl.BlockSpec((nq, HEAD_SLAB, t), lambda b, h, *_: (b, h, 0))
    row_spec = pl.BlockSpec((seq, HEAD_SLAB), lambda b, h, *_: (b, h))
    whole = pl.BlockSpec(memory_space=pltpu.VMEM)
    smem = pl.BlockSpec(memory_space=pltpu.SMEM)
    stat = pltpu.VMEM((nq, N_MAPS, 1, t), F32)
    acc = pltpu.VMEM((nq, N_MAPS, HEAD_SLAB, t), F32)
    score_buf = pltpu.VMEM((N_MAPS, t, t), F32)
    colmax_buf = pltpu.VMEM((N_MAPS, 1, t), F32)
    return pl.pallas_call(
        functools.partial(_attn_kernel, t=t, nq=nq, lam_init=lam_init),
        grid_spec=pltpu.PrefetchScalarGridSpec(
            num_scalar_prefetch=3,
            grid=(batch, N_HEADS),
            in_specs=[smem, whole, whole, whole, whole, whole,
                      qt_spec, qt_spec, row_spec, row_spec, qt_spec],
            out_specs=row_spec,
            scratch_shapes=[stat, stat, acc, score_buf, colmax_buf, score_buf, colmax_buf]),
        out_shape=jax.ShapeDtypeStruct((batch * seq, ATT_WIDTH), BF16),
        compiler_params=pltpu.CompilerParams(
            dimension_semantics=("arbitrary", "arbitrary"),
            vmem_limit_bytes=VMEM_LIMIT_BYTES),
        name="diff_attn",
    )(pair_q, pair_k, pair_n, slopes, lq1, lk1, lq2, lk2, subln_g, qat, qbt, ka, kb, vt)


def _off_diagonal_pairs(norms, slopes, batch, seq):
    nq = seq // ATT_T
    assert MIX_TM == ATT_T
    tile = np.arange(nq)
    below = tile[:, None] > tile[None, :]
    min_dist = ((tile[:, None] - tile[None, :] - 1) * ATT_T + 1).astype(np.float32)
    norms = norms.reshape(batch, nq, SUBLANES, HEAD_SLAB)
    qn = (jnp.sqrt(norms[:, :, 0, :N_HEADS]) * NORM_SAFETY).transpose(0, 2, 1)
    kn = (jnp.sqrt(norms[:, :, 1, :N_HEADS]) * NORM_SAFETY).transpose(0, 2, 1)
    bound = (qn[..., :, None] * (kn[..., None, :] + kn[..., :, None])
             - slopes[None, :, None, None] * min_dist)
    keep = below & ~(bound < -UNDERFLOW_LOG2)
    keep = keep.reshape(batch * N_HEADS, nq * nq)
    n_entries = nq * (nq - 1) // 2 + ATT_UNROLL
    order = jnp.argsort(~keep, axis=-1, stable=True)[:, :n_entries].astype(jnp.int32)
    return order // nq, order % nq, jnp.sum(keep, axis=-1, dtype=jnp.int32)


def _ffn_kernel(x_ref, att_ref, sgu_ref, wo_ref, g2_ref, wup_ref, cw_ref, cb_ref, wdn_ref, fg_ref,
                o_ref, carry_ref, act_ref, *, tm, tiles_per_batch, final):
    fc = FFN_FC
    halo = SUBLANES

    @pl.when(pl.program_id(0) % tiles_per_batch == 0)
    def _():
        carry_ref[...] = jnp.zeros_like(carry_ref)

    x1_parts, h2_parts = [], []
    for r0 in range(0, tm, tm // 2):
        r1 = r0 + tm // 2
        part = (x_ref[r0:r1, :]
                + jnp.dot(att_ref[r0:r1, :], wo_ref[0:ATT_WIDTH, :], preferred_element_type=F32)
                + jnp.dot(sgu_ref[r0:r1, :], wo_ref[ATT_WIDTH:ATT_WIDTH + SGU_WIDTH, :],
                          preferred_element_type=F32))
        ms = jnp.mean(part * part, axis=-1, keepdims=True)
        x1_parts.append(part)
        h2_parts.append((part * lax.rsqrt(ms + NORM_EPS) * g2_ref[...]).astype(BF16))
    x1 = jnp.concatenate(x1_parts, axis=0)
    h2 = jnp.concatenate(h2_parts, axis=0)

    rows8 = lax.broadcasted_iota(jnp.int32, (halo, fc), 0)
    for c in range(D_FF // fc):
        convs = []
        for half in range(2):
            col0 = half * D_FF + c * fc
            up = jnp.dot(h2, wup_ref[:, col0:col0 + fc], preferred_element_type=F32)
            prev = carry_ref[half, c]
            carry_ref[half, c] = up[tm - halo:tm, :]
            scale = 0.5 if half == 1 else 1.0
            w0 = cw_ref[0:1, col0:col0 + fc] * scale
            w1 = cw_ref[1:2, col0:col0 + fc] * scale
            w2 = cw_ref[2:3, col0:col0 + fc] * scale
            bias = cb_ref[:, col0:col0 + fc] * scale

            def conv(cur, back1, back2):
                return bias + w0 * back2 + w1 * back1 + w2 * cur

            body = conv(up, pltpu.roll(up, 1, axis=0), pltpu.roll(up, 2, axis=0))
            head = up[0:halo, :]
            head1 = jnp.where(rows8 < 1, pltpu.roll(prev, 1, axis=0), pltpu.roll(head, 1, axis=0))
            head2 = jnp.where(rows8 < 2, pltpu.roll(prev, 2, axis=0), pltpu.roll(head, 2, axis=0))
            convs.append(jnp.concatenate([conv(head, head1, head2), body[halo:, :]], axis=0))
        act_ref[:, c * fc:(c + 1) * fc] = (_gelu_tanh_x2(convs[0]) * convs[1]).astype(BF16)

    out = x1 + jnp.dot(act_ref[...], wdn_ref[...], preferred_element_type=F32)
    if final:
        ms = jnp.mean(out * out, axis=-1, keepdims=True)
        out = out * lax.rsqrt(ms + NORM_EPS) * fg_ref[...]
    o_ref[...] = out


def _ffn(layer, x2d, att, sgu, w_out, g2, w_up, conv_w, conv_b, w_down, final_g, *, seq, final):
    rows = x2d.shape[0]
    tm = FFN_TM
    grid = (rows // tm,)
    row_spec = lambda width: pl.BlockSpec((tm, width), lambda i: (i, 0))
    whole = pl.BlockSpec(memory_space=pltpu.VMEM)
    return pl.pallas_call(
        functools.partial(_ffn_kernel, tm=tm, tiles_per_batch=seq // tm, final=final),
        grid=grid,
        in_specs=[row_spec(D_MODEL), row_spec(ATT_WIDTH), row_spec(SGU_WIDTH),
                  _layer_spec(layer, w_out.shape[1:]), whole, _layer_spec(layer, w_up.shape[1:]),
                  _layer_spec(layer, conv_w.shape[1:]), whole, _layer_spec(layer, w_down.shape[1:]),
                  whole],
        out_specs=row_spec(D_MODEL),
        out_shape=jax.ShapeDtypeStruct((rows, D_MODEL), F32),
        scratch_shapes=[
            pltpu.VMEM((2, D_FF // FFN_FC, SUBLANES, FFN_FC), F32),
            pltpu.VMEM((tm, D_FF), BF16),
        ],
        compiler_params=pltpu.CompilerParams(
            dimension_semantics=("arbitrary",), vmem_limit_bytes=VMEM_LIMIT_BYTES),
        name="ffn",
    )(x2d, att, sgu, w_out, g2, w_up, conv_w, conv_b, w_down, final_g)


def kernel(x, norm1_g, w_in, lam_q1, lam_k1, lam_q2, lam_k2, subln_g, sgu_ln_g, sgu_ln_b,
           sgu_w, sgu_b, w_out, norm2_g, ffn_w_up, ffn_conv_w, ffn_conv_b, ffn_w_down, final_g):
    batch, seq, _ = x.shape
    rows = batch * seq
    x2d = x.reshape(rows, D_MODEL)
    slopes = jnp.asarray([_alibi_slope(h) * LOG2_E for h in range(N_HEADS)], dtype=F32)
    row1 = lambda a: a.reshape(1, -1).astype(F32)
    w_in, w_out, w_up, w_down = (w.astype(BF16) for w in (w_in, w_out, ffn_w_up, ffn_w_down))
    sgu_w, conv_w = sgu_w.astype(F32), ffn_conv_w.astype(F32)
    for l in range(DEPTH):
        lam_init = _lambda_init(l)
        sgu_b_full = jnp.broadcast_to(sgu_b[l].astype(F32)[:, :, None],
                                      (N_GROUPS, SGU_CHUNK, GROUP_DIM))
        qat, qbt, ka, kb, vt, norms, sgu = _mix_in(l, x2d, row1(norm1_g[l]), w_in,
                                                   row1(sgu_ln_g[l]), row1(sgu_ln_b[l]),
                                                   sgu_w, sgu_b_full)
        pair_q, pair_k, pair_n = _off_diagonal_pairs(norms, slopes, batch, seq)
        att = _attention(pair_q, pair_k, pair_n, slopes,
                         row1(lam_q1[l]), row1(lam_k1[l]), row1(lam_q2[l]), row1(lam_k2[l]),
                         row1(subln_g[l]), qat, qbt, ka, kb, vt,
                         batch=batch, seq=seq, lam_init=lam_init)
        x2d = _ffn(l, x2d, att, sgu, w_out, row1(norm2_g[l]), w_up, conv_w, row1(ffn_conv_b[l]),
                   w_down, row1(final_g), seq=seq, final=(l == DEPTH - 1))
    return x2d.reshape(batch, seq, D_MODEL)
```

```python
import functools
import math

import jax
import jax.numpy as jnp
import numpy as np
from jax import lax
from jax.experimental import pallas as pl
from jax.experimental.pallas import tpu as pltpu

F32 = jnp.float32
BF16 = jnp.bfloat16

D_MODEL = 1024
DEPTH = 2
N_HEADS = 4
N_MAPS = 2
HEAD_DIM = 64
HEAD_SLAB = 2 * HEAD_DIM
ATT_WIDTH = N_HEADS * HEAD_SLAB
N_GROUPS = 4
SGU_CHUNK = 128
GROUP_DIM = 128
SGU_WIDTH = N_GROUPS * GROUP_DIM
D_FF = 2816
CONV_WIDTH = 3
NORM_EPS = 1e-6
SUBLN_EPS = 1e-5
LN_EPS = 1e-5
NEG_INF = -1e30
LOG2_E = math.log2(math.e)

SUBLANES = 8
BF16_EXACT_INT = 256
UNDERFLOW_LOG2 = 150.0
NORM_SAFETY = 1.001
VMEM_LIMIT_BYTES = 56 * 1024 * 1024

MIX_TM = 512
ATT_T = 512
ATT_UNROLL = 4
FFN_TM = 1024
FFN_FC = 256


def _gelu_tanh(x):
    c = math.sqrt(2.0 / math.pi)
    return 0.5 * x * (1.0 + jnp.tanh(c * (x + 0.044715 * (x * x * x))))


def _gelu_tanh_x2(x):
    c = math.sqrt(2.0 / math.pi)
    return x * (1.0 + jnp.tanh(x * (c + (c * 0.044715) * (x * x))))


def _alibi_slope(head):
    return 2.0 ** (-8.0 * (head + 1) / N_HEADS)


def _bf16_terms(value, n=3):
    terms = []
    rest = value
    for _ in range(n):
        term = float(np.asarray(rest, dtype=BF16).astype(np.float32))
        terms.append(term)
        rest -= term
    return terms


def _lambda_init(layer_idx):
    return 0.8 - 0.6 * math.exp(-0.3 * layer_idx)


def _layer_spec(layer, shape):
    zeros = (0,) * len(shape)
    return pl.BlockSpec((None,) + tuple(shape), lambda *_: (layer,) + zeros,
                        pipeline_mode=pl.Buffered(1))


def _alibi_operand_constants(t):
    pos = np.arange(t)
    n_terms = len(_bf16_terms(1.0))
    parts = [pos % BF16_EXACT_INT] * n_terms + [pos - pos % BF16_EXACT_INT] * n_terms
    parts = np.stack(parts, axis=1).astype(np.float32)
    first_lane = (HEAD_DIM, 0)
    k_aug = np.zeros((N_MAPS, t, HEAD_SLAB), np.float32)
    q_aug = np.zeros((N_HEADS * N_MAPS, HEAD_SLAB), np.float32)
    for mp, lane0 in enumerate(first_lane):
        k_aug[mp, :, lane0:lane0 + 2 * n_terms] = parts
        for h in range(N_HEADS):
            factor = _bf16_terms(_alibi_slope(h) * LOG2_E)
            q_aug[N_MAPS * h + mp, lane0:lane0 + 2 * n_terms] = factor + factor
    return jnp.asarray(k_aug), jnp.asarray(q_aug)


def _mixin_kernel(x_ref, g_ref, w_ref, lng_ref, lnb_ref, sw_ref, sb_ref, kaug_ref, qaug_ref,
                  qat_ref, qbt_ref, ka_ref, kb_ref, vt_ref, norm_ref, sgu_ref, *, tm, t):
    x = x_ref[...]
    ms = jnp.mean(x * x, axis=-1, keepdims=True)
    hb = (x * lax.rsqrt(ms + NORM_EPS) * g_ref[...]).astype(BF16)

    def proj(c0, width):
        return jnp.dot(hb, w_ref[:, c0:c0 + width], preferred_element_type=F32)

    zq = proj(0, ATT_WIDTH) * (HEAD_DIM ** -0.5 * LOG2_E)
    zk = proj(ATT_WIDTH, ATT_WIDTH)
    zv = proj(2 * ATT_WIDTH, ATT_WIDTH)
    in_a = lax.broadcasted_iota(jnp.int32, (tm, HEAD_SLAB), 1) < HEAD_DIM

    def max_sq_norm(slab):
        sq = slab.astype(BF16).astype(F32) ** 2
        return jnp.max(jnp.sum(sq, axis=-1, keepdims=True), axis=0, keepdims=True)

    stat_row = lax.broadcasted_iota(jnp.int32, (SUBLANES, HEAD_SLAB), 0)
    stat_lane = lax.broadcasted_iota(jnp.int32, (SUBLANES, HEAD_SLAB), 1)
    norms = jnp.zeros((SUBLANES, HEAD_SLAB), F32)
    for h in range(N_HEADS):
        lo, hi = h * HEAD_SLAB, (h + 1) * HEAD_SLAB
        qs, ks = zq[:, lo:hi], zk[:, lo:hi]
        norms = jnp.where((stat_row == 0) & (stat_lane == h), max_sq_norm(qs), norms)
        norms = jnp.where((stat_row == 1) & (stat_lane == h), max_sq_norm(ks), norms)
        ka_ref[:, lo:hi] = jnp.where(in_a, ks, kaug_ref[0]).astype(BF16)
        kb_ref[:, lo:hi] = jnp.where(in_a, kaug_ref[1], ks).astype(BF16)
        q_rows = qaug_ref[N_MAPS * h:N_MAPS * (h + 1), :]
        qat_ref[0, lo:hi, :] = jnp.where(in_a, qs, q_rows[0:1, :]).T.astype(BF16)
        qbt_ref[0, lo:hi, :] = jnp.where(in_a, q_rows[1:2, :], qs).T.astype(BF16)
        vt_ref[0, lo:hi, :] = zv[:, lo:hi].T.astype(BF16)
    norm_ref[0] = norms

    u = _gelu_tanh(proj(3 * ATT_WIDTH, SGU_WIDTH))
    vg = _gelu_tanh(proj(3 * ATT_WIDTH + SGU_WIDTH, SGU_WIDTH))
    row = lax.broadcasted_iota(jnp.int32, (SGU_CHUNK, SGU_CHUNK), 0)
    col = lax.broadcasted_iota(jnp.int32, (SGU_CHUNK, SGU_CHUNK), 1)
    tril = col <= row
    for g in range(N_GROUPS):
        lo, hi = g * GROUP_DIM, (g + 1) * GROUP_DIM
        y = vg[:, lo:hi]
        mu = jnp.mean(y, axis=-1, keepdims=True)
        yc = y - mu
        var = jnp.mean(yc * yc, axis=-1, keepdims=True)
        yn = (yc * lax.rsqrt(var + LN_EPS) * lng_ref[:, lo:hi] + lnb_ref[:, lo:hi]).astype(BF16)
        wm = jnp.where(tril, sw_ref[g], 0.0).astype(BF16)
        bias = sb_ref[g]
        n_chunks = tm // SGU_CHUNK
        chunks = jnp.concatenate([yn[c * SGU_CHUNK:(c + 1) * SGU_CHUNK, :]
                                  for c in range(n_chunks)], axis=1)
        vmix = jnp.dot(wm, chunks, preferred_element_type=F32)
        for c in range(n_chunks):
            r0, r1 = c * SGU_CHUNK, (c + 1) * SGU_CHUNK
            mixed = vmix[:, c * GROUP_DIM:(c + 1) * GROUP_DIM] + bias
            sgu_ref[r0:r1, lo:hi] = (u[r0:r1, lo:hi] * mixed).astype(BF16)


def _mix_in(layer, x2d, g, w_in, ln_g, ln_b, sgu_w, sgu_b_full):
    rows = x2d.shape[0]
    tm, t = MIX_TM, ATT_T
    assert tm == t
    k_aug, q_aug = _alibi_operand_constants(t)
    grid = (rows // tm,)
    row_spec = lambda width: pl.BlockSpec((tm, width), lambda i: (i, 0))
    tr_spec = pl.BlockSpec((tm // t, ATT_WIDTH, t), lambda i: (i, 0, 0))
    whole = pl.BlockSpec(memory_space=pltpu.VMEM)
    row_sds = jax.ShapeDtypeStruct((rows, ATT_WIDTH), BF16)
    tr_sds = jax.ShapeDtypeStruct((rows // t, ATT_WIDTH, t), BF16)
    norm_spec = pl.BlockSpec((1, SUBLANES, HEAD_SLAB), lambda i: (i, 0, 0))
    norm_sds = jax.ShapeDtypeStruct((rows // tm, SUBLANES, HEAD_SLAB), F32)
    return pl.pallas_call(
        functools.partial(_mixin_kernel, tm=tm, t=t),
        grid=grid,
        in_specs=[row_spec(D_MODEL), whole, _layer_spec(layer, w_in.shape[1:]), whole, whole,
                  _layer_spec(layer, sgu_w.shape[1:]), whole, whole, whole],
        out_specs=[tr_spec, tr_spec, row_spec(ATT_WIDTH), row_spec(ATT_WIDTH), tr_spec, norm_spec,
                   row_spec(SGU_WIDTH)],
        out_shape=[tr_sds, tr_sds, row_sds, row_sds, tr_sds, norm_sds, row_sds],
        compiler_params=pltpu.CompilerParams(
            dimension_semantics=("arbitrary",), vmem_limit_bytes=VMEM_LIMIT_BYTES),
        name="mix_in",
    )(x2d, g, w_in, ln_g, ln_b, sgu_w, sgu_b_full, k_aug, q_aug)


def _score_block(qi, kj, masked, q_refs, k_refs, s_ref, bm_ref, *, t):
    k0 = pl.multiple_of(kj * t, t)
    if masked:
        key = lax.broadcasted_iota(jnp.int32, (t, t), 0)
        qry = lax.broadcasted_iota(jnp.int32, (t, t), 1)
        causal = key <= qry
    for mp in range(N_MAPS):
        st = jnp.dot(k_refs[mp][pl.ds(k0, t), :], q_refs[mp][qi], preferred_element_type=F32)
        if masked:
            st = jnp.where(causal, st, NEG_INF)
        s_ref[mp] = st
        bm_ref[mp] = jnp.max(st, axis=0, keepdims=True)


def _softmax_block(qi, c, vt_blk, s_ref, bm_ref, m_ref, l_ref, acc_ref, first):
    for mp in range(N_MAPS):
        bm = bm_ref[mp] + c
        m_new = bm if first else jnp.maximum(m_ref[qi, mp], bm)
        p = jnp.exp2(s_ref[mp] - (m_new - c))
        ps = jnp.sum(p, axis=0, keepdims=True)
        pv = jnp.dot(vt_blk, p.astype(BF16), preferred_element_type=F32)
        if first:
            l_ref[qi, mp] = ps
            acc_ref[qi, mp] = pv
        else:
            alpha = jnp.exp2(m_ref[qi, mp] - m_new)
            l_ref[qi, mp] = alpha * l_ref[qi, mp] + ps
            acc_ref[qi, mp] = alpha * acc_ref[qi, mp] + pv
        m_ref[qi, mp] = m_new


def _attn_kernel(pair_q_ref, pair_k_ref, pair_n_ref,
                 slopes_ref, lq1_ref, lk1_ref, lq2_ref, lk2_ref, sg_ref,
                 qat_ref, qbt_ref, ka_ref, kb_ref, vt_ref, o_ref,
                 m_ref, l_ref, acc_ref, sa_ref, bma_ref, sb_ref, bmb_ref, *, t, nq, lam_init):
    bh = pl.program_id(0) * N_HEADS + pl.program_id(1)
    slope = slopes_ref[pl.program_id(1)]
    n_pairs = pair_n_ref[bh]
    score = functools.partial(_score_block, q_refs=(qat_ref, qbt_ref), k_refs=(ka_ref, kb_ref), t=t)
    softmax = functools.partial(_softmax_block, m_ref=m_ref, l_ref=l_ref, acc_ref=acc_ref)
    buf_a = dict(s_ref=sa_ref, bm_ref=bma_ref)
    buf_b = dict(s_ref=sb_ref, bm_ref=bmb_ref)

    bufs = (buf_a, buf_b)

    score(0, 0, True, **buf_a)

    def diag_group(n, carry):
        for u in range(ATT_UNROLL):
            qi = ATT_UNROLL * n + u
            nxt = jnp.minimum(qi + 1, nq - 1)
            score(nxt, nxt, True, **bufs[(u + 1) % 2])
            softmax(qi, 0.0, vt_ref[qi], first=True, **bufs[u % 2])
        return carry

    lax.fori_loop(0, nq // ATT_UNROLL, diag_group, 0)

    def entry(e):
        qi, kj = pair_q_ref[bh, e], pair_k_ref[bh, e]
        c = jnp.where(e < n_pairs, ((kj - qi) * t).astype(F32) * slope, NEG_INF)
        return qi, kj, c

    def score_entry(e, buf):
        qi, kj, _ = entry(e)
        score(qi, kj, False, **buf)

    def softmax_entry(e, buf):
        qi, kj, c = entry(e)
        softmax(qi, c, vt_ref[kj], first=False, **buf)

    score_entry(0, buf_a)

    def off_group(n, carry):
        for u in range(ATT_UNROLL):
            e = ATT_UNROLL * n + u
            score_entry(e + 1, bufs[(u + 1) % 2])
            softmax_entry(e, bufs[u % 2])
        return carry

    lax.fori_loop(0, (n_pairs + ATT_UNROLL - 1) // ATT_UNROLL, off_group, 0)

    lam = (jnp.exp(jnp.sum(lq1_ref[...] * lk1_ref[...], axis=-1, keepdims=True))
           - jnp.exp(jnp.sum(lq2_ref[...] * lk2_ref[...], axis=-1, keepdims=True))
           + lam_init)

    def finish(qi, carry):
        o = (acc_ref[qi, 0] * (1.0 / l_ref[qi, 0])
             - lam * (acc_ref[qi, 1] * (1.0 / l_ref[qi, 1])))
        ms = jnp.mean(o * o, axis=0, keepdims=True)
        o = o * lax.rsqrt(ms + SUBLN_EPS) * (1.0 - lam_init)
        o_ref[pl.ds(pl.multiple_of(qi * t, t), t), :] = (o.T * sg_ref[...]).astype(BF16)
        return carry

    lax.fori_loop(0, nq, finish, 0)


def _attention(pair_q, pair_k, pair_n, slopes, lq1, lk1, lq2, lk2, subln_g, qat, qbt, ka, kb, vt,
               *, batch, seq, lam_init):
    t = ATT_T
    nq = seq // t
    qt_spec = pl.BlockSpec((nq, HEAD_SLAB, t), lambda b, h, *_: (b, h, 0))
    row_spec = pl.BlockSpec((seq, HEAD_SLAB), lambda b, h, *_: (b, h))
    whole = pl.BlockSpec(memory_space=pltpu.VMEM)
    smem = pl.BlockSpec(memory_space=pltpu.SMEM)
    stat = pltpu.VMEM((nq, N_MAPS, 1, t), F32)
    acc = pltpu.VMEM((nq, N_MAPS, HEAD_SLAB, t), F32)
    score_buf = pltpu.VMEM((N_MAPS, t, t), F32)
    colmax_buf = pltpu.VMEM((N_MAPS, 1, t), F32)
    return pl.pallas_call(
        functools.partial(_attn_kernel, t=t, nq=nq, lam_init=lam_init),
        grid_spec=pltpu.PrefetchScalarGridSpec(
            num_scalar_prefetch=3,
            grid=(batch, N_HEADS),
            in_specs=[smem, whole, whole, whole, whole, whole,
                      qt_spec, qt_spec, row_spec, row_spec, qt_spec],
            out_specs=row_spec,
            scratch_shapes=[stat, stat, acc, score_buf, colmax_buf, score_buf, colmax_buf]),
        out_shape=jax.ShapeDtypeStruct((batch * seq, ATT_WIDTH), BF16),
        compiler_params=pltpu.CompilerParams(
            dimension_semantics=("arbitrary", "arbitrary"),
            vmem_limit_bytes=VMEM_LIMIT_BYTES),
        name="diff_attn",
    )(pair_q, pair_k, pair_n, slopes, lq1, lk1, lq2, lk2, subln_g, qat, qbt, ka, kb, vt)


def _off_diagonal_pairs(norms, slopes, batch, seq):
    nq = seq // ATT_T
    assert MIX_TM == ATT_T
    tile = np.arange(nq)
    below = tile[:, None] > tile[None, :]
    min_dist = ((tile[:, None] - tile[None, :] - 1) * ATT_T + 1).astype(np.float32)
    norms = norms.reshape(batch, nq, SUBLANES, HEAD_SLAB)
    qn = (jnp.sqrt(norms[:, :, 0, :N_HEADS]) * NORM_SAFETY).transpose(0, 2, 1)
    kn = (jnp.sqrt(norms[:, :, 1, :N_HEADS]) * NORM_SAFETY).transpose(0, 2, 1)
    bound = (qn[..., :, None] * (kn[..., None, :] + kn[..., :, None])
             - slopes[None, :, None, None] * min_dist)
    keep = below & ~(bound < -UNDERFLOW_LOG2)
    keep = keep.reshape(batch * N_HEADS, nq * nq)
    n_entries = nq * (nq - 1) // 2 + ATT_UNROLL
    order = jnp.argsort(~keep, axis=-1, stable=True)[:, :n_entries].astype(jnp.int32)
    return order // nq, order % nq, jnp.sum(keep, axis=-1, dtype=jnp.int32)


def _ffn_kernel(x_ref, att_ref, sgu_ref, wo_ref, g2_ref, wup_ref, cw_ref, cb_ref, wdn_ref, fg_ref,
                o_ref, carry_ref, act_ref, *, tm, tiles_per_batch, final):
    fc = FFN_FC
    halo = SUBLANES

    @pl.when(pl.program_id(0) % tiles_per_batch == 0)
    def _():
        carry_ref[...] = jnp.zeros_like(carry_ref)

    x1_parts, h2_parts = [], []
    for r0 in range(0, tm, tm // 2):
        r1 = r0 + tm // 2
        part = (x_ref[r0:r1, :]
                + jnp.dot(att_ref[r0:r1, :], wo_ref[0:ATT_WIDTH, :], preferred_element_type=F32)
                + jnp.dot(sgu_ref[r0:r1, :], wo_ref[ATT_WIDTH:ATT_WIDTH + SGU_WIDTH, :],
                          preferred_element_type=F32))
        ms = jnp.mean(part * part, axis=-1, keepdims=True)
        x1_parts.append(part)
        h2_parts.append((part * lax.rsqrt(ms + NORM_EPS) * g2_ref[...]).astype(BF16))
    x1 = jnp.concatenate(x1_parts, axis=0)
    h2 = jnp.concatenate(h2_parts, axis=0)

    rows8 = lax.broadcasted_iota(jnp.int32, (halo, fc), 0)
    for c in range(D_FF // fc):
        convs = []
        for half in range(2):
            col0 = half * D_FF + c * fc
            up = jnp.dot(h2, wup_ref[:, col0:col0 + fc], preferred_element_type=F32)
            prev = carry_ref[half, c]
            carry_ref[half, c] = up[tm - halo:tm, :]
            scale = 0.5 if half == 1 else 1.0
            w0 = cw_ref[0:1, col0:col0 + fc] * scale
            w1 = cw_ref[1:2, col0:col0 + fc] * scale
            w2 = cw_ref[2:3, col0:col0 + fc] * scale
            bias = cb_ref[:, col0:col0 + fc] * scale

            def conv(cur, back1, back2):
                return bias + w0 * back2 + w1 * back1 + w2 * cur

            body = conv(up, pltpu.roll(up, 1, axis=0), pltpu.roll(up, 2, axis=0))
            head = up[0:halo, :]
            head1 = jnp.where(rows8 < 1, pltpu.roll(prev, 1, axis=0), pltpu.roll(head, 1, axis=0))
            head2 = jnp.where(rows8 < 2, pltpu.roll(prev, 2, axis=0), pltpu.roll(head, 2, axis=0))
            convs.append(jnp.concatenate([conv(head, head1, head2), body[halo:, :]], axis=0))
        act_ref[:, c * fc:(c + 1) * fc] = (_gelu_tanh_x2(convs[0]) * convs[1]).astype(BF16)

    out = x1 + jnp.dot(act_ref[...], wdn_ref[...], preferred_element_type=F32)
    if final:
        ms = jnp.mean(out * out, axis=-1, keepdims=True)
        out = out * lax.rsqrt(ms + NORM_EPS) * fg_ref[...]
    o_ref[...] = out


def _ffn(layer, x2d, att, sgu, w_out, g2, w_up, conv_w, conv_b, w_down, final_g, *, seq, final):
    rows = x2d.shape[0]
    tm = FFN_TM
    grid = (rows // tm,)
    row_spec = lambda width: pl.BlockSpec((tm, width), lambda i: (i, 0))
    whole = pl.BlockSpec(memory_space=pltpu.VMEM)
    return pl.pallas_call(
        functools.partial(_ffn_kernel, tm=tm, tiles_per_batch=seq // tm, final=final),
        grid=grid,
        in_specs=[row_spec(D_MODEL), row_spec(ATT_WIDTH), row_spec(SGU_WIDTH),
                  _layer_spec(layer, w_out.shape[1:]), whole, _layer_spec(layer, w_up.shape[1:]),
                  _layer_spec(layer, conv_w.shape[1:]), whole, _layer_spec(layer, w_down.shape[1:]),
                  whole],
        out_specs=row_spec(D_MODEL),
        out_shape=jax.ShapeDtypeStruct((rows, D_MODEL), F32),
        scratch_shapes=[
            pltpu.VMEM((2, D_FF // FFN_FC, SUBLANES, FFN_FC), F32),
            pltpu.VMEM((tm, D_FF), BF16),
        ],
        compiler_params=pltpu.CompilerParams(
            dimension_semantics=("arbitrary",), vmem_limit_bytes=VMEM_LIMIT_BYTES),
        name="ffn",
    )(x2d, att, sgu, w_out, g2, w_up, conv_w, conv_b, w_down, final_g)


def kernel(x, norm1_g, w_in, lam_q1, lam_k1, lam_q2, lam_k2, subln_g, sgu_ln_g, sgu_ln_b,
           sgu_w, sgu_b, w_out, norm2_g, ffn_w_up, ffn_conv_w, ffn_conv_b, ffn_w_down, final_g):
    batch, seq, _ = x.shape
    rows = batch * seq
    x2d = x.reshape(rows, D_MODEL)
    slopes = jnp.asarray([_alibi_slope(h) * LOG2_E for h in range(N_HEADS)], dtype=F32)
    row1 = lambda a: a.reshape(1, -1).astype(F32)
    w_in, w_out, w_up, w_down = (w.astype(BF16) for w in (w_in, w_out, ffn_w_up, ffn_w_down))
    sgu_w, conv_w = sgu_w.astype(F32), ffn_conv_w.astype(F32)
    for l in range(DEPTH):
        lam_init = _lambda_init(l)
        sgu_b_full = jnp.broadcast_to(sgu_b[l].astype(F32)[:, :, None],
                                      (N_GROUPS, SGU_CHUNK, GROUP_DIM))
        qat, qbt, ka, kb, vt, norms, sgu = _mix_in(l, x2d, row1(norm1_g[l]), w_in,
                                                   row1(sgu_ln_g[l]), row1(sgu_ln_b[l]),
                                                   sgu_w, sgu_b_full)
        pair_q, pair_k, pair_n = _off_diagonal_pairs(norms, slopes, batch, seq)
        att = _attention(pair_q, pair_k, pair_n, slopes,
                         row1(lam_q1[l]), row1(lam_k1[l]), row1(lam_q2[l]), row1(lam_k2[l]),
                         row1(subln_g[l]), qat, qbt, ka, kb, vt,
                         batch=batch, seq=seq, lam_init=lam_init)
        x2d = _ffn(l, x2d, att, sgu, w_out, row1(norm2_g[l]), w_up, conv_w, row1(ffn_conv_b[l]),
                   w_down, row1(final_g), seq=seq, final=(l == DEPTH - 1))
    return x2d.reshape(batch, seq, D_MODEL)
```

```python
import functools
import math

import jax
import jax.numpy as jnp
import numpy as np
from jax import lax
from jax.experimental import pallas as pl
from jax.experimental.pallas import tpu as pltpu

F32 = jnp.float32
BF16 = jnp.bfloat16

D_MODEL = 1024
DEPTH = 2
N_HEADS = 4
N_MAPS = 2
HEAD_DIM = 64
HEAD_SLAB = 2 * HEAD_DIM
ATT_WIDTH = N_HEADS * HEAD_SLAB
N_GROUPS = 4
SGU_CHUNK = 128
GROUP_DIM = 128
SGU_WIDTH = N_GROUPS * GROUP_DIM
D_FF = 2816
CONV_WIDTH = 3
NORM_EPS = 1e-6
SUBLN_EPS = 1e-5
LN_EPS = 1e-5
NEG_INF = -1e30
LOG2_E = math.log2(math.e)

SUBLANES = 8
BF16_EXACT_INT = 256
UNDERFLOW_LOG2 = 150.0
NORM_SAFETY = 1.001
VMEM_LIMIT_BYTES = 56 * 1024 * 1024

MIX_TM = 512
ATT_T = 512
ATT_UNROLL = 4
FFN_TM = 1024
FFN_FC = 256


def _gelu_tanh(x):
    c = math.sqrt(2.0 / math.pi)
    return 0.5 * x * (1.0 + jnp.tanh(c * (x + 0.044715 * (x * x * x))))


def _gelu_tanh_x2(x):
    c = math.sqrt(2.0 / math.pi)
    return x * (1.0 + jnp.tanh(x * (c + (c * 0.044715) * (x * x))))


def _alibi_slope(head):
    return 2.0 ** (-8.0 * (head + 1) / N_HEADS)


def _bf16_terms(value, n=3):
    terms = []
    rest = value
    for _ in range(n):
        term = float(np.asarray(rest, dtype=BF16).astype(np.float32))
        terms.append(term)
        rest -= term
    return terms


def _lambda_init(layer_idx):
    return 0.8 - 0.6 * math.exp(-0.3 * layer_idx)


def _layer_spec(layer, shape):
    zeros = (0,) * len(shape)
    return pl.BlockSpec((None,) + tuple(shape), lambda *_: (layer,) + zeros,
                        pipeline_mode=pl.Buffered(1))


def _alibi_operand_constants(t):
    pos = np.arange(t)
    n_terms = len(_bf16_terms(1.0))
    parts = [pos % BF16_EXACT_INT] * n_terms + [pos - pos % BF16_EXACT_INT] * n_terms
    parts = np.stack(parts, axis=1).astype(np.float32)
    first_lane = (HEAD_DIM, 0)
    k_aug = np.zeros((N_MAPS, t, HEAD_SLAB), np.float32)
    q_aug = np.zeros((N_HEADS * N_MAPS, HEAD_SLAB), np.float32)
    for mp, lane0 in enumerate(first_lane):
        k_aug[mp, :, lane0:lane0 + 2 * n_terms] = parts
        for h in range(N_HEADS):
            factor = _bf16_terms(_alibi_slope(h) * LOG2_E)
            q_aug[N_MAPS * h + mp, lane0:lane0 + 2 * n_terms] = factor + factor
    return jnp.asarray(k_aug), jnp.asarray(q_aug)


def _mixin_kernel(x_ref, g_ref, w_ref, lng_ref, lnb_ref, sw_ref, sb_ref, kaug_ref, qaug_ref,
                  qat_ref, qbt_ref, ka_ref, kb_ref, vt_ref, norm_ref, sgu_ref, *, tm, t):
    x = x_ref[...]
    ms = jnp.mean(x * x, axis=-1, keepdims=True)
    hb = (x * lax.rsqrt(ms + NORM_EPS) * g_ref[...]).astype(BF16)

    def proj(c0, width):
        return jnp.dot(hb, w_ref[:, c0:c0 + width], preferred_element_type=F32)

    zq = proj(0, ATT_WIDTH) * (HEAD_DIM ** -0.5 * LOG2_E)
    zk = proj(ATT_WIDTH, ATT_WIDTH)
    zv = proj(2 * ATT_WIDTH, ATT_WIDTH)
    in_a = lax.broadcasted_iota(jnp.int32, (tm, HEAD_SLAB), 1) < HEAD_DIM

    def max_sq_norm(slab):
        sq = slab.astype(BF16).astype(F32) ** 2
        return jnp.max(jnp.sum(sq, axis=-1, keepdims=True), axis=0, keepdims=True)

    stat_row = lax.broadcasted_iota(jnp.int32, (SUBLANES, HEAD_SLAB), 0)
    stat_lane = lax.broadcasted_iota(jnp.int32, (SUBLANES, HEAD_SLAB), 1)
    norms = jnp.zeros((SUBLANES, HEAD_SLAB), F32)
    for h in range(N_HEADS):
        lo, hi = h * HEAD_SLAB, (h + 1) * HEAD_SLAB
        qs, ks = zq[:, lo:hi], zk[:, lo:hi]
        norms = jnp.where((stat_row == 0) & (stat_lane == h), max_sq_norm(qs), norms)
        norms = jnp.where((stat_row == 1) & (stat_lane == h), max_sq_norm(ks), norms)
        ka_ref[:, lo:hi] = jnp.where(in_a, ks, kaug_ref[0]).astype(BF16)
        kb_ref[:, lo:hi] = jnp.where(in_a, kaug_ref[1], ks).astype(BF16)
        q_rows = qaug_ref[N_MAPS * h:N_MAPS * (h + 1), :]
        qat_ref[0, lo:hi, :] = jnp.where(in_a, qs, q_rows[0:1, :]).T.astype(BF16)
        qbt_ref[0, lo:hi, :] = jnp.where(in_a, q_rows[1:2, :], qs).T.astype(BF16)
        vt_ref[0, lo:hi, :] = zv[:, lo:hi].T.astype(BF16)
    norm_ref[0] = norms

    u = _gelu_tanh(proj(3 * ATT_WIDTH, SGU_WIDTH))
    vg = _gelu_tanh(proj(3 * ATT_WIDTH + SGU_WIDTH, SGU_WIDTH))
    row = lax.broadcasted_iota(jnp.int32, (SGU_CHUNK, SGU_CHUNK), 0)
    col = lax.broadcasted_iota(jnp.int32, (SGU_CHUNK, SGU_CHUNK), 1)
    tril = col <= row
    for g in range(N_GROUPS):
        lo, hi = g * GROUP_DIM, (g + 1) * GROUP_DIM
        y = vg[:, lo:hi]
        mu = jnp.mean(y, axis=-1, keepdims=True)
        yc = y - mu
        var = jnp.mean(yc * yc, axis=-1, keepdims=True)
        yn = (yc * lax.rsqrt(var + LN_EPS) * lng_ref[:, lo:hi] + lnb_ref[:, lo:hi]).astype(BF16)
        wm = jnp.where(tril, sw_ref[g], 0.0).astype(BF16)
        bias = sb_ref[g]
        n_chunks = tm // SGU_CHUNK
        chunks = jnp.concatenate([yn[c * SGU_CHUNK:(c + 1) * SGU_CHUNK, :]
                                  for c in range(n_chunks)], axis=1)
        vmix = jnp.dot(wm, chunks, preferred_element_type=F32)
        for c in range(n_chunks):
            r0, r1 = c * SGU_CHUNK, (c + 1) * SGU_CHUNK
            mixed = vmix[:, c * GROUP_DIM:(c + 1) * GROUP_DIM] + bias
            sgu_ref[r0:r1, lo:hi] = (u[r0:r1, lo:hi] * mixed).astype(BF16)


def _mix_in(layer, x2d, g, w_in, ln_g, ln_b, sgu_w, sgu_b_full):
    rows = x2d.shape[0]
    tm, t = MIX_TM, ATT_T
    assert tm == t
    k_aug, q_aug = _alibi_operand_constants(t)
    grid = (rows // tm,)
    row_spec = lambda width: pl.BlockSpec((tm, width), lambda i: (i, 0))
    tr_spec = pl.BlockSpec((tm // t, ATT_WIDTH, t), lambda i: (i, 0, 0))
    whole = pl.BlockSpec(memory_space=pltpu.VMEM)
    row_sds = jax.ShapeDtypeStruct((rows, ATT_WIDTH), BF16)
    tr_sds = jax.ShapeDtypeStruct((rows // t, ATT_WIDTH, t), BF16)
    norm_spec = pl.BlockSpec((1, SUBLANES, HEAD_SLAB), lambda i: (i, 0, 0))
    norm_sds = jax.ShapeDtypeStruct((rows // tm, SUBLANES, HEAD_SLAB), F32)
    return pl.pallas_call(
        functools.partial(_mixin_kernel, tm=tm, t=t),
        grid=grid,
        in_specs=[row_spec(D_MODEL), whole, _layer_spec(layer, w_in.shape[1:]), whole, whole,
                  _layer_spec(layer, sgu_w.shape[1:]), whole, whole, whole],
        out_specs=[tr_spec, tr_spec, row_spec(ATT_WIDTH), row_spec(ATT_WIDTH), tr_spec, norm_spec,
                   row_spec(SGU_WIDTH)],
        out_shape=[tr_sds, tr_sds, row_sds, row_sds, tr_sds, norm_sds, row_sds],
        compiler_params=pltpu.CompilerParams(
            dimension_semantics=("arbitrary",), vmem_limit_bytes=VMEM_LIMIT_BYTES),
        name="mix_in",
    )(x2d, g, w_in, ln_g, ln_b, sgu_w, sgu_b_full, k_aug, q_aug)


def _score_block(qi, kj, masked, q_refs, k_refs, s_ref, bm_ref, *, t):
    k0 = pl.multiple_of(kj * t, t)
    if masked:
        key = lax.broadcasted_iota(jnp.int32, (t, t), 0)
        qry = lax.broadcasted_iota(jnp.int32, (t, t), 1)
        causal = key <= qry
    for mp in range(N_MAPS):
        st = jnp.dot(k_refs[mp][pl.ds(k0, t), :], q_refs[mp][qi], preferred_element_type=F32)
        if masked:
            st = jnp.where(causal, st, NEG_INF)
        s_ref[mp] = st
        bm_ref[mp] = jnp.max(st, axis=0, keepdims=True)


def _softmax_block(qi, c, vt_blk, s_ref, bm_ref, m_ref, l_ref, acc_ref, first):
    for mp in range(N_MAPS):
        bm = bm_ref[mp] + c
        m_new = bm if first else jnp.maximum(m_ref[qi, mp], bm)
        p = jnp.exp2(s_ref[mp] - (m_new - c))
        ps = jnp.sum(p, axis=0, keepdims=True)
        pv = jnp.dot(vt_blk, p.astype(BF16), preferred_element_type=F32)
        if first:
            l_ref[qi, mp] = ps
            acc_ref[qi, mp] = pv
        else:
            alpha = jnp.exp2(m_ref[qi, mp] - m_new)
            l_ref[qi, mp] = alpha * l_ref[qi, mp] + ps
            acc_ref[qi, mp] = alpha * acc_ref[qi, mp] + pv
        m_ref[qi, mp] = m_new


def _off_diagonal_pairs(norm_ref, slope, pair_q_ref, pair_k_ref, *, t, nq):
    row = (pl.program_id(0) * N_HEADS + pl.program_id(1)) * 2

    def tile(i, count):
        q_norm, k_own = norm_ref[row, i], norm_ref[row + 1, i]

        def block(j, count):
            min_dist = ((i - j - 1) * t + 1).astype(F32)
            bound = q_norm * (norm_ref[row + 1, j] + k_own) - slope * min_dist
            pair_q_ref[count] = i
            pair_k_ref[count] = j
            return count + jnp.where(bound < -UNDERFLOW_LOG2, 0, 1)

        return lax.fori_loop(0, i, block, count)

    n_pairs = lax.fori_loop(0, nq, tile, 0)
    for pad in range(ATT_UNROLL):
        pair_q_ref[n_pairs + pad] = 0
        pair_k_ref[n_pairs + pad] = 0
    return n_pairs


def _attn_kernel(norm_ref, slopes_ref, lq1_ref, lk1_ref, lq2_ref, lk2_ref, sg_ref,
                 qat_ref, qbt_ref, ka_ref, kb_ref, vt_ref, o_ref,
                 m_ref, l_ref, acc_ref, sa_ref, bma_ref, sb_ref, bmb_ref, pair_q_ref, pair_k_ref,
                 *, t, nq, lam_init):
    slope = slopes_ref[pl.program_id(1)]
    n_pairs = _off_diagonal_pairs(norm_ref, slope, pair_q_ref, pair_k_ref, t=t, nq=nq)
    score = functools.partial(_score_block, q_refs=(qat_ref, qbt_ref), k_refs=(ka_ref, kb_ref), t=t)
    softmax = functools.partial(_softmax_block, m_ref=m_ref, l_ref=l_ref, acc_ref=acc_ref)
    buf_a = dict(s_ref=sa_ref, bm_ref=bma_ref)
    buf_b = dict(s_ref=sb_ref, bm_ref=bmb_ref)

    bufs = (buf_a, buf_b)

    score(0, 0, True, **buf_a)

    def diag_group(n, carry):
        for u in range(ATT_UNROLL):
            qi = ATT_UNROLL * n + u
            nxt = jnp.minimum(qi + 1, nq - 1)
            score(nxt, nxt, True, **bufs[(u + 1) % 2])
            softmax(qi, 0.0, vt_ref[qi], first=True, **bufs[u % 2])
        return carry

    lax.fori_loop(0, nq // ATT_UNROLL, diag_group, 0)

    def entry(e):
        qi, kj = pair_q_ref[e], pair_k_ref[e]
        c = jnp.where(e < n_pairs, ((kj - qi) * t).astype(F32) * slope, NEG_INF)
        return qi, kj, c

    def score_entry(e, buf):
        qi, kj, _ = entry(e)
        score(qi, kj, False, **buf)

    def softmax_entry(e, buf):
        qi, kj, c = entry(e)
        softmax(qi, c, vt_ref[kj], first=False, **buf)

    score_entry(0, buf_a)

    def off_group(n, carry):
        for u in range(ATT_UNROLL):
            e = ATT_UNROLL * n + u
            score_entry(e + 1, bufs[(u + 1) % 2])
            softmax_entry(e, bufs[u % 2])
        return carry

    lax.fori_loop(0, (n_pairs + ATT_UNROLL - 1) // ATT_UNROLL, off_group, 0)

    lam = (jnp.exp(jnp.sum(lq1_ref[...] * lk1_ref[...], axis=-1, keepdims=True))
           - jnp.exp(jnp.sum(lq2_ref[...] * lk2_ref[...], axis=-1, keepdims=True))
           + lam_init)

    def finish(qi, carry):
        o = (acc_ref[qi, 0] * (1.0 / l_ref[qi, 0])
             - lam * (acc_ref[qi, 1] * (1.0 / l_ref[qi, 1])))
        ms = jnp.mean(o * o, axis=0, keepdims=True)
        o = o * lax.rsqrt(ms + SUBLN_EPS) * (1.0 - lam_init)
        o_ref[pl.ds(pl.multiple_of(qi * t, t), t), :] = (o.T * sg_ref[...]).astype(BF16)
        return carry

    lax.fori_loop(0, nq, finish, 0)


def _attention(norms, slopes, lq1, lk1, lq2, lk2, subln_g, qat, qbt, ka, kb, vt,
               *, batch, seq, lam_init):
    t = ATT_T
    nq = seq // t
    assert MIX_TM == t and nq % ATT_UNROLL == 0 and ATT_UNROLL % 2 == 0
    norms = norms.reshape(batch, nq, SUBLANES, HEAD_SLAB)[:, :, 0:2, 0:N_HEADS]
    norms = (jnp.sqrt(norms) * NORM_SAFETY).transpose(0, 3, 2, 1).reshape(batch * N_HEADS * 2, nq)
    n_entries = nq * (nq - 1) // 2 + ATT_UNROLL
    pair_tab = pltpu.SMEM((n_entries,), jnp.int32)
    qt_spec = pl.BlockSpec((nq, HEAD_SLAB, t), lambda b, h: (b, h, 0))
    row_spec = pl.BlockSpec((seq, HEAD_SLAB), lambda b, h: (b, h))
    whole = pl.BlockSpec(memory_space=pltpu.VMEM)
    smem = pl.BlockSpec(memory_space=pltpu.SMEM)
    stat = pltpu.VMEM((nq, N_MAPS, 1, t), F32)
    acc = pltpu.VMEM((nq, N_MAPS, HEAD_SLAB, t), F32)
    score_buf = pltpu.VMEM((N_MAPS, t, t), F32)
    colmax_buf = pltpu.VMEM((N_MAPS, 1, t), F32)
    return pl.pallas_call(
        functools.partial(_attn_kernel, t=t, nq=nq, lam_init=lam_init),
        grid=(batch, N_HEADS),
        in_specs=[smem, smem, whole, whole, whole, whole, whole,
                  qt_spec, qt_spec, row_spec, row_spec, qt_spec],
        out_specs=row_spec,
        scratch_shapes=[stat, stat, acc, score_buf, colmax_buf, score_buf, colmax_buf,
                        pair_tab, pair_tab],
        out_shape=jax.ShapeDtypeStruct((batch * seq, ATT_WIDTH), BF16),
        compiler_params=pltpu.CompilerParams(
            dimension_semantics=("arbitrary", "arbitrary"),
            vmem_limit_bytes=VMEM_LIMIT_BYTES),
        name="diff_attn",
    )(norms, slopes, lq1, lk1, lq2, lk2, subln_g, qat, qbt, ka, kb, vt)


def _ffn_kernel(x_ref, att_ref, sgu_ref, wo_ref, g2_ref, wup_ref, cw_ref, cb_ref, wdn_ref, fg_ref,
                o_ref, carry_ref, act_ref, *, tm, tiles_per_batch, final):
    fc = FFN_FC
    halo = SUBLANES

    @pl.when(pl.program_id(0) % tiles_per_batch == 0)
    def _():
        carry_ref[...] = jnp.zeros_like(carry_ref)

    x1_parts, h2_parts = [], []
    for r0 in range(0, tm, tm // 2):
        r1 = r0 + tm // 2
        part = (x_ref[r0:r1, :]
                + jnp.dot(att_ref[r0:r1, :], wo_ref[0:ATT_WIDTH, :], preferred_element_type=F32)
                + jnp.dot(sgu_ref[r0:r1, :], wo_ref[ATT_WIDTH:ATT_WIDTH + SGU_WIDTH, :],
                          preferred_element_type=F32))
        ms = jnp.mean(part * part, axis=-1, keepdims=True)
        x1_parts.append(part)
        h2_parts.append((part * lax.rsqrt(ms + NORM_EPS) * g2_ref[...]).astype(BF16))
    x1 = jnp.concatenate(x1_parts, axis=0)
    h2 = jnp.concatenate(h2_parts, axis=0)

    rows8 = lax.broadcasted_iota(jnp.int32, (halo, fc), 0)
    for c in range(D_FF // fc):
        convs = []
        for half in range(2):
            col0 = half * D_FF + c * fc
            up = jnp.dot(h2, wup_ref[:, col0:col0 + fc], preferred_element_type=F32)
            prev = carry_ref[half, c]
            carry_ref[half, c] = up[tm - halo:tm, :]
            scale = 0.5 if half == 1 else 1.0
            w0 = cw_ref[0:1, col0:col0 + fc] * scale
            w1 = cw_ref[1:2, col0:col0 + fc] * scale
            w2 = cw_ref[2:3, col0:col0 + fc] * scale
            bias = cb_ref[:, col0:col0 + fc] * scale

            def conv(cur, back1, back2):
                return bias + w0 * back2 + w1 * back1 + w2 * cur

            body = conv(up, pltpu.roll(up, 1, axis=0), pltpu.roll(up, 2, axis=0))
            head = up[0:halo, :]
            head1 = jnp.where(rows8 < 1, pltpu.roll(prev, 1, axis=0), pltpu.roll(head, 1, axis=0))
            head2 = jnp.where(rows8 < 2, pltpu.roll(prev, 2, axis=0), pltpu.roll(head, 2, axis=0))
            convs.append(jnp.concatenate([conv(head, head1, head2), body[halo:, :]], axis=0))
        act_ref[:, c * fc:(c + 1) * fc] = (_gelu_tanh_x2(convs[0]) * convs[1]).astype(BF16)

    out = x1 + jnp.dot(act_ref[...], wdn_ref[...], preferred_element_type=F32)
    if final:
        ms = jnp.mean(out * out, axis=-1, keepdims=True)
        out = out * lax.rsqrt(ms + NORM_EPS) * fg_ref[...]
    o_ref[...] = out


def _ffn(layer, x2d, att, sgu, w_out, g2, w_up, conv_w, conv_b, w_down, final_g, *, seq, final):
    rows = x2d.shape[0]
    tm = FFN_TM
    grid = (rows // tm,)
    row_spec = lambda width: pl.BlockSpec((tm, width), lambda i: (i, 0))
    whole = pl.BlockSpec(memory_space=pltpu.VMEM)
    return pl.pallas_call(
        functools.partial(_ffn_kernel, tm=tm, tiles_per_batch=seq // tm, final=final),
        grid=grid,
        in_specs=[row_spec(D_MODEL), row_spec(ATT_WIDTH), row_spec(SGU_WIDTH),
                  _layer_spec(layer, w_out.shape[1:]), whole, _layer_spec(layer, w_up.shape[1:]),
                  _layer_spec(layer, conv_w.shape[1:]), whole, _layer_spec(layer, w_down.shape[1:]),
                  whole],
        out_specs=row_spec(D_MODEL),
        out_shape=jax.ShapeDtypeStruct((rows, D_MODEL), F32),
        scratch_shapes=[
            pltpu.VMEM((2, D_FF // FFN_FC, SUBLANES, FFN_FC), F32),
            pltpu.VMEM((tm, D_FF), BF16),
        ],
        compiler_params=pltpu.CompilerParams(
            dimension_semantics=("arbitrary",), vmem_limit_bytes=VMEM_LIMIT_BYTES),
        name="ffn",
    )(x2d, att, sgu, w_out, g2, w_up, conv_w, conv_b, w_down, final_g)


def kernel(x, norm1_g, w_in, lam_q1, lam_k1, lam_q2, lam_k2, subln_g, sgu_ln_g, sgu_ln_b,
           sgu_w, sgu_b, w_out, norm2_g, ffn_w_up, ffn_conv_w, ffn_conv_b, ffn_w_down, final_g):
    batch, seq, _ = x.shape
    rows = batch * seq
    x2d = x.reshape(rows, D_MODEL)
    slopes = jnp.asarray([_alibi_slope(h) * LOG2_E for h in range(N_HEADS)], dtype=F32)
    row1 = lambda a: a.reshape(1, -1).astype(F32)
    w_in, w_out, w_up, w_down = (w.astype(BF16) for w in (w_in, w_out, ffn_w_up, ffn_w_down))
    sgu_w, conv_w = sgu_w.astype(F32), ffn_conv_w.astype(F32)
    for l in range(DEPTH):
        lam_init = _lambda_init(l)
        sgu_b_full = jnp.broadcast_to(sgu_b[l].astype(F32)[:, :, None],
                                      (N_GROUPS, SGU_CHUNK, GROUP_DIM))
        qat, qbt, ka, kb, vt, norms, sgu = _mix_in(l, x2d, row1(norm1_g[l]), w_in,
                                                   row1(sgu_ln_g[l]), row1(sgu_ln_b[l]),
                                                   sgu_w, sgu_b_full)
        att = _attention(norms, slopes,
                         row1(lam_q1[l]), row1(lam_k1[l]), row1(lam_q2[l]), row1(lam_k2[l]),
                         row1(subln_g[l]), qat, qbt, ka, kb, vt,
                         batch=batch, seq=seq, lam_init=lam_init)
        x2d = _ffn(l, x2d, att, sgu, w_out, row1(norm2_g[l]), w_up, conv_w, row1(ffn_conv_b[l]),
                   w_down, row1(final_g), seq=seq, final=(l == DEPTH - 1))
    return x2d.reshape(batch, seq, D_MODEL)
```

```python
import functools
import math

import jax
import jax.numpy as jnp
import numpy as np
from jax import lax
from jax.experimental import pallas as pl
from jax.experimental.pallas import tpu as pltpu

F32 = jnp.float32
BF16 = jnp.bfloat16

D_MODEL = 1024
DEPTH = 2
N_HEADS = 4
N_MAPS = 2
HEAD_DIM = 64
HEAD_SLAB = 2 * HEAD_DIM
ATT_WIDTH = N_HEADS * HEAD_SLAB
N_GROUPS = 4
SGU_CHUNK = 128
GROUP_DIM = 128
SGU_WIDTH = N_GROUPS * GROUP_DIM
D_FF = 2816
CONV_WIDTH = 3
NORM_EPS = 1e-6
SUBLN_EPS = 1e-5
LN_EPS = 1e-5
NEG_INF = -1e30
LOG2_E = math.log2(math.e)

SUBLANES = 8
BF16_EXACT_INT = 256
UNDERFLOW_LOG2 = 150.0
NORM_SAFETY = 1.001
VMEM_LIMIT_BYTES = 56 * 1024 * 1024

MIX_TM = 1024
ATT_T = 512
ATT_UNROLL = 4
FFN_TM = 1024
FFN_FC = 256


def _gelu_tanh(x):
    c = math.sqrt(2.0 / math.pi)
    return 0.5 * x * (1.0 + jnp.tanh(c * (x + 0.044715 * (x * x * x))))


def _gelu_tanh_x2(x):
    c = math.sqrt(2.0 / math.pi)
    return x * (1.0 + jnp.tanh(x * (c + (c * 0.044715) * (x * x))))


def _alibi_slope(head):
    return 2.0 ** (-8.0 * (head + 1) / N_HEADS)


def _bf16_terms(value, n=3):
    terms = []
    rest = value
    for _ in range(n):
        term = float(np.asarray(rest, dtype=BF16).astype(np.float32))
        terms.append(term)
        rest -= term
    return terms


def _lambda_init(layer_idx):
    return 0.8 - 0.6 * math.exp(-0.3 * layer_idx)


def _layer_spec(layer, shape):
    zeros = (0,) * len(shape)
    return pl.BlockSpec((None,) + tuple(shape), lambda *_: (layer,) + zeros,
                        pipeline_mode=pl.Buffered(1))


def _alibi_operand_constants(t):
    pos = np.arange(t)
    n_terms = len(_bf16_terms(1.0))
    parts = [pos % BF16_EXACT_INT] * n_terms + [pos - pos % BF16_EXACT_INT] * n_terms
    parts = np.stack(parts, axis=1).astype(np.float32)
    first_lane = (HEAD_DIM, 0)
    k_aug = np.zeros((N_MAPS, t, HEAD_SLAB), np.float32)
    q_aug = np.zeros((N_HEADS * N_MAPS, HEAD_SLAB), np.float32)
    for mp, lane0 in enumerate(first_lane):
        k_aug[mp, :, lane0:lane0 + 2 * n_terms] = parts
        for h in range(N_HEADS):
            factor = _bf16_terms(_alibi_slope(h) * LOG2_E)
            q_aug[N_MAPS * h + mp, lane0:lane0 + 2 * n_terms] = factor + factor
    return jnp.asarray(k_aug), jnp.asarray(q_aug)


def _mixin_kernel(x_ref, g_ref, w_ref, lng_ref, lnb_ref, sw_ref, sb_ref, kaug_ref, qaug_ref,
                  qat_ref, qbt_ref, ka_ref, kb_ref, vt_ref, norm_ref, sgu_ref, *, tm, t):
    x = x_ref[...]
    ms = jnp.mean(x * x, axis=-1, keepdims=True)
    hb = (x * lax.rsqrt(ms + NORM_EPS) * g_ref[...]).astype(BF16)

    def proj(c0, width):
        return jnp.dot(hb, w_ref[:, c0:c0 + width], preferred_element_type=F32)

    zq = proj(0, ATT_WIDTH) * (HEAD_DIM ** -0.5 * LOG2_E)
    zk = proj(ATT_WIDTH, ATT_WIDTH)
    zv = proj(2 * ATT_WIDTH, ATT_WIDTH)
    in_a = lax.broadcasted_iota(jnp.int32, (tm, HEAD_SLAB), 1) < HEAD_DIM

    def max_sq_norm(slab):
        sq = slab.astype(BF16).astype(F32) ** 2
        return jnp.max(jnp.sum(sq, axis=-1, keepdims=True), axis=0, keepdims=True)

    stat_row = lax.broadcasted_iota(jnp.int32, (SUBLANES, HEAD_SLAB), 0)
    stat_lane = lax.broadcasted_iota(jnp.int32, (SUBLANES, HEAD_SLAB), 1)
    n_tiles = tm // t
    norms = [jnp.zeros((SUBLANES, HEAD_SLAB), F32)] * n_tiles
    k_aug = [jnp.concatenate([kaug_ref[mp]] * n_tiles, axis=0) for mp in range(N_MAPS)]
    for h in range(N_HEADS):
        lo, hi = h * HEAD_SLAB, (h + 1) * HEAD_SLAB
        qs, ks = zq[:, lo:hi], zk[:, lo:hi]
        ka_ref[:, lo:hi] = jnp.where(in_a, ks, k_aug[0]).astype(BF16)
        kb_ref[:, lo:hi] = jnp.where(in_a, k_aug[1], ks).astype(BF16)
        q_rows = qaug_ref[N_MAPS * h:N_MAPS * (h + 1), :]
        qat = jnp.where(in_a, qs, q_rows[0:1, :]).T.astype(BF16)
        qbt = jnp.where(in_a, q_rows[1:2, :], qs).T.astype(BF16)
        vt = zv[:, lo:hi].T.astype(BF16)
        for c in range(n_tiles):
            r0, r1 = c * t, (c + 1) * t
            norms[c] = jnp.where((stat_row == 0) & (stat_lane == h), max_sq_norm(qs[r0:r1]), norms[c])
            norms[c] = jnp.where((stat_row == 1) & (stat_lane == h), max_sq_norm(ks[r0:r1]), norms[c])
            qat_ref[c, lo:hi, :] = qat[:, r0:r1]
            qbt_ref[c, lo:hi, :] = qbt[:, r0:r1]
            vt_ref[c, lo:hi, :] = vt[:, r0:r1]
    for c in range(n_tiles):
        norm_ref[c] = norms[c]

    u = _gelu_tanh(proj(3 * ATT_WIDTH, SGU_WIDTH))
    vg = _gelu_tanh(proj(3 * ATT_WIDTH + SGU_WIDTH, SGU_WIDTH))
    row = lax.broadcasted_iota(jnp.int32, (SGU_CHUNK, SGU_CHUNK), 0)
    col = lax.broadcasted_iota(jnp.int32, (SGU_CHUNK, SGU_CHUNK), 1)
    tril = col <= row
    for g in range(N_GROUPS):
        lo, hi = g * GROUP_DIM, (g + 1) * GROUP_DIM
        y = vg[:, lo:hi]
        mu = jnp.mean(y, axis=-1, keepdims=True)
        yc = y - mu
        var = jnp.mean(yc * yc, axis=-1, keepdims=True)
        yn = (yc * lax.rsqrt(var + LN_EPS) * lng_ref[:, lo:hi] + lnb_ref[:, lo:hi]).astype(BF16)
        wm = jnp.where(tril, sw_ref[g], 0.0).astype(BF16)
        bias = sb_ref[g]
        n_chunks = tm // SGU_CHUNK
        chunks = jnp.concatenate([yn[c * SGU_CHUNK:(c + 1) * SGU_CHUNK, :]
                                  for c in range(n_chunks)], axis=1)
        vmix = jnp.dot(wm, chunks, preferred_element_type=F32)
        for c in range(n_chunks):
            r0, r1 = c * SGU_CHUNK, (c + 1) * SGU_CHUNK
            mixed = vmix[:, c * GROUP_DIM:(c + 1) * GROUP_DIM] + bias
            sgu_ref[r0:r1, lo:hi] = (u[r0:r1, lo:hi] * mixed).astype(BF16)


def _mix_in(layer, x2d, g, w_in, ln_g, ln_b, sgu_w, sgu_b_full):
    rows = x2d.shape[0]
    tm, t = MIX_TM, ATT_T
    assert tm % t == 0
    k_aug, q_aug = _alibi_operand_constants(t)
    grid = (rows // tm,)
    row_spec = lambda width: pl.BlockSpec((tm, width), lambda i: (i, 0))
    tr_spec = pl.BlockSpec((tm // t, ATT_WIDTH, t), lambda i: (i, 0, 0))
    whole = pl.BlockSpec(memory_space=pltpu.VMEM)
    row_sds = jax.ShapeDtypeStruct((rows, ATT_WIDTH), BF16)
    tr_sds = jax.ShapeDtypeStruct((rows // t, ATT_WIDTH, t), BF16)
    norm_spec = pl.BlockSpec((tm // t, SUBLANES, HEAD_SLAB), lambda i: (i, 0, 0))
    norm_sds = jax.ShapeDtypeStruct((rows // t, SUBLANES, HEAD_SLAB), F32)
    return pl.pallas_call(
        functools.partial(_mixin_kernel, tm=tm, t=t),
        grid=grid,
        in_specs=[row_spec(D_MODEL), whole, _layer_spec(layer, w_in.shape[1:]), whole, whole,
                  _layer_spec(layer, sgu_w.shape[1:]), whole, whole, whole],
        out_specs=[tr_spec, tr_spec, row_spec(ATT_WIDTH), row_spec(ATT_WIDTH), tr_spec, norm_spec,
                   row_spec(SGU_WIDTH)],
        out_shape=[tr_sds, tr_sds, row_sds, row_sds, tr_sds, norm_sds, row_sds],
        compiler_params=pltpu.CompilerParams(
            dimension_semantics=("arbitrary",), vmem_limit_bytes=VMEM_LIMIT_BYTES),
        name="mix_in",
    )(x2d, g, w_in, ln_g, ln_b, sgu_w, sgu_b_full, k_aug, q_aug)


def _score_block(qi, kj, masked, q_refs, k_refs, s_ref, bm_ref, *, t):
    k0 = pl.multiple_of(kj * t, t)
    if masked:
        key = lax.broadcasted_iota(jnp.int32, (t, t), 0)
        qry = lax.broadcasted_iota(jnp.int32, (t, t), 1)
        causal = key <= qry
    for mp in range(N_MAPS):
        st = jnp.dot(k_refs[mp][pl.ds(k0, t), :], q_refs[mp][qi], preferred_element_type=F32)
        if masked:
            st = jnp.where(causal, st, NEG_INF)
        s_ref[mp] = st
        bm_ref[mp] = jnp.max(st, axis=0, keepdims=True)


def _softmax_block(qi, c, vt_blk, s_ref, bm_ref, m_ref, l_ref, acc_ref, first):
    for mp in range(N_MAPS):
        bm = bm_ref[mp] + c
        m_new = bm if first else jnp.maximum(m_ref[qi, mp], bm)
        p = jnp.exp2(s_ref[mp] - (m_new - c))
        ps = jnp.sum(p, axis=0, keepdims=True)
        pv = jnp.dot(vt_blk, p.astype(BF16), preferred_element_type=F32)
        if first:
            l_ref[qi, mp] = ps
            acc_ref[qi, mp] = pv
        else:
            alpha = jnp.exp2(m_ref[qi, mp] - m_new)
            l_ref[qi, mp] = alpha * l_ref[qi, mp] + ps
            acc_ref[qi, mp] = alpha * acc_ref[qi, mp] + pv
        m_ref[qi, mp] = m_new


def _attn_kernel(pair_q_ref, pair_k_ref, pair_n_ref,
                 slopes_ref, lq1_ref, lk1_ref, lq2_ref, lk2_ref, sg_ref,
                 qat_ref, qbt_ref, ka_ref, kb_ref, vt_ref, o_ref,
                 m_ref, l_ref, acc_ref, sa_ref, bma_ref, sb_ref, bmb_ref, *, t, nq, lam_init):
    bh = pl.program_id(0) * N_HEADS + pl.program_id(1)
    slope = slopes_ref[pl.program_id(1)]
    n_pairs = pair_n_ref[bh]
    score = functools.partial(_score_block, q_refs=(qat_ref, qbt_ref), k_refs=(ka_ref, kb_ref), t=t)
    softmax = functools.partial(_softmax_block, m_ref=m_ref, l_ref=l_ref, acc_ref=acc_ref)
    buf_a = dict(s_ref=sa_ref, bm_ref=bma_ref)
    buf_b = dict(s_ref=sb_ref, bm_ref=bmb_ref)

    bufs = (buf_a, buf_b)

    score(0, 0, True, **buf_a)

    def diag_group(n, carry):
        for u in range(ATT_UNROLL):
            qi = ATT_UNROLL * n + u
            nxt = jnp.minimum(qi + 1, nq - 1)
            score(nxt, nxt, True, **bufs[(u + 1) % 2])
            softmax(qi, 0.0, vt_ref[qi], first=True, **bufs[u % 2])
        return carry

    lax.fori_loop(0, nq // ATT_UNROLL, diag_group, 0)

    def entry(e):
        qi, kj = pair_q_ref[bh, e], pair_k_ref[bh, e]
        c = jnp.where(e < n_pairs, ((kj - qi) * t).astype(F32) * slope, NEG_INF)
        return qi, kj, c

    def score_entry(e, buf):
        qi, kj, _ = entry(e)
        score(qi, kj, False, **buf)

    def softmax_entry(e, buf):
        qi, kj, c = entry(e)
        softmax(qi, c, vt_ref[kj], first=False, **buf)

    score_entry(0, buf_a)

    def off_group(n, carry):
        for u in range(ATT_UNROLL):
            e = ATT_UNROLL * n + u
            score_entry(e + 1, bufs[(u + 1) % 2])
            softmax_entry(e, bufs[u % 2])
        return carry

    lax.fori_loop(0, (n_pairs + ATT_UNROLL - 1) // ATT_UNROLL, off_group, 0)

    lam = (jnp.exp(jnp.sum(lq1_ref[...] * lk1_ref[...], axis=-1, keepdims=True))
           - jnp.exp(jnp.sum(lq2_ref[...] * lk2_ref[...], axis=-1, keepdims=True))
           + lam_init)

    def finish(qi, carry):
        o = (acc_ref[qi, 0] * (1.0 / l_ref[qi, 0])
             - lam * (acc_ref[qi, 1] * (1.0 / l_ref[qi, 1])))
        ms = jnp.mean(o * o, axis=0, keepdims=True)
        o = o * lax.rsqrt(ms + SUBLN_EPS) * (1.0 - lam_init)
        o_ref[pl.ds(pl.multiple_of(qi * t, t), t), :] = (o.T * sg_ref[...]).astype(BF16)
        return carry

    lax.fori_loop(0, nq, finish, 0)


def _attention(pair_q, pair_k, pair_n, slopes, lq1, lk1, lq2, lk2, subln_g, qat, qbt, ka, kb, vt,
               *, batch, seq, lam_init):
    t = ATT_T
    nq = seq // t
    qt_spec = pl.BlockSpec((nq, HEAD_SLAB, t), lambda b, h, *_: (b, h, 0))
    row_spec = pl.BlockSpec((seq, HEAD_SLAB), lambda b, h, *_: (b, h))
    whole = pl.BlockSpec(memory_space=pltpu.VMEM)
    smem = pl.BlockSpec(memory_space=pltpu.SMEM)
    stat = pltpu.VMEM((nq, N_MAPS, 1, t), F32)
    acc = pltpu.VMEM((nq, N_MAPS, HEAD_SLAB, t), F32)
    score_buf = pltpu.VMEM((N_MAPS, t, t), F32)
    colmax_buf = pltpu.VMEM((N_MAPS, 1, t), F32)
    return pl.pallas_call(
        functools.partial(_attn_kernel, t=t, nq=nq, lam_init=lam_init),
        grid_spec=pltpu.PrefetchScalarGridSpec(
            num_scalar_prefetch=3,
            grid=(batch, N_HEADS),
            in_specs=[smem, whole, whole, whole, whole, whole,
                      qt_spec, qt_spec, row_spec, row_spec, qt_spec],
            out_specs=row_spec,
            scratch_shapes=[stat, stat, acc, score_buf, colmax_buf, score_buf, colmax_buf]),
        out_shape=jax.ShapeDtypeStruct((batch * seq, ATT_WIDTH), BF16),
        compiler_params=pltpu.CompilerParams(
            dimension_semantics=("arbitrary", "arbitrary"),
            vmem_limit_bytes=VMEM_LIMIT_BYTES),
        name="diff_attn",
    )(pair_q, pair_k, pair_n, slopes, lq1, lk1, lq2, lk2, subln_g, qat, qbt, ka, kb, vt)


def _off_diagonal_pairs(norms, slopes, batch, seq):
    nq = seq // ATT_T
    tile = np.arange(nq)
    below = tile[:, None] > tile[None, :]
    min_dist = ((tile[:, None] - tile[None, :] - 1) * ATT_T + 1).astype(np.float32)
    norms = norms.reshape(batch, nq, SUBLANES, HEAD_SLAB)
    qn = (jnp.sqrt(norms[:, :, 0, :N_HEADS]) * NORM_SAFETY).transpose(0, 2, 1)
    kn = (jnp.sqrt(norms[:, :, 1, :N_HEADS]) * NORM_SAFETY).transpose(0, 2, 1)
    bound = (qn[..., :, None] * (kn[..., None, :] + kn[..., :, None])
             - slopes[None, :, None, None] * min_dist)
    keep = below & ~(bound < -UNDERFLOW_LOG2)
    keep = keep.reshape(batch * N_HEADS, nq * nq)
    n_entries = nq * (nq - 1) // 2 + ATT_UNROLL
    order = jnp.argsort(~keep, axis=-1, stable=True)[:, :n_entries].astype(jnp.int32)
    return order // nq, order % nq, jnp.sum(keep, axis=-1, dtype=jnp.int32)


def _ffn_kernel(x_ref, att_ref, sgu_ref, wo_ref, g2_ref, wup_ref, cw_ref, cb_ref, wdn_ref, fg_ref,
                o_ref, carry_ref, act_ref, *, tm, tiles_per_batch, final):
    fc = FFN_FC
    halo = SUBLANES

    @pl.when(pl.program_id(0) % tiles_per_batch == 0)
    def _():
        carry_ref[...] = jnp.zeros_like(carry_ref)

    x1_parts, h2_parts = [], []
    for r0 in range(0, tm, tm // 2):
        r1 = r0 + tm // 2
        part = (x_ref[r0:r1, :]
                + jnp.dot(att_ref[r0:r1, :], wo_ref[0:ATT_WIDTH, :], preferred_element_type=F32)
                + jnp.dot(sgu_ref[r0:r1, :], wo_ref[ATT_WIDTH:ATT_WIDTH + SGU_WIDTH, :],
                          preferred_element_type=F32))
        ms = jnp.mean(part * part, axis=-1, keepdims=True)
        x1_parts.append(part)
        h2_parts.append((part * lax.rsqrt(ms + NORM_EPS) * g2_ref[...]).astype(BF16))
    x1 = jnp.concatenate(x1_parts, axis=0)
    h2 = jnp.concatenate(h2_parts, axis=0)

    rows8 = lax.broadcasted_iota(jnp.int32, (halo, fc), 0)
    for c in range(D_FF // fc):
        convs = []
        for half in range(2):
            col0 = half * D_FF + c * fc
            up = jnp.dot(h2, wup_ref[:, col0:col0 + fc], preferred_element_type=F32)
            prev = carry_ref[half, c]
            carry_ref[half, c] = up[tm - halo:tm, :]
            scale = 0.5 if half == 1 else 1.0
            w0 = cw_ref[0:1, col0:col0 + fc] * scale
            w1 = cw_ref[1:2, col0:col0 + fc] * scale
            w2 = cw_ref[2:3, col0:col0 + fc] * scale
            bias = cb_ref[:, col0:col0 + fc] * scale

            def conv(cur, back1, back2):
                return bias + w0 * back2 + w1 * back1 + w2 * cur

            body = conv(up, pltpu.roll(up, 1, axis=0), pltpu.roll(up, 2, axis=0))
            head = up[0:halo, :]
            head1 = jnp.where(rows8 < 1, pltpu.roll(prev, 1, axis=0), pltpu.roll(head, 1, axis=0))
            head2 = jnp.where(rows8 < 2, pltpu.roll(prev, 2, axis=0), pltpu.roll(head, 2, axis=0))
            convs.append(jnp.concatenate([conv(head, head1, head2), body[halo:, :]], axis=0))
        act_ref[:, c * fc:(c + 1) * fc] = (_gelu_tanh_x2(convs[0]) * convs[1]).astype(BF16)

    out = x1 + jnp.dot(act_ref[...], wdn_ref[...], preferred_element_type=F32)
    if final:
        ms = jnp.mean(out * out, axis=-1, keepdims=True)
        out = out * lax.rsqrt(ms + NORM_EPS) * fg_ref[...]
    o_ref[...] = out


def _ffn(layer, x2d, att, sgu, w_out, g2, w_up, conv_w, conv_b, w_down, final_g, *, seq, final):
    rows = x2d.shape[0]
    tm = FFN_TM
    grid = (rows // tm,)
    row_spec = lambda width: pl.BlockSpec((tm, width), lambda i: (i, 0))
    whole = pl.BlockSpec(memory_space=pltpu.VMEM)
    return pl.pallas_call(
        functools.partial(_ffn_kernel, tm=tm, tiles_per_batch=seq // tm, final=final),
        grid=grid,
        in_specs=[row_spec(D_MODEL), row_spec(ATT_WIDTH), row_spec(SGU_WIDTH),
                  _layer_spec(layer, w_out.shape[1:]), whole, _layer_spec(layer, w_up.shape[1:]),
                  _layer_spec(layer, conv_w.shape[1:]), whole, _layer_spec(layer, w_down.shape[1:]),
                  whole],
        out_specs=row_spec(D_MODEL),
        out_shape=jax.ShapeDtypeStruct((rows, D_MODEL), F32),
        scratch_shapes=[
            pltpu.VMEM((2, D_FF // FFN_FC, SUBLANES, FFN_FC), F32),
            pltpu.VMEM((tm, D_FF), BF16),
        ],
        compiler_params=pltpu.CompilerParams(
            dimension_semantics=("arbitrary",), vmem_limit_bytes=VMEM_LIMIT_BYTES),
        name="ffn",
    )(x2d, att, sgu, w_out, g2, w_up, conv_w, conv_b, w_down, final_g)


def kernel(x, norm1_g, w_in, lam_q1, lam_k1, lam_q2, lam_k2, subln_g, sgu_ln_g, sgu_ln_b,
           sgu_w, sgu_b, w_out, norm2_g, ffn_w_up, ffn_conv_w, ffn_conv_b, ffn_w_down, final_g):
    batch, seq, _ = x.shape
    rows = batch * seq
    x2d = x.reshape(rows, D_MODEL)
    slopes = jnp.asarray([_alibi_slope(h) * LOG2_E for h in range(N_HEADS)], dtype=F32)
    row1 = lambda a: a.reshape(1, -1).astype(F32)
    w_in, w_out, w_up, w_down = (w.astype(BF16) for w in (w_in, w_out, ffn_w_up, ffn_w_down))
    sgu_w, conv_w = sgu_w.astype(F32), ffn_conv_w.astype(F32)
    for l in range(DEPTH):
        lam_init = _lambda_init(l)
        sgu_b_full = jnp.broadcast_to(sgu_b[l].astype(F32)[:, :, None],
                                      (N_GROUPS, SGU_CHUNK, GROUP_DIM))
        qat, qbt, ka, kb, vt, norms, sgu = _mix_in(l, x2d, row1(norm1_g[l]), w_in,
                                                   row1(sgu_ln_g[l]), row1(sgu_ln_b[l]),
                                                   sgu_w, sgu_b_full)
        pair_q, pair_k, pair_n = _off_diagonal_pairs(norms, slopes, batch, seq)
        att = _attention(pair_q, pair_k, pair_n, slopes,
                         row1(lam_q1[l]), row1(lam_k1[l]), row1(lam_q2[l]), row1(lam_k2[l]),
                         row1(subln_g[l]), qat, qbt, ka, kb, vt,
                         batch=batch, seq=seq, lam_init=lam_init)
        x2d = _ffn(l, x2d, att, sgu, w_out, row1(norm2_g[l]), w_up, conv_w, row1(ffn_conv_b[l]),
                   w_down, row1(final_g), seq=seq, final=(l == DEPTH - 1))
    return x2d.reshape(batch, seq, D_MODEL)
```

```python
import functools
import math

import jax
import jax.numpy as jnp
import numpy as np
from jax import lax
from jax.experimental import pallas as pl
from jax.experimental.pallas import tpu as pltpu

F32 = jnp.float32
BF16 = jnp.bfloat16

D_MODEL = 1024
DEPTH = 2
N_HEADS = 4
N_MAPS = 2
HEAD_DIM = 64
HEAD_SLAB = 2 * HEAD_DIM
ATT_WIDTH = N_HEADS * HEAD_SLAB
N_GROUPS = 4
SGU_CHUNK = 128
GROUP_DIM = 128
SGU_WIDTH = N_GROUPS * GROUP_DIM
D_FF = 2816
CONV_WIDTH = 3
NORM_EPS = 1e-6
SUBLN_EPS = 1e-5
LN_EPS = 1e-5
NEG_INF = -1e30
LOG2_E = math.log2(math.e)

SUBLANES = 8
BF16_EXACT_INT = 256
UNDERFLOW_LOG2 = math.inf
NORM_SAFETY = 1.001
VMEM_LIMIT_BYTES = 56 * 1024 * 1024

MIX_TM = 1024
ATT_T = 512
ATT_UNROLL = 4
FFN_TM = 1024
FFN_FC = 256


def _gelu_tanh(x):
    c = math.sqrt(2.0 / math.pi)
    return 0.5 * x * (1.0 + jnp.tanh(c * (x + 0.044715 * (x * x * x))))


def _gelu_tanh_x2(x):
    c = math.sqrt(2.0 / math.pi)
    return x * (1.0 + jnp.tanh(x * (c + (c * 0.044715) * (x * x))))


def _alibi_slope(head):
    return 2.0 ** (-8.0 * (head + 1) / N_HEADS)


def _bf16_terms(value, n=3):
    terms = []
    rest = value
    for _ in range(n):
        term = float(np.asarray(rest, dtype=BF16).astype(np.float32))
        terms.append(term)
        rest -= term
    return terms


def _lambda_init(layer_idx):
    return 0.8 - 0.6 * math.exp(-0.3 * layer_idx)


def _layer_spec(layer, shape):
    zeros = (0,) * len(shape)
    return pl.BlockSpec((None,) + tuple(shape), lambda *_: (layer,) + zeros,
                        pipeline_mode=pl.Buffered(1))


def _alibi_operand_constants(t):
    pos = np.arange(t)
    n_terms = len(_bf16_terms(1.0))
    parts = [pos % BF16_EXACT_INT] * n_terms + [pos - pos % BF16_EXACT_INT] * n_terms
    parts = np.stack(parts, axis=1).astype(np.float32)
    first_lane = (HEAD_DIM, 0)
    k_aug = np.zeros((N_MAPS, t, HEAD_SLAB), np.float32)
    q_aug = np.zeros((N_HEADS * N_MAPS, HEAD_SLAB), np.float32)
    for mp, lane0 in enumerate(first_lane):
        k_aug[mp, :, lane0:lane0 + 2 * n_terms] = parts
        for h in range(N_HEADS):
            factor = _bf16_terms(_alibi_slope(h) * LOG2_E)
            q_aug[N_MAPS * h + mp, lane0:lane0 + 2 * n_terms] = factor + factor
    return jnp.asarray(k_aug), jnp.asarray(q_aug)


def _mixin_kernel(x_ref, g_ref, w_ref, lng_ref, lnb_ref, sw_ref, sb_ref, kaug_ref, qaug_ref,
                  qat_ref, qbt_ref, ka_ref, kb_ref, vt_ref, norm_ref, sgu_ref, *, tm, t):
    x = x_ref[...]
    ms = jnp.mean(x * x, axis=-1, keepdims=True)
    hb = (x * lax.rsqrt(ms + NORM_EPS) * g_ref[...]).astype(BF16)

    def proj(c0, width):
        return jnp.dot(hb, w_ref[:, c0:c0 + width], preferred_element_type=F32)

    zq = proj(0, ATT_WIDTH) * (HEAD_DIM ** -0.5 * LOG2_E)
    zk = proj(ATT_WIDTH, ATT_WIDTH)
    zv = proj(2 * ATT_WIDTH, ATT_WIDTH)
    in_a = lax.broadcasted_iota(jnp.int32, (tm, HEAD_SLAB), 1) < HEAD_DIM

    def max_sq_norm(slab):
        sq = slab.astype(BF16).astype(F32) ** 2
        return jnp.max(jnp.sum(sq, axis=-1, keepdims=True), axis=0, keepdims=True)

    stat_row = lax.broadcasted_iota(jnp.int32, (SUBLANES, HEAD_SLAB), 0)
    stat_lane = lax.broadcasted_iota(jnp.int32, (SUBLANES, HEAD_SLAB), 1)
    n_tiles = tm // t
    norms = [jnp.zeros((SUBLANES, HEAD_SLAB), F32)] * n_tiles
    k_aug = [jnp.concatenate([kaug_ref[mp]] * n_tiles, axis=0) for mp in range(N_MAPS)]
    for h in range(N_HEADS):
        lo, hi = h * HEAD_SLAB, (h + 1) * HEAD_SLAB
        qs, ks = zq[:, lo:hi], zk[:, lo:hi]
        ka_ref[:, lo:hi] = jnp.where(in_a, ks, k_aug[0]).astype(BF16)
        kb_ref[:, lo:hi] = jnp.where(in_a, k_aug[1], ks).astype(BF16)
        q_rows = qaug_ref[N_MAPS * h:N_MAPS * (h + 1), :]
        qat = jnp.where(in_a, qs, q_rows[0:1, :]).T.astype(BF16)
        qbt = jnp.where(in_a, q_rows[1:2, :], qs).T.astype(BF16)
        vt = zv[:, lo:hi].T.astype(BF16)
        for c in range(n_tiles):
            r0, r1 = c * t, (c + 1) * t
            norms[c] = jnp.where((stat_row == 0) & (stat_lane == h), max_sq_norm(qs[r0:r1]), norms[c])
            norms[c] = jnp.where((stat_row == 1) & (stat_lane == h), max_sq_norm(ks[r0:r1]), norms[c])
            qat_ref[c, lo:hi, :] = qat[:, r0:r1]
            qbt_ref[c, lo:hi, :] = qbt[:, r0:r1]
            vt_ref[c, lo:hi, :] = vt[:, r0:r1]
    for c in range(n_tiles):
        norm_ref[c] = norms[c]

    u = _gelu_tanh(proj(3 * ATT_WIDTH, SGU_WIDTH))
    vg = _gelu_tanh(proj(3 * ATT_WIDTH + SGU_WIDTH, SGU_WIDTH))
    row = lax.broadcasted_iota(jnp.int32, (SGU_CHUNK, SGU_CHUNK), 0)
    col = lax.broadcasted_iota(jnp.int32, (SGU_CHUNK, SGU_CHUNK), 1)
    tril = col <= row
    for g in range(N_GROUPS):
        lo, hi = g * GROUP_DIM, (g + 1) * GROUP_DIM
        y = vg[:, lo:hi]
        mu = jnp.mean(y, axis=-1, keepdims=True)
        yc = y - mu
        var = jnp.mean(yc * yc, axis=-1, keepdims=True)
        yn = (yc * lax.rsqrt(var + LN_EPS) * lng_ref[:, lo:hi] + lnb_ref[:, lo:hi]).astype(BF16)
        wm = jnp.where(tril, sw_ref[g], 0.0).astype(BF16)
        bias = sb_ref[g]
        n_chunks = tm // SGU_CHUNK
        chunks = jnp.concatenate([yn[c * SGU_CHUNK:(c + 1) * SGU_CHUNK, :]
                                  for c in range(n_chunks)], axis=1)
        vmix = jnp.dot(wm, chunks, preferred_element_type=F32)
        for c in range(n_chunks):
            r0, r1 = c * SGU_CHUNK, (c + 1) * SGU_CHUNK
            mixed = vmix[:, c * GROUP_DIM:(c + 1) * GROUP_DIM] + bias
            sgu_ref[r0:r1, lo:hi] = (u[r0:r1, lo:hi] * mixed).astype(BF16)


def _mix_in(layer, x2d, g, w_in, ln_g, ln_b, sgu_w, sgu_b_full):
    rows = x2d.shape[0]
    tm, t = MIX_TM, ATT_T
    assert tm % t == 0
    k_aug, q_aug = _alibi_operand_constants(t)
    grid = (rows // tm,)
    row_spec = lambda width: pl.BlockSpec((tm, width), lambda i: (i, 0))
    tr_spec = pl.BlockSpec((tm // t, ATT_WIDTH, t), lambda i: (i, 0, 0))
    whole = pl.BlockSpec(memory_space=pltpu.VMEM)
    row_sds = jax.ShapeDtypeStruct((rows, ATT_WIDTH), BF16)
    tr_sds = jax.ShapeDtypeStruct((rows // t, ATT_WIDTH, t), BF16)
    norm_spec = pl.BlockSpec((tm // t, SUBLANES, HEAD_SLAB), lambda i: (i, 0, 0))
    norm_sds = jax.ShapeDtypeStruct((rows // t, SUBLANES, HEAD_SLAB), F32)
    return pl.pallas_call(
        functools.partial(_mixin_kernel, tm=tm, t=t),
        grid=grid,
        in_specs=[row_spec(D_MODEL), whole, _layer_spec(layer, w_in.shape[1:]), whole, whole,
                  _layer_spec(layer, sgu_w.shape[1:]), whole, whole, whole],
        out_specs=[tr_spec, tr_spec, row_spec(ATT_WIDTH), row_spec(ATT_WIDTH), tr_spec, norm_spec,
                   row_spec(SGU_WIDTH)],
        out_shape=[tr_sds, tr_sds, row_sds, row_sds, tr_sds, norm_sds, row_sds],
        compiler_params=pltpu.CompilerParams(
            dimension_semantics=("arbitrary",), vmem_limit_bytes=VMEM_LIMIT_BYTES),
        name="mix_in",
    )(x2d, g, w_in, ln_g, ln_b, sgu_w, sgu_b_full, k_aug, q_aug)


def _score_block(qi, kj, masked, q_refs, k_refs, s_ref, bm_ref, *, t):
    k0 = pl.multiple_of(kj * t, t)
    if masked:
        key = lax.broadcasted_iota(jnp.int32, (t, t), 0)
        qry = lax.broadcasted_iota(jnp.int32, (t, t), 1)
        causal = key <= qry
    for mp in range(N_MAPS):
        st = jnp.dot(k_refs[mp][pl.ds(k0, t), :], q_refs[mp][qi], preferred_element_type=F32)
        if masked:
            st = jnp.where(causal, st, NEG_INF)
        s_ref[mp] = st
        bm_ref[mp] = jnp.max(st, axis=0, keepdims=True)


def _softmax_block(qi, c, vt_blk, s_ref, bm_ref, m_ref, l_ref, acc_ref, first):
    for mp in range(N_MAPS):
        bm = bm_ref[mp] + c
        m_new = bm if first else jnp.maximum(m_ref[qi, mp], bm)
        p = jnp.exp2(s_ref[mp] - (m_new - c))
        ps = jnp.sum(p, axis=0, keepdims=True)
        pv = jnp.dot(vt_blk, p.astype(BF16), preferred_element_type=F32)
        if first:
            l_ref[qi, mp] = ps
            acc_ref[qi, mp] = pv
        else:
            alpha = jnp.exp2(m_ref[qi, mp] - m_new)
            l_ref[qi, mp] = alpha * l_ref[qi, mp] + ps
            acc_ref[qi, mp] = alpha * acc_ref[qi, mp] + pv
        m_ref[qi, mp] = m_new


def _attn_kernel(pair_q_ref, pair_k_ref, pair_n_ref,
                 slopes_ref, lq1_ref, lk1_ref, lq2_ref, lk2_ref, sg_ref,
                 qat_ref, qbt_ref, ka_ref, kb_ref, vt_ref, o_ref,
                 m_ref, l_ref, acc_ref, sa_ref, bma_ref, sb_ref, bmb_ref, *, t, nq, lam_init):
    bh = pl.program_id(0) * N_HEADS + pl.program_id(1)
    slope = slopes_ref[pl.program_id(1)]
    n_pairs = pair_n_ref[bh]
    score = functools.partial(_score_block, q_refs=(qat_ref, qbt_ref), k_refs=(ka_ref, kb_ref), t=t)
    softmax = functools.partial(_softmax_block, m_ref=m_ref, l_ref=l_ref, acc_ref=acc_ref)
    buf_a = dict(s_ref=sa_ref, bm_ref=bma_ref)
    buf_b = dict(s_ref=sb_ref, bm_ref=bmb_ref)

    bufs = (buf_a, buf_b)

    score(0, 0, True, **buf_a)

    def diag_group(n, carry):
        for u in range(ATT_UNROLL):
            qi = ATT_UNROLL * n + u
            nxt = jnp.minimum(qi + 1, nq - 1)
            score(nxt, nxt, True, **bufs[(u + 1) % 2])
            softmax(qi, 0.0, vt_ref[qi], first=True, **bufs[u % 2])
        return carry

    lax.fori_loop(0, nq // ATT_UNROLL, diag_group, 0)

    def entry(e):
        qi, kj = pair_q_ref[bh, e], pair_k_ref[bh, e]
        c = jnp.where(e < n_pairs, ((kj - qi) * t).astype(F32) * slope, NEG_INF)
        return qi, kj, c

    def score_entry(e, buf):
        qi, kj, _ = entry(e)
        score(qi, kj, False, **buf)

    def softmax_entry(e, buf):
        qi, kj, c = entry(e)
        softmax(qi, c, vt_ref[kj], first=False, **buf)

    score_entry(0, buf_a)

    def off_group(n, carry):
        for u in range(ATT_UNROLL):
            e = ATT_UNROLL * n + u
            score_entry(e + 1, bufs[(u + 1) % 2])
            softmax_entry(e, bufs[u % 2])
        return carry

    lax.fori_loop(0, (n_pairs + ATT_UNROLL - 1) // ATT_UNROLL, off_group, 0)

    lam = (jnp.exp(jnp.sum(lq1_ref[...] * lk1_ref[...], axis=-1, keepdims=True))
           - jnp.exp(jnp.sum(lq2_ref[...] * lk2_ref[...], axis=-1, keepdims=True))
           + lam_init)

    def finish(qi, carry):
        o = (acc_ref[qi, 0] * (1.0 / l_ref[qi, 0])
             - lam * (acc_ref[qi, 1] * (1.0 / l_ref[qi, 1])))
        ms = jnp.mean(o * o, axis=0, keepdims=True)
        o = o * lax.rsqrt(ms + SUBLN_EPS) * (1.0 - lam_init)
        o_ref[pl.ds(pl.multiple_of(qi * t, t), t), :] = (o.T * sg_ref[...]).astype(BF16)
        return carry

    lax.fori_loop(0, nq, finish, 0)


def _attention(pair_q, pair_k, pair_n, slopes, lq1, lk1, lq2, lk2, subln_g, qat, qbt, ka, kb, vt,
               *, batch, seq, lam_init):
    t = ATT_T
    nq = seq // t
    qt_spec = pl.BlockSpec((nq, HEAD_SLAB, t), lambda b, h, *_: (b, h, 0))
    row_spec = pl.BlockSpec((seq, HEAD_SLAB), lambda b, h, *_: (b, h))
    whole = pl.BlockSpec(memory_space=pltpu.VMEM)
    smem = pl.BlockSpec(memory_space=pltpu.SMEM)
    stat = pltpu.VMEM((nq, N_MAPS, 1, t), F32)
    acc = pltpu.VMEM((nq, N_MAPS, HEAD_SLAB, t), F32)
    score_buf = pltpu.VMEM((N_MAPS, t, t), F32)
    colmax_buf = pltpu.VMEM((N_MAPS, 1, t), F32)
    return pl.pallas_call(
        functools.partial(_attn_kernel, t=t, nq=nq, lam_init=lam_init),
        grid_spec=pltpu.PrefetchScalarGridSpec(
            num_scalar_prefetch=3,
            grid=(batch, N_HEADS),
            in_specs=[smem, whole, whole, whole, whole, whole,
                      qt_spec, qt_spec, row_spec, row_spec, qt_spec],
            out_specs=row_spec,
            scratch_shapes=[stat, stat, acc, score_buf, colmax_buf, score_buf, colmax_buf]),
        out_shape=jax.ShapeDtypeStruct((batch * seq, ATT_WIDTH), BF16),
        compiler_params=pltpu.CompilerParams(
            dimension_semantics=("arbitrary", "arbitrary"),
            vmem_limit_bytes=VMEM_LIMIT_BYTES),
        name="diff_attn",
    )(pair_q, pair_k, pair_n, slopes, lq1, lk1, lq2, lk2, subln_g, qat, qbt, ka, kb, vt)


def _off_diagonal_pairs(norms, slopes, batch, seq):
    nq = seq // ATT_T
    tile = np.arange(nq)
    below = tile[:, None] > tile[None, :]
    min_dist = ((tile[:, None] - tile[None, :] - 1) * ATT_T + 1).astype(np.float32)
    norms = norms.reshape(batch, nq, SUBLANES, HEAD_SLAB)
    qn = (jnp.sqrt(norms[:, :, 0, :N_HEADS]) * NORM_SAFETY).transpose(0, 2, 1)
    kn = (jnp.sqrt(norms[:, :, 1, :N_HEADS]) * NORM_SAFETY).transpose(0, 2, 1)
    bound = (qn[..., :, None] * (kn[..., None, :] + kn[..., :, None])
             - slopes[None, :, None, None] * min_dist)
    keep = below & ~(bound < -UNDERFLOW_LOG2)
    keep = keep.reshape(batch * N_HEADS, nq * nq)
    n_entries = nq * (nq - 1) // 2 + ATT_UNROLL
    order = jnp.argsort(~keep, axis=-1, stable=True)[:, :n_entries].astype(jnp.int32)
    return order // nq, order % nq, jnp.sum(keep, axis=-1, dtype=jnp.int32)


def _ffn_kernel(x_ref, att_ref, sgu_ref, wo_ref, g2_ref, wup_ref, cw_ref, cb_ref, wdn_ref, fg_ref,
                o_ref, carry_ref, act_ref, *, tm, tiles_per_batch, final):
    fc = FFN_FC
    halo = SUBLANES

    @pl.when(pl.program_id(0) % tiles_per_batch == 0)
    def _():
        carry_ref[...] = jnp.zeros_like(carry_ref)

    x1_parts, h2_parts = [], []
    for r0 in range(0, tm, tm // 2):
        r1 = r0 + tm // 2
        part = (x_ref[r0:r1, :]
                + jnp.dot(att_ref[r0:r1, :], wo_ref[0:ATT_WIDTH, :], preferred_element_type=F32)
                + jnp.dot(sgu_ref[r0:r1, :], wo_ref[ATT_WIDTH:ATT_WIDTH + SGU_WIDTH, :],
                          preferred_element_type=F32))
        ms = jnp.mean(part * part, axis=-1, keepdims=True)
        x1_parts.append(part)
        h2_parts.append((part * lax.rsqrt(ms + NORM_EPS) * g2_ref[...]).astype(BF16))
    x1 = jnp.concatenate(x1_parts, axis=0)
    h2 = jnp.concatenate(h2_parts, axis=0)

    rows8 = lax.broadcasted_iota(jnp.int32, (halo, fc), 0)
    for c in range(D_FF // fc):
        convs = []
        for half in range(2):
            col0 = half * D_FF + c * fc
            up = jnp.dot(h2, wup_ref[:, col0:col0 + fc], preferred_element_type=F32)
            prev = carry_ref[half, c]
            carry_ref[half, c] = up[tm - halo:tm, :]
            scale = 0.5 if half == 1 else 1.0
            w0 = cw_ref[0:1, col0:col0 + fc] * scale
            w1 = cw_ref[1:2, col0:col0 + fc] * scale
            w2 = cw_ref[2:3, col0:col0 + fc] * scale
            bias = cb_ref[:, col0:col0 + fc] * scale

            def conv(cur, back1, back2):
                return bias + w0 * back2 + w1 * back1 + w2 * cur

            body = conv(up, pltpu.roll(up, 1, axis=0), pltpu.roll(up, 2, axis=0))
            head = up[0:halo, :]
            head1 = jnp.where(rows8 < 1, pltpu.roll(prev, 1, axis=0), pltpu.roll(head, 1, axis=0))
            head2 = jnp.where(rows8 < 2, pltpu.roll(prev, 2, axis=0), pltpu.roll(head, 2, axis=0))
            convs.append(jnp.concatenate([conv(head, head1, head2), body[halo:, :]], axis=0))
        act_ref[:, c * fc:(c + 1) * fc] = (_gelu_tanh_x2(convs[0]) * convs[1]).astype(BF16)

    out = x1 + jnp.dot(act_ref[...], wdn_ref[...], preferred_element_type=F32)
    if final:
        ms = jnp.mean(out * out, axis=-1, keepdims=True)
        out = out * lax.rsqrt(ms + NORM_EPS) * fg_ref[...]
    o_ref[...] = out


def _ffn(layer, x2d, att, sgu, w_out, g2, w_up, conv_w, conv_b, w_down, final_g, *, seq, final):
    rows = x2d.shape[0]
    tm = FFN_TM
    grid = (rows // tm,)
    row_spec = lambda width: pl.BlockSpec((tm, width), lambda i: (i, 0))
    whole = pl.BlockSpec(memory_space=pltpu.VMEM)
    return pl.pallas_call(
        functools.partial(_ffn_kernel, tm=tm, tiles_per_batch=seq // tm, final=final),
        grid=grid,
        in_specs=[row_spec(D_MODEL), row_spec(ATT_WIDTH), row_spec(SGU_WIDTH),
                  _layer_spec(layer, w_out.shape[1:]), whole, _layer_spec(layer, w_up.shape[1:]),
                  _layer_spec(layer, conv_w.shape[1:]), whole, _layer_spec(layer, w_down.shape[1:]),
                  whole],
        out_specs=row_spec(D_MODEL),
        out_shape=jax.ShapeDtypeStruct((rows, D_MODEL), F32),
        scratch_shapes=[
            pltpu.VMEM((2, D_FF // FFN_FC, SUBLANES, FFN_FC), F32),
            pltpu.VMEM((tm, D_FF), BF16),
        ],
        compiler_params=pltpu.CompilerParams(
            dimension_semantics=("arbitrary",), vmem_limit_bytes=VMEM_LIMIT_BYTES),
        name="ffn",
    )(x2d, att, sgu, w_out, g2, w_up, conv_w, conv_b, w_down, final_g)


def kernel(x, norm1_g, w_in, lam_q1, lam_k1, lam_q2, lam_k2, subln_g, sgu_ln_g, sgu_ln_b,
           sgu_w, sgu_b, w_out, norm2_g, ffn_w_up, ffn_conv_w, ffn_conv_b, ffn_w_down, final_g):
    batch, seq, _ = x.shape
    rows = batch * seq
    x2d = x.reshape(rows, D_MODEL)
    slopes = jnp.asarray([_alibi_slope(h) * LOG2_E for h in range(N_HEADS)], dtype=F32)
    row1 = lambda a: a.reshape(1, -1).astype(F32)
    w_in, w_out, w_up, w_down = (w.astype(BF16) for w in (w_in, w_out, ffn_w_up, ffn_w_down))
    sgu_w, conv_w = sgu_w.astype(F32), ffn_conv_w.astype(F32)
    for l in range(DEPTH):
        lam_init = _lambda_init(l)
        sgu_b_full = jnp.broadcast_to(sgu_b[l].astype(F32)[:, :, None],
                                      (N_GROUPS, SGU_CHUNK, GROUP_DIM))
        qat, qbt, ka, kb, vt, norms, sgu = _mix_in(l, x2d, row1(norm1_g[l]), w_in,
                                                   row1(sgu_ln_g[l]), row1(sgu_ln_b[l]),
                                                   sgu_w, sgu_b_full)
        pair_q, pair_k, pair_n = _off_diagonal_pairs(norms, slopes, batch, seq)
        att = _attention(pair_q, pair_k, pair_n, slopes,
                         row1(lam_q1[l]), row1(lam_k1[l]), row1(lam_q2[l]), row1(lam_k2[l]),
                         row1(subln_g[l]), qat, qbt, ka, kb, vt,
                         batch=batch, seq=seq, lam_init=lam_init)
        x2d = _ffn(l, x2d, att, sgu, w_out, row1(norm2_g[l]), w_up, conv_w, row1(ffn_conv_b[l]),
                   w_down, row1(final_g), seq=seq, final=(l == DEPTH - 1))
    return x2d.reshape(batch, seq, D_MODEL)
```

```python
import functools
import math

import jax
import jax.numpy as jnp
import numpy as np
from jax import lax
from jax.experimental import pallas as pl
from jax.experimental.pallas import tpu as pltpu

F32 = jnp.float32
BF16 = jnp.bfloat16

D_MODEL = 1024
DEPTH = 2
N_HEADS = 4
N_MAPS = 2
HEAD_DIM = 64
HEAD_SLAB = 2 * HEAD_DIM
ATT_WIDTH = N_HEADS * HEAD_SLAB
N_GROUPS = 4
SGU_CHUNK = 128
GROUP_DIM = 128
SGU_WIDTH = N_GROUPS * GROUP_DIM
D_FF = 2816
CONV_WIDTH = 3
NORM_EPS = 1e-6
SUBLN_EPS = 1e-5
LN_EPS = 1e-5
NEG_INF = -1e30
LOG2_E = math.log2(math.e)

SUBLANES = 8
BF16_EXACT_INT = 256
UNDERFLOW_LOG2 = 150.0
NORM_SAFETY = 1.001
VMEM_LIMIT_BYTES = 56 * 1024 * 1024

MIX_TM = 1024
ATT_T = 512
ATT_UNROLL = 4
FFN_TM = 1024
FFN_FC = 256


def _gelu_tanh(x):
    c = math.sqrt(2.0 / math.pi)
    return 0.5 * x * (1.0 + jnp.tanh(c * (x + 0.044715 * (x * x * x))))


def _gelu_tanh_x2(x):
    c = math.sqrt(2.0 / math.pi)
    return x * (1.0 + jnp.tanh(x * (c + (c * 0.044715) * (x * x))))


def _alibi_slope(head):
    return 2.0 ** (-8.0 * (head + 1) / N_HEADS)


def _bf16_terms(value, n=3):
    terms = []
    rest = value
    for _ in range(n):
        term = float(np.asarray(rest, dtype=BF16).astype(np.float32))
        terms.append(term)
        rest -= term
    return terms


def _lambda_init(layer_idx):
    return 0.8 - 0.6 * math.exp(-0.3 * layer_idx)


def _layer_spec(layer, shape):
    zeros = (0,) * len(shape)
    return pl.BlockSpec((None,) + tuple(shape), lambda *_: (layer,) + zeros,
                        pipeline_mode=pl.Buffered(1))


def _alibi_operand_constants(t):
    pos = np.arange(t)
    n_terms = len(_bf16_terms(1.0))
    parts = [pos % BF16_EXACT_INT] * n_terms + [pos - pos % BF16_EXACT_INT] * n_terms
    parts = np.stack(parts, axis=1).astype(np.float32)
    first_lane = (HEAD_DIM, 0)
    k_aug = np.zeros((N_MAPS, t, HEAD_SLAB), np.float32)
    q_aug = np.zeros((N_HEADS * N_MAPS, HEAD_SLAB), np.float32)
    for mp, lane0 in enumerate(first_lane):
        k_aug[mp, :, lane0:lane0 + 2 * n_terms] = parts
        for h in range(N_HEADS):
            factor = _bf16_terms(_alibi_slope(h) * LOG2_E)
            q_aug[N_MAPS * h + mp, lane0:lane0 + 2 * n_terms] = factor + factor
    return jnp.asarray(k_aug), jnp.asarray(q_aug)


def _mixin_kernel(x_ref, g_ref, w_ref, lng_ref, lnb_ref, sw_ref, sb_ref, kaug_ref, qaug_ref,
                  qat_ref, qbt_ref, ka_ref, kb_ref, vt_ref, norm_ref, sgu_ref, *, tm, t):
    x = x_ref[...]
    ms = jnp.mean(x * x, axis=-1, keepdims=True)
    hb = (x * lax.rsqrt(ms + NORM_EPS) * g_ref[...]).astype(BF16)

    def proj(c0, width):
        return jnp.dot(hb, w_ref[:, c0:c0 + width], preferred_element_type=F32)

    zq = proj(0, ATT_WIDTH) * (HEAD_DIM ** -0.5 * LOG2_E)
    zk = proj(ATT_WIDTH, ATT_WIDTH)
    zv = proj(2 * ATT_WIDTH, ATT_WIDTH)
    in_a = lax.broadcasted_iota(jnp.int32, (tm, HEAD_SLAB), 1) < HEAD_DIM

    def max_sq_norm(slab):
        sq = slab.astype(BF16).astype(F32) ** 2
        return jnp.max(jnp.sum(sq, axis=-1, keepdims=True), axis=0, keepdims=True)

    stat_row = lax.broadcasted_iota(jnp.int32, (SUBLANES, HEAD_SLAB), 0)
    stat_lane = lax.broadcasted_iota(jnp.int32, (SUBLANES, HEAD_SLAB), 1)
    n_tiles = tm // t
    norms = [jnp.zeros((SUBLANES, HEAD_SLAB), F32)] * n_tiles
    k_aug = [jnp.concatenate([kaug_ref[mp]] * n_tiles, axis=0) for mp in range(N_MAPS)]
    for h in range(N_HEADS):
        lo, hi = h * HEAD_SLAB, (h + 1) * HEAD_SLAB
        qs, ks = zq[:, lo:hi], zk[:, lo:hi]
        ka_ref[:, lo:hi] = jnp.where(in_a, ks, k_aug[0]).astype(BF16)
        kb_ref[:, lo:hi] = jnp.where(in_a, k_aug[1], ks).astype(BF16)
        q_rows = qaug_ref[N_MAPS * h:N_MAPS * (h + 1), :]
        qat = jnp.where(in_a, qs, q_rows[0:1, :]).T.astype(BF16)
        qbt = jnp.where(in_a, q_rows[1:2, :], qs).T.astype(BF16)
        vt = zv[:, lo:hi].T.astype(BF16)
        for c in range(n_tiles):
            r0, r1 = c * t, (c + 1) * t
            norms[c] = jnp.where((stat_row == 0) & (stat_lane == h), max_sq_norm(qs[r0:r1]), norms[c])
            norms[c] = jnp.where((stat_row == 1) & (stat_lane == h), max_sq_norm(ks[r0:r1]), norms[c])
            qat_ref[c, lo:hi, :] = qat[:, r0:r1]
            qbt_ref[c, lo:hi, :] = qbt[:, r0:r1]
            vt_ref[c, lo:hi, :] = vt[:, r0:r1]
    for c in range(n_tiles):
        norm_ref[c] = norms[c]

    u = _gelu_tanh(proj(3 * ATT_WIDTH, SGU_WIDTH))
    vg = _gelu_tanh(proj(3 * ATT_WIDTH + SGU_WIDTH, SGU_WIDTH))
    row = lax.broadcasted_iota(jnp.int32, (SGU_CHUNK, SGU_CHUNK), 0)
    col = lax.broadcasted_iota(jnp.int32, (SGU_CHUNK, SGU_CHUNK), 1)
    tril = col <= row
    for g in range(N_GROUPS):
        lo, hi = g * GROUP_DIM, (g + 1) * GROUP_DIM
        y = vg[:, lo:hi]
        mu = jnp.mean(y, axis=-1, keepdims=True)
        yc = y - mu
        var = jnp.mean(yc * yc, axis=-1, keepdims=True)
        yn = (yc * lax.rsqrt(var + LN_EPS) * lng_ref[:, lo:hi] + lnb_ref[:, lo:hi]).astype(BF16)
        wm = jnp.where(tril, sw_ref[g], 0.0).astype(BF16)
        bias = sb_ref[g]
        n_chunks = tm // SGU_CHUNK
        chunks = jnp.concatenate([yn[c * SGU_CHUNK:(c + 1) * SGU_CHUNK, :]
                                  for c in range(n_chunks)], axis=1)
        vmix = jnp.dot(wm, chunks, preferred_element_type=F32)
        for c in range(n_chunks):
            r0, r1 = c * SGU_CHUNK, (c + 1) * SGU_CHUNK
            mixed = vmix[:, c * GROUP_DIM:(c + 1) * GROUP_DIM] + bias
            sgu_ref[r0:r1, lo:hi] = (u[r0:r1, lo:hi] * mixed).astype(BF16)


def _mix_in(layer, x2d, g, w_in, ln_g, ln_b, sgu_w, sgu_b_full):
    rows = x2d.shape[0]
    tm, t = MIX_TM, ATT_T
    assert tm % t == 0
    k_aug, q_aug = _alibi_operand_constants(t)
    grid = (rows // tm,)
    row_spec = lambda width: pl.BlockSpec((tm, width), lambda i: (i, 0))
    tr_spec = pl.BlockSpec((tm // t, ATT_WIDTH, t), lambda i: (i, 0, 0))
    whole = pl.BlockSpec(memory_space=pltpu.VMEM)
    row_sds = jax.ShapeDtypeStruct((rows, ATT_WIDTH), BF16)
    tr_sds = jax.ShapeDtypeStruct((rows // t, ATT_WIDTH, t), BF16)
    norm_spec = pl.BlockSpec((tm // t, SUBLANES, HEAD_SLAB), lambda i: (i, 0, 0))
    norm_sds = jax.ShapeDtypeStruct((rows // t, SUBLANES, HEAD_SLAB), F32)
    return pl.pallas_call(
        functools.partial(_mixin_kernel, tm=tm, t=t),
        grid=grid,
        in_specs=[row_spec(D_MODEL), whole, _layer_spec(layer, w_in.shape[1:]), whole, whole,
                  _layer_spec(layer, sgu_w.shape[1:]), whole, whole, whole],
        out_specs=[tr_spec, tr_spec, row_spec(ATT_WIDTH), row_spec(ATT_WIDTH), tr_spec, norm_spec,
                   row_spec(SGU_WIDTH)],
        out_shape=[tr_sds, tr_sds, row_sds, row_sds, tr_sds, norm_sds, row_sds],
        compiler_params=pltpu.CompilerParams(
            dimension_semantics=("arbitrary",), vmem_limit_bytes=VMEM_LIMIT_BYTES),
        name="mix_in",
    )(x2d, g, w_in, ln_g, ln_b, sgu_w, sgu_b_full, k_aug, q_aug)


def _score_block(qi, kj, masked, q_refs, k_refs, s_ref, bm_ref, *, t):
    k0 = pl.multiple_of(kj * t, t)
    if masked:
        key = lax.broadcasted_iota(jnp.int32, (t, t), 0)
        qry = lax.broadcasted_iota(jnp.int32, (t, t), 1)
        causal = key <= qry
    for mp in range(N_MAPS):
        st = jnp.dot(k_refs[mp][pl.ds(k0, t), :], q_refs[mp][qi], preferred_element_type=F32)
        if masked:
            st = jnp.where(causal, st, NEG_INF)
        s_ref[mp] = st
        bm_ref[mp] = jnp.max(st, axis=0, keepdims=True)


def _softmax_block(qi, c, vt_blk, s_ref, bm_ref, m_ref, l_ref, acc_ref, first):
    for mp in range(N_MAPS):
        bm = bm_ref[mp] + c
        m_new = bm if first else jnp.maximum(m_ref[qi, mp], bm)
        p = jnp.exp2(s_ref[mp] - (m_new - c))
        ps = jnp.sum(p, axis=0, keepdims=True)
        pv = jnp.dot(vt_blk, p.astype(BF16), preferred_element_type=F32)
        if first:
            l_ref[qi, mp] = ps
            acc_ref[qi, mp] = pv
        else:
            alpha = jnp.exp2(m_ref[qi, mp] - m_new)
            l_ref[qi, mp] = alpha * l_ref[qi, mp] + ps
            acc_ref[qi, mp] = alpha * acc_ref[qi, mp] + pv
        m_ref[qi, mp] = m_new


def _attn_kernel(pair_q_ref, pair_k_ref, pair_n_ref,
                 slopes_ref, lq1_ref, lk1_ref, lq2_ref, lk2_ref, sg_ref,
                 qat_ref, qbt_ref, ka_ref, kb_ref, vt_ref, o_ref,
                 m_ref, l_ref, acc_ref, sa_ref, bma_ref, sb_ref, bmb_ref, *, t, nq, lam_init):
    bh = pl.program_id(0) * N_HEADS + pl.program_id(1)
    slope = slopes_ref[pl.program_id(1)]
    n_pairs = pair_n_ref[bh]
    score = functools.partial(_score_block, q_refs=(qat_ref, qbt_ref), k_refs=(ka_ref, kb_ref), t=t)
    softmax = functools.partial(_softmax_block, m_ref=m_ref, l_ref=l_ref, acc_ref=acc_ref)
    buf_a = dict(s_ref=sa_ref, bm_ref=bma_ref)
    buf_b = dict(s_ref=sb_ref, bm_ref=bmb_ref)

    bufs = (buf_a, buf_b)

    score(0, 0, True, **buf_a)

    def diag_group(n, carry):
        for u in range(ATT_UNROLL):
            qi = ATT_UNROLL * n + u
            nxt = jnp.minimum(qi + 1, nq - 1)
            score(nxt, nxt, True, **bufs[(u + 1) % 2])
            softmax(qi, 0.0, vt_ref[qi], first=True, **bufs[u % 2])
        return carry

    lax.fori_loop(0, nq // ATT_UNROLL, diag_group, 0)

    def entry(e):
        qi, kj = pair_q_ref[bh, e], pair_k_ref[bh, e]
        c = jnp.where(e < n_pairs, ((kj - qi) * t).astype(F32) * slope, NEG_INF)
        return qi, kj, c

    def score_entry(e, buf):
        qi, kj, _ = entry(e)
        score(qi, kj, False, **buf)

    def softmax_entry(e, buf):
        qi, kj, c = entry(e)
        softmax(qi, c, vt_ref[kj], first=False, **buf)

    score_entry(0, buf_a)

    def off_group(n, carry):
        for u in range(ATT_UNROLL):
            e = ATT_UNROLL * n + u
            score_entry(e + 1, bufs[(u + 1) % 2])
            softmax_entry(e, bufs[u % 2])
        return carry

    lax.fori_loop(0, (n_pairs + ATT_UNROLL - 1) // ATT_UNROLL, off_group, 0)

    lam = (jnp.exp(jnp.sum(lq1_ref[...] * lk1_ref[...], axis=-1, keepdims=True))
           - jnp.exp(jnp.sum(lq2_ref[...] * lk2_ref[...], axis=-1, keepdims=True))
           + lam_init)

    def finish(qi, carry):
        o = (acc_ref[qi, 0] * (1.0 / l_ref[qi, 0])
             - lam * (acc_ref[qi, 1] * (1.0 / l_ref[qi, 1])))
        ms = jnp.mean(o * o, axis=0, keepdims=True)
        o = o * lax.rsqrt(ms + SUBLN_EPS) * (1.0 - lam_init)
        o_ref[pl.ds(pl.multiple_of(qi * t, t), t), :] = (o.T * sg_ref[...]).astype(BF16)
        return carry

    lax.fori_loop(0, nq, finish, 0, unroll=4)


def _attention(pair_q, pair_k, pair_n, slopes, lq1, lk1, lq2, lk2, subln_g, qat, qbt, ka, kb, vt,
               *, batch, seq, lam_init):
    t = ATT_T
    nq = seq // t
    qt_spec = pl.BlockSpec((nq, HEAD_SLAB, t), lambda b, h, *_: (b, h, 0))
    row_spec = pl.BlockSpec((seq, HEAD_SLAB), lambda b, h, *_: (b, h))
    whole = pl.BlockSpec(memory_space=pltpu.VMEM)
    smem = pl.BlockSpec(memory_space=pltpu.SMEM)
    stat = pltpu.VMEM((nq, N_MAPS, 1, t), F32)
    acc = pltpu.VMEM((nq, N_MAPS, HEAD_SLAB, t), F32)
    score_buf = pltpu.VMEM((N_MAPS, t, t), F32)
    colmax_buf = pltpu.VMEM((N_MAPS, 1, t), F32)
    return pl.pallas_call(
        functools.partial(_attn_kernel, t=t, nq=nq, lam_init=lam_init),
        grid_spec=pltpu.PrefetchScalarGridSpec(
            num_scalar_prefetch=3,
            grid=(batch, N_HEADS),
            in_specs=[smem, whole, whole, whole, whole, whole,
                      qt_spec, qt_spec, row_spec, row_spec, qt_spec],
            out_specs=row_spec,
            scratch_shapes=[stat, stat, acc, score_buf, colmax_buf, score_buf, colmax_buf]),
        out_shape=jax.ShapeDtypeStruct((batch * seq, ATT_WIDTH), BF16),
        compiler_params=pltpu.CompilerParams(
            dimension_semantics=("arbitrary", "arbitrary"),
            vmem_limit_bytes=VMEM_LIMIT_BYTES),
        name="diff_attn",
    )(pair_q, pair_k, pair_n, slopes, lq1, lk1, lq2, lk2, subln_g, qat, qbt, ka, kb, vt)


def _off_diagonal_pairs(norms, slopes, batch, seq):
    nq = seq // ATT_T
    tile = np.arange(nq)
    below = tile[:, None] > tile[None, :]
    min_dist = ((tile[:, None] - tile[None, :] - 1) * ATT_T + 1).astype(np.float32)
    norms = norms.reshape(batch, nq, SUBLANES, HEAD_SLAB)
    qn = (jnp.sqrt(norms[:, :, 0, :N_HEADS]) * NORM_SAFETY).transpose(0, 2, 1)
    kn = (jnp.sqrt(norms[:, :, 1, :N_HEADS]) * NORM_SAFETY).transpose(0, 2, 1)
    bound = (qn[..., :, None] * (kn[..., None, :] + kn[..., :, None])
             - slopes[None, :, None, None] * min_dist)
    keep = below & ~(bound < -UNDERFLOW_LOG2)
    keep = keep.reshape(batch * N_HEADS, nq * nq)
    n_entries = nq * (nq - 1) // 2 + ATT_UNROLL
    order = jnp.argsort(~keep, axis=-1, stable=True)[:, :n_entries].astype(jnp.int32)
    return order // nq, order % nq, jnp.sum(keep, axis=-1, dtype=jnp.int32)


def _ffn_kernel(x_ref, att_ref, sgu_ref, wo_ref, g2_ref, wup_ref, cw_ref, cb_ref, wdn_ref, fg_ref,
                o_ref, carry_ref, act_ref, *, tm, tiles_per_batch, final):
    fc = FFN_FC
    halo = SUBLANES

    @pl.when(pl.program_id(0) % tiles_per_batch == 0)
    def _():
        carry_ref[...] = jnp.zeros_like(carry_ref)

    x1_parts, h2_parts = [], []
    for r0 in range(0, tm, tm // 2):
        r1 = r0 + tm // 2
        part = (x_ref[r0:r1, :]
                + jnp.dot(att_ref[r0:r1, :], wo_ref[0:ATT_WIDTH, :], preferred_element_type=F32)
                + jnp.dot(sgu_ref[r0:r1, :], wo_ref[ATT_WIDTH:ATT_WIDTH + SGU_WIDTH, :],
                          preferred_element_type=F32))
        ms = jnp.mean(part * part, axis=-1, keepdims=True)
        x1_parts.append(part)
        h2_parts.append((part * lax.rsqrt(ms + NORM_EPS) * g2_ref[...]).astype(BF16))
    x1 = jnp.concatenate(x1_parts, axis=0)
    h2 = jnp.concatenate(h2_parts, axis=0)

    rows8 = lax.broadcasted_iota(jnp.int32, (halo, fc), 0)
    for c in range(D_FF // fc):
        convs = []
        for half in range(2):
            col0 = half * D_FF + c * fc
            up = jnp.dot(h2, wup_ref[:, col0:col0 + fc], preferred_element_type=F32)
            prev = carry_ref[half, c]
            carry_ref[half, c] = up[tm - halo:tm, :]
            scale = 0.5 if half == 1 else 1.0
            w0 = cw_ref[0:1, col0:col0 + fc] * scale
            w1 = cw_ref[1:2, col0:col0 + fc] * scale
            w2 = cw_ref[2:3, col0:col0 + fc] * scale
            bias = cb_ref[:, col0:col0 + fc] * scale

            def conv(cur, back1, back2):
                return bias + w0 * back2 + w1 * back1 + w2 * cur

            body = conv(up, pltpu.roll(up, 1, axis=0), pltpu.roll(up, 2, axis=0))
            head = up[0:halo, :]
            head1 = jnp.where(rows8 < 1, pltpu.roll(prev, 1, axis=0), pltpu.roll(head, 1, axis=0))
            head2 = jnp.where(rows8 < 2, pltpu.roll(prev, 2, axis=0), pltpu.roll(head, 2, axis=0))
            convs.append(jnp.concatenate([conv(head, head1, head2), body[halo:, :]], axis=0))
        act_ref[:, c * fc:(c + 1) * fc] = (_gelu_tanh_x2(convs[0]) * convs[1]).astype(BF16)

    out = x1 + jnp.dot(act_ref[...], wdn_ref[...], preferred_element_type=F32)
    if final:
        ms = jnp.mean(out * out, axis=-1, keepdims=True)
        out = out * lax.rsqrt(ms + NORM_EPS) * fg_ref[...]
    o_ref[...] = out


def _ffn(layer, x2d, att, sgu, w_out, g2, w_up, conv_w, conv_b, w_down, final_g, *, seq, final):
    rows = x2d.shape[0]
    tm = FFN_TM
    grid = (rows // tm,)
    row_spec = lambda width: pl.BlockSpec((tm, width), lambda i: (i, 0))
    whole = pl.BlockSpec(memory_space=pltpu.VMEM)
    return pl.pallas_call(
        functools.partial(_ffn_kernel, tm=tm, tiles_per_batch=seq // tm, final=final),
        grid=grid,
        in_specs=[row_spec(D_MODEL), row_spec(ATT_WIDTH), row_spec(SGU_WIDTH),
                  _layer_spec(layer, w_out.shape[1:]), whole, _layer_spec(layer, w_up.shape[1:]),
                  _layer_spec(layer, conv_w.shape[1:]), whole, _layer_spec(layer, w_down.shape[1:]),
                  whole],
        out_specs=row_spec(D_MODEL),
        out_shape=jax.ShapeDtypeStruct((rows, D_MODEL), F32),
        scratch_shapes=[
            pltpu.VMEM((2, D_FF // FFN_FC, SUBLANES, FFN_FC), F32),
            pltpu.VMEM((tm, D_FF), BF16),
        ],
        compiler_params=pltpu.CompilerParams(
            dimension_semantics=("arbitrary",), vmem_limit_bytes=VMEM_LIMIT_BYTES),
        name="ffn",
    )(x2d, att, sgu, w_out, g2, w_up, conv_w, conv_b, w_down, final_g)


def kernel(x, norm1_g, w_in, lam_q1, lam_k1, lam_q2, lam_k2, subln_g, sgu_ln_g, sgu_ln_b,
           sgu_w, sgu_b, w_out, norm2_g, ffn_w_up, ffn_conv_w, ffn_conv_b, ffn_w_down, final_g):
    batch, seq, _ = x.shape
    rows = batch * seq
    x2d = x.reshape(rows, D_MODEL)
    slopes = jnp.asarray([_alibi_slope(h) * LOG2_E for h in range(N_HEADS)], dtype=F32)
    row1 = lambda a: a.reshape(1, -1).astype(F32)
    w_in, w_out, w_up, w_down = (w.astype(BF16) for w in (w_in, w_out, ffn_w_up, ffn_w_down))
    sgu_w, conv_w = sgu_w.astype(F32), ffn_conv_w.astype(F32)
    for l in range(DEPTH):
        lam_init = _lambda_init(l)
        sgu_b_full = jnp.broadcast_to(sgu_b[l].astype(F32)[:, :, None],
                                      (N_GROUPS, SGU_CHUNK, GROUP_DIM))
        qat, qbt, ka, kb, vt, norms, sgu = _mix_in(l, x2d, row1(norm1_g[l]), w_in,
                                                   row1(sgu_ln_g[l]), row1(sgu_ln_b[l]),
                                                   sgu_w, sgu_b_full)
        pair_q, pair_k, pair_n = _off_diagonal_pairs(norms, slopes, batch, seq)
        att = _attention(pair_q, pair_k, pair_n, slopes,
                         row1(lam_q1[l]), row1(lam_k1[l]), row1(lam_q2[l]), row1(lam_k2[l]),
                         row1(subln_g[l]), qat, qbt, ka, kb, vt,
                         batch=batch, seq=seq, lam_init=lam_init)
        x2d = _ffn(l, x2d, att, sgu, w_out, row1(norm2_g[l]), w_up, conv_w, row1(ffn_conv_b[l]),
                   w_down, row1(final_g), seq=seq, final=(l == DEPTH - 1))
    return x2d.reshape(batch, seq, D_MODEL)
```

```python
import functools
import math

import jax
import jax.numpy as jnp
import numpy as np
from jax import lax
from jax.experimental import pallas as pl
from jax.experimental.pallas import tpu as pltpu

F32 = jnp.float32
BF16 = jnp.bfloat16

D_MODEL = 1024
DEPTH = 2
N_HEADS = 4
N_MAPS = 2
HEAD_DIM = 64
HEAD_SLAB = 2 * HEAD_DIM
ATT_WIDTH = N_HEADS * HEAD_SLAB
N_GROUPS = 4
SGU_CHUNK = 128
GROUP_DIM = 128
SGU_WIDTH = N_GROUPS * GROUP_DIM
D_FF = 2816
CONV_WIDTH = 3
NORM_EPS = 1e-6
SUBLN_EPS = 1e-5
LN_EPS = 1e-5
NEG_INF = -1e30
LOG2_E = math.log2(math.e)

SUBLANES = 8
BF16_EXACT_INT = 256
UNDERFLOW_LOG2 = 150.0
NORM_SAFETY = 1.001
VMEM_LIMIT_BYTES = 56 * 1024 * 1024

MIX_TM = 1024
ATT_T = 512
ATT_UNROLL = 4
FFN_TM = 1024
FFN_FC = 256


def _gelu_tanh(x):
    c = math.sqrt(2.0 / math.pi)
    return 0.5 * x * (1.0 + jnp.tanh(c * (x + 0.044715 * (x * x * x))))


def _gelu_tanh_x2(x):
    c = math.sqrt(2.0 / math.pi)
    return x * (1.0 + jnp.tanh(x * (c + (c * 0.044715) * (x * x))))


def _alibi_slope(head):
    return 2.0 ** (-8.0 * (head + 1) / N_HEADS)


def _bf16_terms(value, n=3):
    terms = []
    rest = value
    for _ in range(n):
        term = float(np.asarray(rest, dtype=BF16).astype(np.float32))
        terms.append(term)
        rest -= term
    return terms


def _lambda_init(layer_idx):
    return 0.8 - 0.6 * math.exp(-0.3 * layer_idx)


def _layer_spec(layer, shape):
    zeros = (0,) * len(shape)
    return pl.BlockSpec((None,) + tuple(shape), lambda *_: (layer,) + zeros,
                        pipeline_mode=pl.Buffered(1))


def _alibi_operand_constants(t):
    pos = np.arange(t)
    n_terms = len(_bf16_terms(1.0))
    parts = [pos % BF16_EXACT_INT] * n_terms + [pos - pos % BF16_EXACT_INT] * n_terms
    parts = np.stack(parts, axis=1).astype(np.float32)
    first_lane = (HEAD_DIM, 0)
    k_aug = np.zeros((N_MAPS, t, HEAD_SLAB), np.float32)
    q_aug = np.zeros((N_HEADS * N_MAPS, HEAD_SLAB), np.float32)
    for mp, lane0 in enumerate(first_lane):
        k_aug[mp, :, lane0:lane0 + 2 * n_terms] = parts
        for h in range(N_HEADS):
            factor = _bf16_terms(_alibi_slope(h) * LOG2_E)
            q_aug[N_MAPS * h + mp, lane0:lane0 + 2 * n_terms] = factor + factor
    return jnp.asarray(k_aug), jnp.asarray(q_aug)


def _mixin_kernel(x_ref, g_ref, w_ref, lng_ref, lnb_ref, sw_ref, sb_ref, kaug_ref, qaug_ref,
                  qat_ref, qbt_ref, ka_ref, kb_ref, vt_ref, norm_ref, sgu_ref, *, tm, t):
    x = x_ref[...]
    ms = jnp.mean(x * x, axis=-1, keepdims=True)
    hb = (x * lax.rsqrt(ms + NORM_EPS) * g_ref[...]).astype(BF16)

    def proj(c0, width):
        return jnp.dot(hb, w_ref[:, c0:c0 + width], preferred_element_type=F32)

    zq = proj(0, ATT_WIDTH) * (HEAD_DIM ** -0.5 * LOG2_E)
    zk = proj(ATT_WIDTH, ATT_WIDTH)
    zv = proj(2 * ATT_WIDTH, ATT_WIDTH)
    in_a = lax.broadcasted_iota(jnp.int32, (tm, HEAD_SLAB), 1) < HEAD_DIM

    def max_sq_norm(slab):
        sq = slab.astype(BF16).astype(F32) ** 2
        return jnp.max(jnp.sum(sq, axis=-1, keepdims=True), axis=0, keepdims=True)

    stat_row = lax.broadcasted_iota(jnp.int32, (SUBLANES, HEAD_SLAB), 0)
    stat_lane = lax.broadcasted_iota(jnp.int32, (SUBLANES, HEAD_SLAB), 1)
    n_tiles = tm // t
    norms = [jnp.zeros((SUBLANES, HEAD_SLAB), F32)] * n_tiles
    k_aug = [jnp.concatenate([kaug_ref[mp]] * n_tiles, axis=0) for mp in range(N_MAPS)]
    for h in range(N_HEADS):
        lo, hi = h * HEAD_SLAB, (h + 1) * HEAD_SLAB
        qs, ks = zq[:, lo:hi], zk[:, lo:hi]
        ka_ref[:, lo:hi] = jnp.where(in_a, ks, k_aug[0]).astype(BF16)
        kb_ref[:, lo:hi] = jnp.where(in_a, k_aug[1], ks).astype(BF16)
        q_rows = qaug_ref[N_MAPS * h:N_MAPS * (h + 1), :]
        qat = jnp.where(in_a, qs, q_rows[0:1, :]).T.astype(BF16)
        qbt = jnp.where(in_a, q_rows[1:2, :], qs).T.astype(BF16)
        vt = zv[:, lo:hi].T.astype(BF16)
        for c in range(n_tiles):
            r0, r1 = c * t, (c + 1) * t
            norms[c] = jnp.where((stat_row == 0) & (stat_lane == h), max_sq_norm(qs[r0:r1]), norms[c])
            norms[c] = jnp.where((stat_row == 1) & (stat_lane == h), max_sq_norm(ks[r0:r1]), norms[c])
            qat_ref[c, lo:hi, :] = qat[:, r0:r1]
            qbt_ref[c, lo:hi, :] = qbt[:, r0:r1]
            vt_ref[c, lo:hi, :] = vt[:, r0:r1]
    for c in range(n_tiles):
        norm_ref[c] = norms[c]

    u = _gelu_tanh(proj(3 * ATT_WIDTH, SGU_WIDTH))
    vg = _gelu_tanh(proj(3 * ATT_WIDTH + SGU_WIDTH, SGU_WIDTH))
    row = lax.broadcasted_iota(jnp.int32, (SGU_CHUNK, SGU_CHUNK), 0)
    col = lax.broadcasted_iota(jnp.int32, (SGU_CHUNK, SGU_CHUNK), 1)
    tril = col <= row
    for g in range(N_GROUPS):
        lo, hi = g * GROUP_DIM, (g + 1) * GROUP_DIM
        y = vg[:, lo:hi]
        mu = jnp.mean(y, axis=-1, keepdims=True)
        yc = y - mu
        var = jnp.mean(yc * yc, axis=-1, keepdims=True)
        yn = (yc * lax.rsqrt(var + LN_EPS) * lng_ref[:, lo:hi] + lnb_ref[:, lo:hi]).astype(BF16)
        wm = jnp.where(tril, sw_ref[g], 0.0).astype(BF16)
        bias = sb_ref[g]
        n_chunks = tm // SGU_CHUNK
        chunks = jnp.concatenate([yn[c * SGU_CHUNK:(c + 1) * SGU_CHUNK, :]
                                  for c in range(n_chunks)], axis=1)
        vmix = jnp.dot(wm, chunks, preferred_element_type=F32)
        for c in range(n_chunks):
            r0, r1 = c * SGU_CHUNK, (c + 1) * SGU_CHUNK
            mixed = vmix[:, c * GROUP_DIM:(c + 1) * GROUP_DIM] + bias
            sgu_ref[r0:r1, lo:hi] = (u[r0:r1, lo:hi] * mixed).astype(BF16)


def _mix_in(layer, x2d, g, w_in, ln_g, ln_b, sgu_w, sgu_b_full):
    rows = x2d.shape[0]
    tm, t = MIX_TM, ATT_T
    assert tm % t == 0
    k_aug, q_aug = _alibi_operand_constants(t)
    grid = (rows // tm,)
    row_spec = lambda width: pl.BlockSpec((tm, width), lambda i: (i, 0))
    tr_spec = pl.BlockSpec((tm // t, ATT_WIDTH, t), lambda i: (i, 0, 0))
    whole = pl.BlockSpec(memory_space=pltpu.VMEM)
    row_sds = jax.ShapeDtypeStruct((rows, ATT_WIDTH), BF16)
    tr_sds = jax.ShapeDtypeStruct((rows // t, ATT_WIDTH, t), BF16)
    norm_spec = pl.BlockSpec((tm // t, SUBLANES, HEAD_SLAB), lambda i: (i, 0, 0))
    norm_sds = jax.ShapeDtypeStruct((rows // t, SUBLANES, HEAD_SLAB), F32)
    return pl.pallas_call(
        functools.partial(_mixin_kernel, tm=tm, t=t),
        grid=grid,
        in_specs=[row_spec(D_MODEL), whole, _layer_spec(layer, w_in.shape[1:]), whole, whole,
                  _layer_spec(layer, sgu_w.shape[1:]), whole, whole, whole],
        out_specs=[tr_spec, tr_spec, row_spec(ATT_WIDTH), row_spec(ATT_WIDTH), tr_spec, norm_spec,
                   row_spec(SGU_WIDTH)],
        out_shape=[tr_sds, tr_sds, row_sds, row_sds, tr_sds, norm_sds, row_sds],
        compiler_params=pltpu.CompilerParams(
            dimension_semantics=("arbitrary",), vmem_limit_bytes=VMEM_LIMIT_BYTES),
        name="mix_in",
    )(x2d, g, w_in, ln_g, ln_b, sgu_w, sgu_b_full, k_aug, q_aug)


def _score_block(qi, kj, masked, q_refs, k_refs, s_ref, bm_ref, *, t):
    k0 = pl.multiple_of(kj * t, t)
    if masked:
        key = lax.broadcasted_iota(jnp.int32, (t, t), 0)
        qry = lax.broadcasted_iota(jnp.int32, (t, t), 1)
        causal = key <= qry
    for mp in range(N_MAPS):
        st = jnp.dot(k_refs[mp][pl.ds(k0, t), :], q_refs[mp][qi], preferred_element_type=F32)
        if masked:
            st = jnp.where(causal, st, NEG_INF)
        s_ref[mp] = st
        bm_ref[mp] = jnp.max(st, axis=0, keepdims=True)


def _softmax_block(qi, c, vt_blk, s_ref, bm_ref, m_ref, l_ref, acc_ref, first):
    for mp in range(N_MAPS):
        bm = bm_ref[mp] + c
        m_new = bm if first else jnp.maximum(m_ref[qi, mp], bm)
        p = jnp.exp2(s_ref[mp] - (m_new - c))
        ps = jnp.sum(p, axis=0, keepdims=True)
        pv = jnp.dot(vt_blk, p.astype(BF16), preferred_element_type=F32)
        if first:
            l_ref[qi, mp] = ps
            acc_ref[qi, mp] = pv
        else:
            alpha = jnp.exp2(m_ref[qi, mp] - m_new)
            l_ref[qi, mp] = alpha * l_ref[qi, mp] + ps
            acc_ref[qi, mp] = alpha * acc_ref[qi, mp] + pv
        m_ref[qi, mp] = m_new


def _attn_kernel(pair_q_ref, pair_k_ref, pair_n_ref,
                 slopes_ref, lq1_ref, lk1_ref, lq2_ref, lk2_ref, sg_ref,
                 qat_ref, qbt_ref, ka_ref, kb_ref, vt_ref, o_ref,
                 m_ref, l_ref, acc_ref, sa_ref, bma_ref, sb_ref, bmb_ref, *, t, nq, lam_init):
    bh = pl.program_id(0) * N_HEADS + pl.program_id(1)
    slope = slopes_ref[pl.program_id(1)]
    n_pairs = pair_n_ref[bh]
    score = functools.partial(_score_block, q_refs=(qat_ref, qbt_ref), k_refs=(ka_ref, kb_ref), t=t)
    softmax = functools.partial(_softmax_block, m_ref=m_ref, l_ref=l_ref, acc_ref=acc_ref)
    buf_a = dict(s_ref=sa_ref, bm_ref=bma_ref)
    buf_b = dict(s_ref=sb_ref, bm_ref=bmb_ref)

    bufs = (buf_a, buf_b)

    def entry(e):
        qi, kj = pair_q_ref[bh, e], pair_k_ref[bh, e]
        c = jnp.where(e < n_pairs, ((kj - qi) * t).astype(F32) * slope, NEG_INF)
        return qi, kj, c

    def score_entry(e, buf):
        qi, kj, _ = entry(e)
        score(qi, kj, False, **buf)

    def softmax_entry(e, buf):
        qi, kj, c = entry(e)
        softmax(qi, c, vt_ref[kj], first=False, **buf)

    score(0, 0, True, **buf_a)

    def diag_group(n, carry):
        for u in range(ATT_UNROLL):
            qi = ATT_UNROLL * n + u
            if isinstance(qi, int) and qi == nq - 1:
                score_entry(0, bufs[(u + 1) % 2])
            else:
                score(qi + 1, qi + 1, True, **bufs[(u + 1) % 2])
            softmax(qi, 0.0, vt_ref[qi], first=True, **bufs[u % 2])
        return carry

    lax.fori_loop(0, nq // ATT_UNROLL - 1, diag_group, 0)
    diag_group(nq // ATT_UNROLL - 1, 0)

    def off_group(n, carry):
        for u in range(ATT_UNROLL):
            e = ATT_UNROLL * n + u
            score_entry(e + 1, bufs[(u + 1) % 2])
            softmax_entry(e, bufs[u % 2])
        return carry

    lax.fori_loop(0, (n_pairs + ATT_UNROLL - 1) // ATT_UNROLL, off_group, 0)

    lam = (jnp.exp(jnp.sum(lq1_ref[...] * lk1_ref[...], axis=-1, keepdims=True))
           - jnp.exp(jnp.sum(lq2_ref[...] * lk2_ref[...], axis=-1, keepdims=True))
           + lam_init)

    def finish(qi, carry):
        o = (acc_ref[qi, 0] * (1.0 / l_ref[qi, 0])
             - lam * (acc_ref[qi, 1] * (1.0 / l_ref[qi, 1])))
        ms = jnp.mean(o * o, axis=0, keepdims=True)
        o = o * lax.rsqrt(ms + SUBLN_EPS) * (1.0 - lam_init)
        o_ref[pl.ds(pl.multiple_of(qi * t, t), t), :] = (o.T * sg_ref[...]).astype(BF16)
        return carry

    lax.fori_loop(0, nq, finish, 0, unroll=4)


def _attention(pair_q, pair_k, pair_n, slopes, lq1, lk1, lq2, lk2, subln_g, qat, qbt, ka, kb, vt,
               *, batch, seq, lam_init):
    t = ATT_T
    nq = seq // t
    qt_spec = pl.BlockSpec((nq, HEAD_SLAB, t), lambda b, h, *_: (b, h, 0))
    row_spec = pl.BlockSpec((seq, HEAD_SLAB), lambda b, h, *_: (b, h))
    whole = pl.BlockSpec(memory_space=pltpu.VMEM)
    smem = pl.BlockSpec(memory_space=pltpu.SMEM)
    stat = pltpu.VMEM((nq, N_MAPS, 1, t), F32)
    acc = pltpu.VMEM((nq, N_MAPS, HEAD_SLAB, t), F32)
    score_buf = pltpu.VMEM((N_MAPS, t, t), F32)
    colmax_buf = pltpu.VMEM((N_MAPS, 1, t), F32)
    return pl.pallas_call(
        functools.partial(_attn_kernel, t=t, nq=nq, lam_init=lam_init),
        grid_spec=pltpu.PrefetchScalarGridSpec(
            num_scalar_prefetch=3,
            grid=(batch, N_HEADS),
            in_specs=[smem, whole, whole, whole, whole, whole,
                      qt_spec, qt_spec, row_spec, row_spec, qt_spec],
            out_specs=row_spec,
            scratch_shapes=[stat, stat, acc, score_buf, colmax_buf, score_buf, colmax_buf]),
        out_shape=jax.ShapeDtypeStruct((batch * seq, ATT_WIDTH), BF16),
        compiler_params=pltpu.CompilerParams(
            dimension_semantics=("arbitrary", "arbitrary"),
            vmem_limit_bytes=VMEM_LIMIT_BYTES),
        name="diff_attn",
    )(pair_q, pair_k, pair_n, slopes, lq1, lk1, lq2, lk2, subln_g, qat, qbt, ka, kb, vt)


def _off_diagonal_pairs(norms, slopes, batch, seq):
    nq = seq // ATT_T
    tile = np.arange(nq)
    below = tile[:, None] > tile[None, :]
    min_dist = ((tile[:, None] - tile[None, :] - 1) * ATT_T + 1).astype(np.float32)
    norms = norms.reshape(batch, nq, SUBLANES, HEAD_SLAB)
    qn = (jnp.sqrt(norms[:, :, 0, :N_HEADS]) * NORM_SAFETY).transpose(0, 2, 1)
    kn = (jnp.sqrt(norms[:, :, 1, :N_HEADS]) * NORM_SAFETY).transpose(0, 2, 1)
    bound = (qn[..., :, None] * (kn[..., None, :] + kn[..., :, None])
             - slopes[None, :, None, None] * min_dist)
    keep = below & ~(bound < -UNDERFLOW_LOG2)
    keep = keep.reshape(batch * N_HEADS, nq * nq)
    n_entries = nq * (nq - 1) // 2 + ATT_UNROLL
    order = jnp.argsort(~keep, axis=-1, stable=True)[:, :n_entries].astype(jnp.int32)
    return order // nq, order % nq, jnp.sum(keep, axis=-1, dtype=jnp.int32)


def _ffn_kernel(x_ref, att_ref, sgu_ref, wo_ref, g2_ref, wup_ref, cw_ref, cb_ref, wdn_ref, fg_ref,
                o_ref, carry_ref, act_ref, *, tm, tiles_per_batch, final):
    fc = FFN_FC
    halo = SUBLANES

    @pl.when(pl.program_id(0) % tiles_per_batch == 0)
    def _():
        carry_ref[...] = jnp.zeros_like(carry_ref)

    x1_parts, h2_parts = [], []
    for r0 in range(0, tm, tm // 2):
        r1 = r0 + tm // 2
        part = (x_ref[r0:r1, :]
                + jnp.dot(att_ref[r0:r1, :], wo_ref[0:ATT_WIDTH, :], preferred_element_type=F32)
                + jnp.dot(sgu_ref[r0:r1, :], wo_ref[ATT_WIDTH:ATT_WIDTH + SGU_WIDTH, :],
                          preferred_element_type=F32))
        ms = jnp.mean(part * part, axis=-1, keepdims=True)
        x1_parts.append(part)
        h2_parts.append((part * lax.rsqrt(ms + NORM_EPS) * g2_ref[...]).astype(BF16))
    x1 = jnp.concatenate(x1_parts, axis=0)
    h2 = jnp.concatenate(h2_parts, axis=0)

    rows8 = lax.broadcasted_iota(jnp.int32, (halo, fc), 0)
    for c in range(D_FF // fc):
        convs = []
        for half in range(2):
            col0 = half * D_FF + c * fc
            up = jnp.dot(h2, wup_ref[:, col0:col0 + fc], preferred_element_type=F32)
            prev = carry_ref[half, c]
            carry_ref[half, c] = up[tm - halo:tm, :]
            scale = 0.5 if half == 1 else 1.0
            w0 = cw_ref[0:1, col0:col0 + fc] * scale
            w1 = cw_ref[1:2, col0:col0 + fc] * scale
            w2 = cw_ref[2:3, col0:col0 + fc] * scale
            bias = cb_ref[:, col0:col0 + fc] * scale

            def conv(cur, back1, back2):
                return bias + w0 * back2 + w1 * back1 + w2 * cur

            body = conv(up, pltpu.roll(up, 1, axis=0), pltpu.roll(up, 2, axis=0))
            head = up[0:halo, :]
            head1 = jnp.where(rows8 < 1, pltpu.roll(prev, 1, axis=0), pltpu.roll(head, 1, axis=0))
            head2 = jnp.where(rows8 < 2, pltpu.roll(prev, 2, axis=0), pltpu.roll(head, 2, axis=0))
            convs.append(jnp.concatenate([conv(head, head1, head2), body[halo:, :]], axis=0))
        act_ref[:, c * fc:(c + 1) * fc] = (_gelu_tanh_x2(convs[0]) * convs[1]).astype(BF16)

    out = x1 + jnp.dot(act_ref[...], wdn_ref[...], preferred_element_type=F32)
    if final:
        ms = jnp.mean(out * out, axis=-1, keepdims=True)
        out = out * lax.rsqrt(ms + NORM_EPS) * fg_ref[...]
    o_ref[...] = out


def _ffn(layer, x2d, att, sgu, w_out, g2, w_up, conv_w, conv_b, w_down, final_g, *, seq, final):
    rows = x2d.shape[0]
    tm = FFN_TM
    grid = (rows // tm,)
    row_spec = lambda width: pl.BlockSpec((tm, width), lambda i: (i, 0))
    whole = pl.BlockSpec(memory_space=pltpu.VMEM)
    return pl.pallas_call(
        functools.partial(_ffn_kernel, tm=tm, tiles_per_batch=seq // tm, final=final),
        grid=grid,
        in_specs=[row_spec(D_MODEL), row_spec(ATT_WIDTH), row_spec(SGU_WIDTH),
                  _layer_spec(layer, w_out.shape[1:]), whole, _layer_spec(layer, w_up.shape[1:]),
                  _layer_spec(layer, conv_w.shape[1:]), whole, _layer_spec(layer, w_down.shape[1:]),
                  whole],
        out_specs=row_spec(D_MODEL),
        out_shape=jax.ShapeDtypeStruct((rows, D_MODEL), F32),
        scratch_shapes=[
            pltpu.VMEM((2, D_FF // FFN_FC, SUBLANES, FFN_FC), F32),
            pltpu.VMEM((tm, D_FF), BF16),
        ],
        compiler_params=pltpu.CompilerParams(
            dimension_semantics=("arbitrary",), vmem_limit_bytes=VMEM_LIMIT_BYTES),
        name="ffn",
    )(x2d, att, sgu, w_out, g2, w_up, conv_w, conv_b, w_down, final_g)


def kernel(x, norm1_g, w_in, lam_q1, lam_k1, lam_q2, lam_k2, subln_g, sgu_ln_g, sgu_ln_b,
           sgu_w, sgu_b, w_out, norm2_g, ffn_w_up, ffn_conv_w, ffn_conv_b, ffn_w_down, final_g):
    batch, seq, _ = x.shape
    rows = batch * seq
    x2d = x.reshape(rows, D_MODEL)
    slopes = jnp.asarray([_alibi_slope(h) * LOG2_E for h in range(N_HEADS)], dtype=F32)
    row1 = lambda a: a.reshape(1, -1).astype(F32)
    w_in, w_out, w_up, w_down = (w.astype(BF16) for w in (w_in, w_out, ffn_w_up, ffn_w_down))
    sgu_w, conv_w = sgu_w.astype(F32), ffn_conv_w.astype(F32)
    for l in range(DEPTH):
        lam_init = _lambda_init(l)
        sgu_b_full = jnp.broadcast_to(sgu_b[l].astype(F32)[:, :, None],
                                      (N_GROUPS, SGU_CHUNK, GROUP_DIM))
        qat, qbt, ka, kb, vt, norms, sgu = _mix_in(l, x2d, row1(norm1_g[l]), w_in,
                                                   row1(sgu_ln_g[l]), row1(sgu_ln_b[l]),
                                                   sgu_w, sgu_b_full)
        pair_q, pair_k, pair_n = _off_diagonal_pairs(norms, slopes, batch, seq)
        att = _attention(pair_q, pair_k, pair_n, slopes,
                         row1(lam_q1[l]), row1(lam_k1[l]), row1(lam_q2[l]), row1(lam_k2[l]),
                         row1(subln_g[l]), qat, qbt, ka, kb, vt,
                         batch=batch, seq=seq, lam_init=lam_init)
        x2d = _ffn(l, x2d, att, sgu, w_out, row1(norm2_g[l]), w_up, conv_w, row1(ffn_conv_b[l]),
                   w_down, row1(final_g), seq=seq, final=(l == DEPTH - 1))
    return x2d.reshape(batch, seq, D_MODEL)
```

```python
import functools
import math

import jax
import jax.numpy as jnp
import numpy as np
from jax import lax
from jax.experimental import pallas as pl
from jax.experimental.pallas import tpu as pltpu

F32 = jnp.float32
BF16 = jnp.bfloat16

D_MODEL = 1024
DEPTH = 2
N_HEADS = 4
N_MAPS = 2
HEAD_DIM = 64
HEAD_SLAB = 2 * HEAD_DIM
ATT_WIDTH = N_HEADS * HEAD_SLAB
N_GROUPS = 4
SGU_CHUNK = 128
GROUP_DIM = 128
SGU_WIDTH = N_GROUPS * GROUP_DIM
D_FF = 2816
CONV_WIDTH = 3
NORM_EPS = 1e-6
SUBLN_EPS = 1e-5
LN_EPS = 1e-5
NEG_INF = -1e30
LOG2_E = math.log2(math.e)

SUBLANES = 8
BF16_EXACT_INT = 256
UNDERFLOW_LOG2 = 150.0
NORM_SAFETY = 1.001
VMEM_LIMIT_BYTES = 56 * 1024 * 1024

MIX_TM = 1024
ATT_T = 512
ATT_UNROLL = 4
FFN_TM = 1024
FFN_FC = 256


def _gelu_tanh(x):
    c = math.sqrt(2.0 / math.pi)
    return 0.5 * x * (1.0 + jnp.tanh(c * (x + 0.044715 * (x * x * x))))


def _gelu_tanh_x2(x):
    c = math.sqrt(2.0 / math.pi)
    return x * (1.0 + jnp.tanh(x * (c + (c * 0.044715) * (x * x))))


def _alibi_slope(head):
    return 2.0 ** (-8.0 * (head + 1) / N_HEADS)


def _bf16_terms(value, n=3):
    terms = []
    rest = value
    for _ in range(n):
        term = float(np.asarray(rest, dtype=BF16).astype(np.float32))
        terms.append(term)
        rest -= term
    return terms


def _lambda_init(layer_idx):
    return 0.8 - 0.6 * math.exp(-0.3 * layer_idx)


def _layer_spec(layer, shape):
    zeros = (0,) * len(shape)
    return pl.BlockSpec((None,) + tuple(shape), lambda *_: (layer,) + zeros,
                        pipeline_mode=pl.Buffered(1))


def _alibi_operand_constants(t):
    pos = np.arange(t)
    n_terms = len(_bf16_terms(1.0))
    parts = [pos % BF16_EXACT_INT] * n_terms + [pos - pos % BF16_EXACT_INT] * n_terms
    parts = np.stack(parts, axis=1).astype(np.float32)
    first_lane = (HEAD_DIM, 0)
    k_aug = np.zeros((N_MAPS, t, HEAD_SLAB), np.float32)
    q_aug = np.zeros((N_HEADS * N_MAPS, HEAD_SLAB), np.float32)
    for mp, lane0 in enumerate(first_lane):
        k_aug[mp, :, lane0:lane0 + 2 * n_terms] = parts
        for h in range(N_HEADS):
            factor = _bf16_terms(_alibi_slope(h) * LOG2_E)
            q_aug[N_MAPS * h + mp, lane0:lane0 + 2 * n_terms] = factor + factor
    return jnp.asarray(k_aug), jnp.asarray(q_aug)


def _mixin_kernel(x_ref, g_ref, w_ref, lng_ref, lnb_ref, sw_ref, sb_ref, kaug_ref, qaug_ref,
                  qat_ref, qbt_ref, ka_ref, kb_ref, vt_ref, norm_ref, sgu_ref, *, tm, t):
    x = x_ref[...]
    ms = jnp.mean(x * x, axis=-1, keepdims=True)
    hb = (x * lax.rsqrt(ms + NORM_EPS) * g_ref[...]).astype(BF16)

    def proj(c0, width):
        return jnp.dot(hb, w_ref[:, c0:c0 + width], preferred_element_type=F32)

    zu = proj(3 * ATT_WIDTH, SGU_WIDTH)
    zg = proj(3 * ATT_WIDTH + SGU_WIDTH, SGU_WIDTH)
    zq = proj(0, ATT_WIDTH) * (HEAD_DIM ** -0.5 * LOG2_E)
    zk = proj(ATT_WIDTH, ATT_WIDTH)
    zv = proj(2 * ATT_WIDTH, ATT_WIDTH)
    in_a = lax.broadcasted_iota(jnp.int32, (tm, HEAD_SLAB), 1) < HEAD_DIM

    def max_sq_norm(slab):
        sq = slab.astype(BF16).astype(F32) ** 2
        return jnp.max(jnp.sum(sq, axis=-1, keepdims=True), axis=0, keepdims=True)

    stat_row = lax.broadcasted_iota(jnp.int32, (SUBLANES, HEAD_SLAB), 0)
    stat_lane = lax.broadcasted_iota(jnp.int32, (SUBLANES, HEAD_SLAB), 1)
    n_tiles = tm // t
    norms = [jnp.zeros((SUBLANES, HEAD_SLAB), F32)] * n_tiles
    k_aug = [jnp.concatenate([kaug_ref[mp]] * n_tiles, axis=0) for mp in range(N_MAPS)]
    for h in range(N_HEADS):
        lo, hi = h * HEAD_SLAB, (h + 1) * HEAD_SLAB
        qs, ks = zq[:, lo:hi], zk[:, lo:hi]
        ka_ref[:, lo:hi] = jnp.where(in_a, ks, k_aug[0]).astype(BF16)
        kb_ref[:, lo:hi] = jnp.where(in_a, k_aug[1], ks).astype(BF16)
        q_rows = qaug_ref[N_MAPS * h:N_MAPS * (h + 1), :]
        qat = jnp.where(in_a, qs, q_rows[0:1, :]).T.astype(BF16)
        qbt = jnp.where(in_a, q_rows[1:2, :], qs).T.astype(BF16)
        vt = zv[:, lo:hi].T.astype(BF16)
        for c in range(n_tiles):
            r0, r1 = c * t, (c + 1) * t
            norms[c] = jnp.where((stat_row == 0) & (stat_lane == h), max_sq_norm(qs[r0:r1]), norms[c])
            norms[c] = jnp.where((stat_row == 1) & (stat_lane == h), max_sq_norm(ks[r0:r1]), norms[c])
            qat_ref[c, lo:hi, :] = qat[:, r0:r1]
            qbt_ref[c, lo:hi, :] = qbt[:, r0:r1]
            vt_ref[c, lo:hi, :] = vt[:, r0:r1]
    for c in range(n_tiles):
        norm_ref[c] = norms[c]

    u = _gelu_tanh(zu)
    vg = _gelu_tanh(zg)
    row = lax.broadcasted_iota(jnp.int32, (SGU_CHUNK, SGU_CHUNK), 0)
    col = lax.broadcasted_iota(jnp.int32, (SGU_CHUNK, SGU_CHUNK), 1)
    tril = col <= row
    for g in range(N_GROUPS):
        lo, hi = g * GROUP_DIM, (g + 1) * GROUP_DIM
        y = vg[:, lo:hi]
        mu = jnp.mean(y, axis=-1, keepdims=True)
        yc = y - mu
        var = jnp.mean(yc * yc, axis=-1, keepdims=True)
        yn = (yc * lax.rsqrt(var + LN_EPS) * lng_ref[:, lo:hi] + lnb_ref[:, lo:hi]).astype(BF16)
        wm = jnp.where(tril, sw_ref[g], 0.0).astype(BF16)
        bias = sb_ref[g]
        n_chunks = tm // SGU_CHUNK
        chunks = jnp.concatenate([yn[c * SGU_CHUNK:(c + 1) * SGU_CHUNK, :]
                                  for c in range(n_chunks)], axis=1)
        vmix = jnp.dot(wm, chunks, preferred_element_type=F32)
        for c in range(n_chunks):
            r0, r1 = c * SGU_CHUNK, (c + 1) * SGU_CHUNK
            mixed = vmix[:, c * GROUP_DIM:(c + 1) * GROUP_DIM] + bias
            sgu_ref[r0:r1, lo:hi] = (u[r0:r1, lo:hi] * mixed).astype(BF16)


def _mix_in(layer, x2d, g, w_in, ln_g, ln_b, sgu_w, sgu_b_full):
    rows = x2d.shape[0]
    tm, t = MIX_TM, ATT_T
    assert tm % t == 0
    k_aug, q_aug = _alibi_operand_constants(t)
    grid = (rows // tm,)
    row_spec = lambda width: pl.BlockSpec((tm, width), lambda i: (i, 0))
    tr_spec = pl.BlockSpec((tm // t, ATT_WIDTH, t), lambda i: (i, 0, 0))
    whole = pl.BlockSpec(memory_space=pltpu.VMEM)
    row_sds = jax.ShapeDtypeStruct((rows, ATT_WIDTH), BF16)
    tr_sds = jax.ShapeDtypeStruct((rows // t, ATT_WIDTH, t), BF16)
    norm_spec = pl.BlockSpec((tm // t, SUBLANES, HEAD_SLAB), lambda i: (i, 0, 0))
    norm_sds = jax.ShapeDtypeStruct((rows // t, SUBLANES, HEAD_SLAB), F32)
    return pl.pallas_call(
        functools.partial(_mixin_kernel, tm=tm, t=t),
        grid=grid,
        in_specs=[row_spec(D_MODEL), whole, _layer_spec(layer, w_in.shape[1:]), whole, whole,
                  _layer_spec(layer, sgu_w.shape[1:]), whole, whole, whole],
        out_specs=[tr_spec, tr_spec, row_spec(ATT_WIDTH), row_spec(ATT_WIDTH), tr_spec, norm_spec,
                   row_spec(SGU_WIDTH)],
        out_shape=[tr_sds, tr_sds, row_sds, row_sds, tr_sds, norm_sds, row_sds],
        compiler_params=pltpu.CompilerParams(
            dimension_semantics=("arbitrary",), vmem_limit_bytes=VMEM_LIMIT_BYTES),
        name="mix_in",
    )(x2d, g, w_in, ln_g, ln_b, sgu_w, sgu_b_full, k_aug, q_aug)


def _score_block(qi, kj, masked, q_refs, k_refs, s_ref, bm_ref, *, t):
    k0 = pl.multiple_of(kj * t, t)
    if masked:
        key = lax.broadcasted_iota(jnp.int32, (t, t), 0)
        qry = lax.broadcasted_iota(jnp.int32, (t, t), 1)
        causal = key <= qry
    for mp in range(N_MAPS):
        st = jnp.dot(k_refs[mp][pl.ds(k0, t), :], q_refs[mp][qi], preferred_element_type=F32)
        if masked:
            st = jnp.where(causal, st, NEG_INF)
        s_ref[mp] = st
        bm_ref[mp] = jnp.max(st, axis=0, keepdims=True)


def _softmax_block(qi, c, vt_blk, s_ref, bm_ref, m_ref, l_ref, acc_ref, first):
    for mp in range(N_MAPS):
        bm = bm_ref[mp] + c
        m_new = bm if first else jnp.maximum(m_ref[qi, mp], bm)
        p = jnp.exp2(s_ref[mp] - (m_new - c))
        ps = jnp.sum(p, axis=0, keepdims=True)
        pv = jnp.dot(vt_blk, p.astype(BF16), preferred_element_type=F32)
        if first:
            l_ref[qi, mp] = ps
            acc_ref[qi, mp] = pv
        else:
            alpha = jnp.exp2(m_ref[qi, mp] - m_new)
            l_ref[qi, mp] = alpha * l_ref[qi, mp] + ps
            acc_ref[qi, mp] = alpha * acc_ref[qi, mp] + pv
        m_ref[qi, mp] = m_new


def _attn_kernel(pair_q_ref, pair_k_ref, pair_n_ref,
                 slopes_ref, lq1_ref, lk1_ref, lq2_ref, lk2_ref, sg_ref,
                 qat_ref, qbt_ref, ka_ref, kb_ref, vt_ref, o_ref,
                 m_ref, l_ref, acc_ref, sa_ref, bma_ref, sb_ref, bmb_ref, *, t, nq, lam_init):
    bh = pl.program_id(0) * N_HEADS + pl.program_id(1)
    slope = slopes_ref[pl.program_id(1)]
    n_pairs = pair_n_ref[bh]
    score = functools.partial(_score_block, q_refs=(qat_ref, qbt_ref), k_refs=(ka_ref, kb_ref), t=t)
    softmax = functools.partial(_softmax_block, m_ref=m_ref, l_ref=l_ref, acc_ref=acc_ref)
    buf_a = dict(s_ref=sa_ref, bm_ref=bma_ref)
    buf_b = dict(s_ref=sb_ref, bm_ref=bmb_ref)

    bufs = (buf_a, buf_b)

    def entry(e):
        qi, kj = pair_q_ref[bh, e], pair_k_ref[bh, e]
        c = jnp.where(e < n_pairs, ((kj - qi) * t).astype(F32) * slope, NEG_INF)
        return qi, kj, c

    def score_entry(e, buf):
        qi, kj, _ = entry(e)
        score(qi, kj, False, **buf)

    def softmax_entry(e, buf):
        qi, kj, c = entry(e)
        softmax(qi, c, vt_ref[kj], first=False, **buf)

    score(0, 0, True, **buf_a)

    def diag_group(n, carry):
        for u in range(ATT_UNROLL):
            qi = ATT_UNROLL * n + u
            if isinstance(qi, int) and qi == nq - 1:
                score_entry(0, bufs[(u + 1) % 2])
            else:
                score(qi + 1, qi + 1, True, **bufs[(u + 1) % 2])
            softmax(qi, 0.0, vt_ref[qi], first=True, **bufs[u % 2])
        return carry

    lax.fori_loop(0, nq // ATT_UNROLL - 1, diag_group, 0)
    diag_group(nq // ATT_UNROLL - 1, 0)

    def off_group(n, carry):
        for u in range(ATT_UNROLL):
            e = ATT_UNROLL * n + u
            score_entry(e + 1, bufs[(u + 1) % 2])
            softmax_entry(e, bufs[u % 2])
        return carry

    lax.fori_loop(0, (n_pairs + ATT_UNROLL - 1) // ATT_UNROLL, off_group, 0)

    lam = (jnp.exp(jnp.sum(lq1_ref[...] * lk1_ref[...], axis=-1, keepdims=True))
           - jnp.exp(jnp.sum(lq2_ref[...] * lk2_ref[...], axis=-1, keepdims=True))
           + lam_init)

    def finish(qi, carry):
        o = (acc_ref[qi, 0] * (1.0 / l_ref[qi, 0])
             - lam * (acc_ref[qi, 1] * (1.0 / l_ref[qi, 1])))
        ms = jnp.mean(o * o, axis=0, keepdims=True)
        o = o * lax.rsqrt(ms + SUBLN_EPS) * (1.0 - lam_init)
        o_ref[pl.ds(pl.multiple_of(qi * t, t), t), :] = (o.T * sg_ref[...]).astype(BF16)
        return carry

    lax.fori_loop(0, nq, finish, 0, unroll=4)


def _attention(pair_q, pair_k, pair_n, slopes, lq1, lk1, lq2, lk2, subln_g, qat, qbt, ka, kb, vt,
               *, batch, seq, lam_init):
    t = ATT_T
    nq = seq // t
    qt_spec = pl.BlockSpec((nq, HEAD_SLAB, t), lambda b, h, *_: (b, h, 0))
    row_spec = pl.BlockSpec((seq, HEAD_SLAB), lambda b, h, *_: (b, h))
    whole = pl.BlockSpec(memory_space=pltpu.VMEM)
    smem = pl.BlockSpec(memory_space=pltpu.SMEM)
    stat = pltpu.VMEM((nq, N_MAPS, 1, t), F32)
    acc = pltpu.VMEM((nq, N_MAPS, HEAD_SLAB, t), F32)
    score_buf = pltpu.VMEM((N_MAPS, t, t), F32)
    colmax_buf = pltpu.VMEM((N_MAPS, 1, t), F32)
    return pl.pallas_call(
        functools.partial(_attn_kernel, t=t, nq=nq, lam_init=lam_init),
        grid_spec=pltpu.PrefetchScalarGridSpec(
            num_scalar_prefetch=3,
            grid=(batch, N_HEADS),
            in_specs=[smem, whole, whole, whole, whole, whole,
                      qt_spec, qt_spec, row_spec, row_spec, qt_spec],
            out_specs=row_spec,
            scratch_shapes=[stat, stat, acc, score_buf, colmax_buf, score_buf, colmax_buf]),
        out_shape=jax.ShapeDtypeStruct((batch * seq, ATT_WIDTH), BF16),
        compiler_params=pltpu.CompilerParams(
            dimension_semantics=("arbitrary", "arbitrary"),
            vmem_limit_bytes=VMEM_LIMIT_BYTES),
        name="diff_attn",
    )(pair_q, pair_k, pair_n, slopes, lq1, lk1, lq2, lk2, subln_g, qat, qbt, ka, kb, vt)


def _off_diagonal_pairs(norms, slopes, batch, seq):
    nq = seq // ATT_T
    tile = np.arange(nq)
    below = tile[:, None] > tile[None, :]
    min_dist = ((tile[:, None] - tile[None, :] - 1) * ATT_T + 1).astype(np.float32)
    norms = norms.reshape(batch, nq, SUBLANES, HEAD_SLAB)
    qn = (jnp.sqrt(norms[:, :, 0, :N_HEADS]) * NORM_SAFETY).transpose(0, 2, 1)
    kn = (jnp.sqrt(norms[:, :, 1, :N_HEADS]) * NORM_SAFETY).transpose(0, 2, 1)
    bound = (qn[..., :, None] * (kn[..., None, :] + kn[..., :, None])
             - slopes[None, :, None, None] * min_dist)
    keep = below & ~(bound < -UNDERFLOW_LOG2)
    keep = keep.reshape(batch * N_HEADS, nq * nq)
    n_entries = nq * (nq - 1) // 2 + ATT_UNROLL
    order = jnp.argsort(~keep, axis=-1, stable=True)[:, :n_entries].astype(jnp.int32)
    return order // nq, order % nq, jnp.sum(keep, axis=-1, dtype=jnp.int32)


def _ffn_kernel(x_ref, att_ref, sgu_ref, wo_ref, g2_ref, wup_ref, cw_ref, cb_ref, wdn_ref, fg_ref,
                o_ref, carry_ref, act_ref, *, tm, tiles_per_batch, final):
    fc = FFN_FC
    halo = SUBLANES

    @pl.when(pl.program_id(0) % tiles_per_batch == 0)
    def _():
        carry_ref[...] = jnp.zeros_like(carry_ref)

    x1_parts, h2_parts = [], []
    for r0 in range(0, tm, tm // 2):
        r1 = r0 + tm // 2
        part = (x_ref[r0:r1, :]
                + jnp.dot(att_ref[r0:r1, :], wo_ref[0:ATT_WIDTH, :], preferred_element_type=F32)
                + jnp.dot(sgu_ref[r0:r1, :], wo_ref[ATT_WIDTH:ATT_WIDTH + SGU_WIDTH, :],
                          preferred_element_type=F32))
        ms = jnp.mean(part * part, axis=-1, keepdims=True)
        x1_parts.append(part)
        h2_parts.append((part * lax.rsqrt(ms + NORM_EPS) * g2_ref[...]).astype(BF16))
    x1 = jnp.concatenate(x1_parts, axis=0)
    h2 = jnp.concatenate(h2_parts, axis=0)

    rows8 = lax.broadcasted_iota(jnp.int32, (halo, fc), 0)
    for c in range(D_FF // fc):
        convs = []
        for half in range(2):
            col0 = half * D_FF + c * fc
            up = jnp.dot(h2, wup_ref[:, col0:col0 + fc], preferred_element_type=F32)
            prev = carry_ref[half, c]
            carry_ref[half, c] = up[tm - halo:tm, :]
            scale = 0.5 if half == 1 else 1.0
            w0 = cw_ref[0:1, col0:col0 + fc] * scale
            w1 = cw_ref[1:2, col0:col0 + fc] * scale
            w2 = cw_ref[2:3, col0:col0 + fc] * scale
            bias = cb_ref[:, col0:col0 + fc] * scale

            def conv(cur, back1, back2):
                return bias + w0 * back2 + w1 * back1 + w2 * cur

            body = conv(up, pltpu.roll(up, 1, axis=0), pltpu.roll(up, 2, axis=0))
            head = up[0:halo, :]
            head1 = jnp.where(rows8 < 1, pltpu.roll(prev, 1, axis=0), pltpu.roll(head, 1, axis=0))
            head2 = jnp.where(rows8 < 2, pltpu.roll(prev, 2, axis=0), pltpu.roll(head, 2, axis=0))
            convs.append(jnp.concatenate([conv(head, head1, head2), body[halo:, :]], axis=0))
        act_ref[:, c * fc:(c + 1) * fc] = (_gelu_tanh_x2(convs[0]) * convs[1]).astype(BF16)

    out = x1 + jnp.dot(act_ref[...], wdn_ref[...], preferred_element_type=F32)
    if final:
        ms = jnp.mean(out * out, axis=-1, keepdims=True)
        out = out * lax.rsqrt(ms + NORM_EPS) * fg_ref[...]
    o_ref[...] = out


def _ffn(layer, x2d, att, sgu, w_out, g2, w_up, conv_w, conv_b, w_down, final_g, *, seq, final):
    rows = x2d.shape[0]
    tm = FFN_TM
    grid = (rows // tm,)
    row_spec = lambda width: pl.BlockSpec((tm, width), lambda i: (i, 0))
    whole = pl.BlockSpec(memory_space=pltpu.VMEM)
    return pl.pallas_call(
        functools.partial(_ffn_kernel, tm=tm, tiles_per_batch=seq // tm, final=final),
        grid=grid,
        in_specs=[row_spec(D_MODEL), row_spec(ATT_WIDTH), row_spec(SGU_WIDTH),
                  _layer_spec(layer, w_out.shape[1:]), whole, _layer_spec(layer, w_up.shape[1:]),
                  _layer_spec(layer, conv_w.shape[1:]), whole, _layer_spec(layer, w_down.shape[1:]),
                  whole],
        out_specs=row_spec(D_MODEL),
        out_shape=jax.ShapeDtypeStruct((rows, D_MODEL), F32),
        scratch_shapes=[
            pltpu.VMEM((2, D_FF // FFN_FC, SUBLANES, FFN_FC), F32),
            pltpu.VMEM((tm, D_FF), BF16),
        ],
        compiler_params=pltpu.CompilerParams(
            dimension_semantics=("arbitrary",), vmem_limit_bytes=VMEM_LIMIT_BYTES),
        name="ffn",
    )(x2d, att, sgu, w_out, g2, w_up, conv_w, conv_b, w_down, final_g)


def kernel(x, norm1_g, w_in, lam_q1, lam_k1, lam_q2, lam_k2, subln_g, sgu_ln_g, sgu_ln_b,
           sgu_w, sgu_b, w_out, norm2_g, ffn_w_up, ffn_conv_w, ffn_conv_b, ffn_w_down, final_g):
    batch, seq, _ = x.shape
    rows = batch * seq
    x2d = x.reshape(rows, D_MODEL)
    slopes = jnp.asarray([_alibi_slope(h) * LOG2_E for h in range(N_HEADS)], dtype=F32)
    row1 = lambda a: a.reshape(1, -1).astype(F32)
    w_in, w_out, w_up, w_down = (w.astype(BF16) for w in (w_in, w_out, ffn_w_up, ffn_w_down))
    sgu_w, conv_w = sgu_w.astype(F32), ffn_conv_w.astype(F32)
    for l in range(DEPTH):
        lam_init = _lambda_init(l)
        sgu_b_full = jnp.broadcast_to(sgu_b[l].astype(F32)[:, :, None],
                                      (N_GROUPS, SGU_CHUNK, GROUP_DIM))
        qat, qbt, ka, kb, vt, norms, sgu = _mix_in(l, x2d, row1(norm1_g[l]), w_in,
                                                   row1(sgu_ln_g[l]), row1(sgu_ln_b[l]),
                                                   sgu_w, sgu_b_full)
        pair_q, pair_k, pair_n = _off_diagonal_pairs(norms, slopes, batch, seq)
        att = _attention(pair_q, pair_k, pair_n, slopes,
                         row1(lam_q1[l]), row1(lam_k1[l]), row1(lam_q2[l]), row1(lam_k2[l]),
                         row1(subln_g[l]), qat, qbt, ka, kb, vt,
                         batch=batch, seq=seq, lam_init=lam_init)
        x2d = _ffn(l, x2d, att, sgu, w_out, row1(norm2_g[l]), w_up, conv_w, row1(ffn_conv_b[l]),
                   w_down, row1(final_g), seq=seq, final=(l == DEPTH - 1))
    return x2d.reshape(batch, seq, D_MODEL)
```

```python
import functools
import math

import jax
import jax.numpy as jnp
import numpy as np
from jax import lax
from jax.experimental import pallas as pl
from jax.experimental.pallas import tpu as pltpu

F32 = jnp.float32
BF16 = jnp.bfloat16

D_MODEL = 1024
DEPTH = 2
N_HEADS = 4
N_MAPS = 2
HEAD_DIM = 64
HEAD_SLAB = 2 * HEAD_DIM
BF16_ROWS = 16
V_ROWS = HEAD_SLAB + BF16_ROWS
ATT_WIDTH = N_HEADS * HEAD_SLAB
N_GROUPS = 4
SGU_CHUNK = 128
GROUP_DIM = 128
SGU_WIDTH = N_GROUPS * GROUP_DIM
D_FF = 2816
CONV_WIDTH = 3
NORM_EPS = 1e-6
SUBLN_EPS = 1e-5
LN_EPS = 1e-5
NEG_INF = -1e30
LOG2_E = math.log2(math.e)

SUBLANES = 8
BF16_EXACT_INT = 256
UNDERFLOW_LOG2 = 150.0
NORM_SAFETY = 1.001
VMEM_LIMIT_BYTES = 56 * 1024 * 1024

MIX_TM = 1024
ATT_T = 512
ATT_UNROLL = 4
FFN_TM = 1024
FFN_FC = 256


def _gelu_tanh(x):
    c = math.sqrt(2.0 / math.pi)
    return 0.5 * x * (1.0 + jnp.tanh(c * (x + 0.044715 * (x * x * x))))


def _gelu_tanh_x2(x):
    c = math.sqrt(2.0 / math.pi)
    return x * (1.0 + jnp.tanh(x * (c + (c * 0.044715) * (x * x))))


def _alibi_slope(head):
    return 2.0 ** (-8.0 * (head + 1) / N_HEADS)


def _bf16_terms(value, n=3):
    terms = []
    rest = value
    for _ in range(n):
        term = float(np.asarray(rest, dtype=BF16).astype(np.float32))
        terms.append(term)
        rest -= term
    return terms


def _lambda_init(layer_idx):
    return 0.8 - 0.6 * math.exp(-0.3 * layer_idx)


def _layer_spec(layer, shape):
    zeros = (0,) * len(shape)
    return pl.BlockSpec((None,) + tuple(shape), lambda *_: (layer,) + zeros,
                        pipeline_mode=pl.Buffered(1))


def _alibi_operand_constants(t):
    pos = np.arange(t)
    n_terms = len(_bf16_terms(1.0))
    parts = [pos % BF16_EXACT_INT] * n_terms + [pos - pos % BF16_EXACT_INT] * n_terms
    parts = np.stack(parts, axis=1).astype(np.float32)
    first_lane = (HEAD_DIM, 0)
    k_aug = np.zeros((N_MAPS, t, HEAD_SLAB), np.float32)
    q_aug = np.zeros((N_HEADS * N_MAPS, HEAD_SLAB), np.float32)
    for mp, lane0 in enumerate(first_lane):
        k_aug[mp, :, lane0:lane0 + 2 * n_terms] = parts
        for h in range(N_HEADS):
            factor = _bf16_terms(_alibi_slope(h) * LOG2_E)
            q_aug[N_MAPS * h + mp, lane0:lane0 + 2 * n_terms] = factor + factor
    return jnp.asarray(k_aug), jnp.asarray(q_aug)


def _mixin_kernel(x_ref, g_ref, w_ref, lng_ref, lnb_ref, sw_ref, sb_ref, kaug_ref, qaug_ref,
                  qat_ref, qbt_ref, ka_ref, kb_ref, vt_ref, norm_ref, sgu_ref, *, tm, t):
    x = x_ref[...]
    ms = jnp.mean(x * x, axis=-1, keepdims=True)
    hb = (x * lax.rsqrt(ms + NORM_EPS) * g_ref[...]).astype(BF16)

    def proj(c0, width):
        return jnp.dot(hb, w_ref[:, c0:c0 + width], preferred_element_type=F32)

    zu = proj(3 * ATT_WIDTH, SGU_WIDTH)
    zg = proj(3 * ATT_WIDTH + SGU_WIDTH, SGU_WIDTH)
    zq = proj(0, ATT_WIDTH) * (HEAD_DIM ** -0.5 * LOG2_E)
    zk = proj(ATT_WIDTH, ATT_WIDTH)
    zv = proj(2 * ATT_WIDTH, ATT_WIDTH)
    in_a = lax.broadcasted_iota(jnp.int32, (tm, HEAD_SLAB), 1) < HEAD_DIM

    def max_sq_norm(slab):
        sq = slab.astype(BF16).astype(F32) ** 2
        return jnp.max(jnp.sum(sq, axis=-1, keepdims=True), axis=0, keepdims=True)

    stat_row = lax.broadcasted_iota(jnp.int32, (SUBLANES, HEAD_SLAB), 0)
    stat_lane = lax.broadcasted_iota(jnp.int32, (SUBLANES, HEAD_SLAB), 1)
    n_tiles = tm // t
    norms = [jnp.zeros((SUBLANES, HEAD_SLAB), F32)] * n_tiles
    k_aug = [jnp.concatenate([kaug_ref[mp]] * n_tiles, axis=0) for mp in range(N_MAPS)]
    for h in range(N_HEADS):
        lo, hi = h * HEAD_SLAB, (h + 1) * HEAD_SLAB
        qs, ks = zq[:, lo:hi], zk[:, lo:hi]
        ka_ref[:, lo:hi] = jnp.where(in_a, ks, k_aug[0]).astype(BF16)
        kb_ref[:, lo:hi] = jnp.where(in_a, k_aug[1], ks).astype(BF16)
        q_rows = qaug_ref[N_MAPS * h:N_MAPS * (h + 1), :]
        qat = jnp.where(in_a, qs, q_rows[0:1, :]).T.astype(BF16)
        qbt = jnp.where(in_a, q_rows[1:2, :], qs).T.astype(BF16)
        vt = zv[:, lo:hi].T.astype(BF16)
        for c in range(n_tiles):
            r0, r1 = c * t, (c + 1) * t
            norms[c] = jnp.where((stat_row == 0) & (stat_lane == h), max_sq_norm(qs[r0:r1]), norms[c])
            norms[c] = jnp.where((stat_row == 1) & (stat_lane == h), max_sq_norm(ks[r0:r1]), norms[c])
            qat_ref[c, lo:hi, :] = qat[:, r0:r1]
            qbt_ref[c, lo:hi, :] = qbt[:, r0:r1]
            vt_ref[c, h * V_ROWS:h * V_ROWS + HEAD_SLAB, :] = vt[:, r0:r1]
            vt_ref[c, h * V_ROWS + HEAD_SLAB:(h + 1) * V_ROWS, :] = jnp.ones((BF16_ROWS, t), BF16)
    for c in range(n_tiles):
        norm_ref[c] = norms[c]

    u = _gelu_tanh(zu)
    vg = _gelu_tanh(zg)
    row = lax.broadcasted_iota(jnp.int32, (SGU_CHUNK, SGU_CHUNK), 0)
    col = lax.broadcasted_iota(jnp.int32, (SGU_CHUNK, SGU_CHUNK), 1)
    tril = col <= row
    for g in range(N_GROUPS):
        lo, hi = g * GROUP_DIM, (g + 1) * GROUP_DIM
        y = vg[:, lo:hi]
        mu = jnp.mean(y, axis=-1, keepdims=True)
        yc = y - mu
        var = jnp.mean(yc * yc, axis=-1, keepdims=True)
        yn = (yc * lax.rsqrt(var + LN_EPS) * lng_ref[:, lo:hi] + lnb_ref[:, lo:hi]).astype(BF16)
        wm = jnp.where(tril, sw_ref[g], 0.0).astype(BF16)
        bias = sb_ref[g]
        n_chunks = tm // SGU_CHUNK
        chunks = jnp.concatenate([yn[c * SGU_CHUNK:(c + 1) * SGU_CHUNK, :]
                                  for c in range(n_chunks)], axis=1)
        vmix = jnp.dot(wm, chunks, preferred_element_type=F32)
        for c in range(n_chunks):
            r0, r1 = c * SGU_CHUNK, (c + 1) * SGU_CHUNK
            mixed = vmix[:, c * GROUP_DIM:(c + 1) * GROUP_DIM] + bias
            sgu_ref[r0:r1, lo:hi] = (u[r0:r1, lo:hi] * mixed).astype(BF16)


def _mix_in(layer, x2d, g, w_in, ln_g, ln_b, sgu_w, sgu_b_full):
    rows = x2d.shape[0]
    tm, t = MIX_TM, ATT_T
    assert tm % t == 0
    k_aug, q_aug = _alibi_operand_constants(t)
    grid = (rows // tm,)
    row_spec = lambda width: pl.BlockSpec((tm, width), lambda i: (i, 0))
    tr_spec = pl.BlockSpec((tm // t, ATT_WIDTH, t), lambda i: (i, 0, 0))
    whole = pl.BlockSpec(memory_space=pltpu.VMEM)
    row_sds = jax.ShapeDtypeStruct((rows, ATT_WIDTH), BF16)
    tr_sds = jax.ShapeDtypeStruct((rows // t, ATT_WIDTH, t), BF16)
    vt_spec = pl.BlockSpec((tm // t, N_HEADS * V_ROWS, t), lambda i: (i, 0, 0))
    vt_sds = jax.ShapeDtypeStruct((rows // t, N_HEADS * V_ROWS, t), BF16)
    norm_spec = pl.BlockSpec((tm // t, SUBLANES, HEAD_SLAB), lambda i: (i, 0, 0))
    norm_sds = jax.ShapeDtypeStruct((rows // t, SUBLANES, HEAD_SLAB), F32)
    return pl.pallas_call(
        functools.partial(_mixin_kernel, tm=tm, t=t),
        grid=grid,
        in_specs=[row_spec(D_MODEL), whole, _layer_spec(layer, w_in.shape[1:]), whole, whole,
                  _layer_spec(layer, sgu_w.shape[1:]), whole, whole, whole],
        out_specs=[tr_spec, tr_spec, row_spec(ATT_WIDTH), row_spec(ATT_WIDTH), vt_spec, norm_spec,
                   row_spec(SGU_WIDTH)],
        out_shape=[tr_sds, tr_sds, row_sds, row_sds, vt_sds, norm_sds, row_sds],
        compiler_params=pltpu.CompilerParams(
            dimension_semantics=("arbitrary",), vmem_limit_bytes=VMEM_LIMIT_BYTES),
        name="mix_in",
    )(x2d, g, w_in, ln_g, ln_b, sgu_w, sgu_b_full, k_aug, q_aug)


def _score_block(qi, kj, masked, q_refs, k_refs, s_ref, bm_ref, *, t):
    k0 = pl.multiple_of(kj * t, t)
    if masked:
        key = lax.broadcasted_iota(jnp.int32, (t, t), 0)
        qry = lax.broadcasted_iota(jnp.int32, (t, t), 1)
        causal = key <= qry
    for mp in range(N_MAPS):
        st = jnp.dot(k_refs[mp][pl.ds(k0, t), :], q_refs[mp][qi], preferred_element_type=F32)
        if masked:
            st = jnp.where(causal, st, NEG_INF)
        s_ref[mp] = st
        bm_ref[mp] = jnp.max(st, axis=0, keepdims=True)


def _softmax_block(qi, c, vt_blk, s_ref, bm_ref, m_ref, l_ref, acc_ref, first):
    for mp in range(N_MAPS):
        bm = bm_ref[mp] + c
        m_new = bm if first else jnp.maximum(m_ref[qi, mp], bm)
        p = jnp.exp2(s_ref[mp] - (m_new - c))
        pv_ps = jnp.dot(vt_blk, p.astype(BF16), preferred_element_type=F32)
        pv, ps = pv_ps[0:HEAD_SLAB, :], pv_ps[HEAD_SLAB:HEAD_SLAB + 1, :]
        if first:
            l_ref[qi, mp] = ps
            acc_ref[qi, mp] = pv
        else:
            alpha = jnp.exp2(m_ref[qi, mp] - m_new)
            l_ref[qi, mp] = alpha * l_ref[qi, mp] + ps
            acc_ref[qi, mp] = alpha * acc_ref[qi, mp] + pv
        m_ref[qi, mp] = m_new


def _attn_kernel(pair_q_ref, pair_k_ref, pair_n_ref,
                 slopes_ref, lq1_ref, lk1_ref, lq2_ref, lk2_ref, sg_ref,
                 qat_ref, qbt_ref, ka_ref, kb_ref, vt_ref, o_ref,
                 m_ref, l_ref, acc_ref, sa_ref, bma_ref, sb_ref, bmb_ref, *, t, nq, lam_init):
    bh = pl.program_id(0) * N_HEADS + pl.program_id(1)
    slope = slopes_ref[pl.program_id(1)]
    n_pairs = pair_n_ref[bh]
    score = functools.partial(_score_block, q_refs=(qat_ref, qbt_ref), k_refs=(ka_ref, kb_ref), t=t)
    softmax = functools.partial(_softmax_block, m_ref=m_ref, l_ref=l_ref, acc_ref=acc_ref)
    buf_a = dict(s_ref=sa_ref, bm_ref=bma_ref)
    buf_b = dict(s_ref=sb_ref, bm_ref=bmb_ref)

    bufs = (buf_a, buf_b)

    def entry(e):
        qi, kj = pair_q_ref[bh, e], pair_k_ref[bh, e]
        c = jnp.where(e < n_pairs, ((kj - qi) * t).astype(F32) * slope, NEG_INF)
        return qi, kj, c

    def score_entry(e, buf):
        qi, kj, _ = entry(e)
        score(qi, kj, False, **buf)

    def softmax_entry(e, buf):
        qi, kj, c = entry(e)
        softmax(qi, c, vt_ref[kj], first=False, **buf)

    score(0, 0, True, **buf_a)

    def diag_group(n, carry):
        for u in range(ATT_UNROLL):
            qi = ATT_UNROLL * n + u
            if isinstance(qi, int) and qi == nq - 1:
                score_entry(0, bufs[(u + 1) % 2])
            else:
                score(qi + 1, qi + 1, True, **bufs[(u + 1) % 2])
            softmax(qi, 0.0, vt_ref[qi], first=True, **bufs[u % 2])
        return carry

    lax.fori_loop(0, nq // ATT_UNROLL - 1, diag_group, 0)
    diag_group(nq // ATT_UNROLL - 1, 0)

    def off_group(n, carry):
        for u in range(ATT_UNROLL):
            e = ATT_UNROLL * n + u
            score_entry(e + 1, bufs[(u + 1) % 2])
            softmax_entry(e, bufs[u % 2])
        return carry

    lax.fori_loop(0, (n_pairs + ATT_UNROLL - 1) // ATT_UNROLL, off_group, 0)

    lam = (jnp.exp(jnp.sum(lq1_ref[...] * lk1_ref[...], axis=-1, keepdims=True))
           - jnp.exp(jnp.sum(lq2_ref[...] * lk2_ref[...], axis=-1, keepdims=True))
           + lam_init)

    def finish(qi, carry):
        o = (acc_ref[qi, 0] * (1.0 / l_ref[qi, 0])
             - lam * (acc_ref[qi, 1] * (1.0 / l_ref[qi, 1])))
        ms = jnp.mean(o * o, axis=0, keepdims=True)
        o = o * lax.rsqrt(ms + SUBLN_EPS) * (1.0 - lam_init)
        o_ref[pl.ds(pl.multiple_of(qi * t, t), t), :] = (o.T * sg_ref[...]).astype(BF16)
        return carry

    lax.fori_loop(0, nq, finish, 0, unroll=4)


def _attention(pair_q, pair_k, pair_n, slopes, lq1, lk1, lq2, lk2, subln_g, qat, qbt, ka, kb, vt,
               *, batch, seq, lam_init):
    t = ATT_T
    nq = seq // t
    qt_spec = pl.BlockSpec((nq, HEAD_SLAB, t), lambda b, h, *_: (b, h, 0))
    vt_spec = pl.BlockSpec((nq, V_ROWS, t), lambda b, h, *_: (b, h, 0))
    row_spec = pl.BlockSpec((seq, HEAD_SLAB), lambda b, h, *_: (b, h))
    whole = pl.BlockSpec(memory_space=pltpu.VMEM)
    smem = pl.BlockSpec(memory_space=pltpu.SMEM)
    stat = pltpu.VMEM((nq, N_MAPS, 1, t), F32)
    acc = pltpu.VMEM((nq, N_MAPS, HEAD_SLAB, t), F32)
    score_buf = pltpu.VMEM((N_MAPS, t, t), F32)
    colmax_buf = pltpu.VMEM((N_MAPS, 1, t), F32)
    return pl.pallas_call(
        functools.partial(_attn_kernel, t=t, nq=nq, lam_init=lam_init),
        grid_spec=pltpu.PrefetchScalarGridSpec(
            num_scalar_prefetch=3,
            grid=(batch, N_HEADS),
            in_specs=[smem, whole, whole, whole, whole, whole,
                      qt_spec, qt_spec, row_spec, row_spec, vt_spec],
            out_specs=row_spec,
            scratch_shapes=[stat, stat, acc, score_buf, colmax_buf, score_buf, colmax_buf]),
        out_shape=jax.ShapeDtypeStruct((batch * seq, ATT_WIDTH), BF16),
        compiler_params=pltpu.CompilerParams(
            dimension_semantics=("arbitrary", "arbitrary"),
            vmem_limit_bytes=VMEM_LIMIT_BYTES),
        name="diff_attn",
    )(pair_q, pair_k, pair_n, slopes, lq1, lk1, lq2, lk2, subln_g, qat, qbt, ka, kb, vt)


def _off_diagonal_pairs(norms, slopes, batch, seq):
    nq = seq // ATT_T
    tile = np.arange(nq)
    below = tile[:, None] > tile[None, :]
    min_dist = ((tile[:, None] - tile[None, :] - 1) * ATT_T + 1).astype(np.float32)
    norms = norms.reshape(batch, nq, SUBLANES, HEAD_SLAB)
    qn = (jnp.sqrt(norms[:, :, 0, :N_HEADS]) * NORM_SAFETY).transpose(0, 2, 1)
    kn = (jnp.sqrt(norms[:, :, 1, :N_HEADS]) * NORM_SAFETY).transpose(0, 2, 1)
    bound = (qn[..., :, None] * (kn[..., None, :] + kn[..., :, None])
             - slopes[None, :, None, None] * min_dist)
    keep = below & ~(bound < -UNDERFLOW_LOG2)
    keep = keep.reshape(batch * N_HEADS, nq * nq)
    n_entries = nq * (nq - 1) // 2 + ATT_UNROLL
    order = jnp.argsort(~keep, axis=-1, stable=True)[:, :n_entries].astype(jnp.int32)
    return order // nq, order % nq, jnp.sum(keep, axis=-1, dtype=jnp.int32)


def _ffn_kernel(x_ref, att_ref, sgu_ref, wo_ref, g2_ref, wup_ref, cw_ref, cb_ref, wdn_ref, fg_ref,
                o_ref, carry_ref, act_ref, *, tm, tiles_per_batch, final):
    fc = FFN_FC
    halo = SUBLANES

    @pl.when(pl.program_id(0) % tiles_per_batch == 0)
    def _():
        carry_ref[...] = jnp.zeros_like(carry_ref)

    x1_parts, h2_parts = [], []
    for r0 in range(0, tm, tm // 2):
        r1 = r0 + tm // 2
        part = (x_ref[r0:r1, :]
                + jnp.dot(att_ref[r0:r1, :], wo_ref[0:ATT_WIDTH, :], preferred_element_type=F32)
                + jnp.dot(sgu_ref[r0:r1, :], wo_ref[ATT_WIDTH:ATT_WIDTH + SGU_WIDTH, :],
                          preferred_element_type=F32))
        ms = jnp.mean(part * part, axis=-1, keepdims=True)
        x1_parts.append(part)
        h2_parts.append((part * lax.rsqrt(ms + NORM_EPS) * g2_ref[...]).astype(BF16))
    x1 = jnp.concatenate(x1_parts, axis=0)
    h2 = jnp.concatenate(h2_parts, axis=0)

    rows8 = lax.broadcasted_iota(jnp.int32, (halo, fc), 0)
    for c in range(D_FF // fc):
        convs = []
        for half in range(2):
            col0 = half * D_FF + c * fc
            up = jnp.dot(h2, wup_ref[:, col0:col0 + fc], preferred_element_type=F32)
            prev = carry_ref[half, c]
            carry_ref[half, c] = up[tm - halo:tm, :]
            scale = 0.5 if half == 1 else 1.0
            w0 = cw_ref[0:1, col0:col0 + fc] * scale
            w1 = cw_ref[1:2, col0:col0 + fc] * scale
            w2 = cw_ref[2:3, col0:col0 + fc] * scale
            bias = cb_ref[:, col0:col0 + fc] * scale

            def conv(cur, back1, back2):
                return bias + w0 * back2 + w1 * back1 + w2 * cur

            body = conv(up, pltpu.roll(up, 1, axis=0), pltpu.roll(up, 2, axis=0))
            head = up[0:halo, :]
            head1 = jnp.where(rows8 < 1, pltpu.roll(prev, 1, axis=0), pltpu.roll(head, 1, axis=0))
            head2 = jnp.where(rows8 < 2, pltpu.roll(prev, 2, axis=0), pltpu.roll(head, 2, axis=0))
            convs.append(jnp.concatenate([conv(head, head1, head2), body[halo:, :]], axis=0))
        act_ref[:, c * fc:(c + 1) * fc] = (_gelu_tanh_x2(convs[0]) * convs[1]).astype(BF16)

    out = x1 + jnp.dot(act_ref[...], wdn_ref[...], preferred_element_type=F32)
    if final:
        ms = jnp.mean(out * out, axis=-1, keepdims=True)
        out = out * lax.rsqrt(ms + NORM_EPS) * fg_ref[...]
    o_ref[...] = out


def _ffn(layer, x2d, att, sgu, w_out, g2, w_up, conv_w, conv_b, w_down, final_g, *, seq, final):
    rows = x2d.shape[0]
    tm = FFN_TM
    grid = (rows // tm,)
    row_spec = lambda width: pl.BlockSpec((tm, width), lambda i: (i, 0))
    whole = pl.BlockSpec(memory_space=pltpu.VMEM)
    return pl.pallas_call(
        functools.partial(_ffn_kernel, tm=tm, tiles_per_batch=seq // tm, final=final),
        grid=grid,
        in_specs=[row_spec(D_MODEL), row_spec(ATT_WIDTH), row_spec(SGU_WIDTH),
                  _layer_spec(layer, w_out.shape[1:]), whole, _layer_spec(layer, w_up.shape[1:]),
                  _layer_spec(layer, conv_w.shape[1:]), whole, _layer_spec(layer, w_down.shape[1:]),
                  whole],
        out_specs=row_spec(D_MODEL),
        out_shape=jax.ShapeDtypeStruct((rows, D_MODEL), F32),
        scratch_shapes=[
            pltpu.VMEM((2, D_FF // FFN_FC, SUBLANES, FFN_FC), F32),
            pltpu.VMEM((tm, D_FF), BF16),
        ],
        compiler_params=pltpu.CompilerParams(
            dimension_semantics=("arbitrary",), vmem_limit_bytes=VMEM_LIMIT_BYTES),
        name="ffn",
    )(x2d, att, sgu, w_out, g2, w_up, conv_w, conv_b, w_down, final_g)


def kernel(x, norm1_g, w_in, lam_q1, lam_k1, lam_q2, lam_k2, subln_g, sgu_ln_g, sgu_ln_b,
           sgu_w, sgu_b, w_out, norm2_g, ffn_w_up, ffn_conv_w, ffn_conv_b, ffn_w_down, final_g):
    batch, seq, _ = x.shape
    rows = batch * seq
    x2d = x.reshape(rows, D_MODEL)
    slopes = jnp.asarray([_alibi_slope(h) * LOG2_E for h in range(N_HEADS)], dtype=F32)
    row1 = lambda a: a.reshape(1, -1).astype(F32)
    w_in, w_out, w_up, w_down = (w.astype(BF16) for w in (w_in, w_out, ffn_w_up, ffn_w_down))
    sgu_w, conv_w = sgu_w.astype(F32), ffn_conv_w.astype(F32)
    for l in range(DEPTH):
        lam_init = _lambda_init(l)
        sgu_b_full = jnp.broadcast_to(sgu_b[l].astype(F32)[:, :, None],
                                      (N_GROUPS, SGU_CHUNK, GROUP_DIM))
        qat, qbt, ka, kb, vt, norms, sgu = _mix_in(l, x2d, row1(norm1_g[l]), w_in,
                                                   row1(sgu_ln_g[l]), row1(sgu_ln_b[l]),
                                                   sgu_w, sgu_b_full)
        pair_q, pair_k, pair_n = _off_diagonal_pairs(norms, slopes, batch, seq)
        att = _attention(pair_q, pair_k, pair_n, slopes,
                         row1(lam_q1[l]), row1(lam_k1[l]), row1(lam_q2[l]), row1(lam_k2[l]),
                         row1(subln_g[l]), qat, qbt, ka, kb, vt,
                         batch=batch, seq=seq, lam_init=lam_init)
        x2d = _ffn(l, x2d, att, sgu, w_out, row1(norm2_g[l]), w_up, conv_w, row1(ffn_conv_b[l]),
                   w_down, row1(final_g), seq=seq, final=(l == DEPTH - 1))
    return x2d.reshape(batch, seq, D_MODEL)
```

```python
import functools
import math

import jax
import jax.numpy as jnp
import numpy as np
from jax import lax
from jax.experimental import pallas as pl
from jax.experimental.pallas import tpu as pltpu

F32 = jnp.float32
BF16 = jnp.bfloat16

D_MODEL = 1024
DEPTH = 2
N_HEADS = 4
N_MAPS = 2
HEAD_DIM = 64
HEAD_SLAB = 2 * HEAD_DIM
BF16_ROWS = 16
V_ROWS = HEAD_SLAB + BF16_ROWS
ATT_WIDTH = N_HEADS * HEAD_SLAB
N_GROUPS = 4
SGU_CHUNK = 128
GROUP_DIM = 128
SGU_WIDTH = N_GROUPS * GROUP_DIM
D_FF = 2816
CONV_WIDTH = 3
NORM_EPS = 1e-6
SUBLN_EPS = 1e-5
LN_EPS = 1e-5
NEG_INF = -1e30
LOG2_E = math.log2(math.e)

SUBLANES = 8
BF16_EXACT_INT = 256
UNDERFLOW_LOG2 = 150.0
NORM_SAFETY = 1.001
VMEM_LIMIT_BYTES = 56 * 1024 * 1024

MIX_TM = 1024
ATT_T = 512
ATT_UNROLL = 4
FFN_TM = 1024
FFN_FC = 256


def _gelu_tanh(x):
    c = math.sqrt(2.0 / math.pi)
    return 0.5 * x * (1.0 + jnp.tanh(c * (x + 0.044715 * (x * x * x))))


def _gelu_tanh_x2(x):
    c = math.sqrt(2.0 / math.pi)
    return x * (1.0 + jnp.tanh(x * (c + (c * 0.044715) * (x * x))))


def _alibi_slope(head):
    return 2.0 ** (-8.0 * (head + 1) / N_HEADS)


def _bf16_terms(value, n=3):
    terms = []
    rest = value
    for _ in range(n):
        term = float(np.asarray(rest, dtype=BF16).astype(np.float32))
        terms.append(term)
        rest -= term
    return terms


def _lambda_init(layer_idx):
    return 0.8 - 0.6 * math.exp(-0.3 * layer_idx)


def _layer_spec(layer, shape):
    zeros = (0,) * len(shape)
    return pl.BlockSpec((None,) + tuple(shape), lambda *_: (layer,) + zeros,
                        pipeline_mode=pl.Buffered(1))


def _alibi_operand_constants(t):
    pos = np.arange(t)
    n_terms = len(_bf16_terms(1.0))
    parts = [pos % BF16_EXACT_INT] * n_terms + [pos - pos % BF16_EXACT_INT] * n_terms
    parts = np.stack(parts, axis=1).astype(np.float32)
    first_lane = (HEAD_DIM, 0)
    k_aug = np.zeros((N_MAPS, t, HEAD_SLAB), np.float32)
    q_aug = np.zeros((N_HEADS * N_MAPS, HEAD_SLAB), np.float32)
    for mp, lane0 in enumerate(first_lane):
        k_aug[mp, :, lane0:lane0 + 2 * n_terms] = parts
        for h in range(N_HEADS):
            factor = _bf16_terms(_alibi_slope(h) * LOG2_E)
            q_aug[N_MAPS * h + mp, lane0:lane0 + 2 * n_terms] = factor + factor
    return jnp.asarray(k_aug), jnp.asarray(q_aug)


def _mixin_kernel(x_ref, g_ref, w_ref, lng_ref, lnb_ref, sw_ref, sb_ref, kaug_ref, qaug_ref,
                  qat_ref, qbt_ref, ka_ref, kb_ref, vt_ref, norm_ref, sgu_ref, *, tm, t):
    x = x_ref[...]
    ms = jnp.mean(x * x, axis=-1, keepdims=True)
    hb = (x * lax.rsqrt(ms + NORM_EPS) * g_ref[...]).astype(BF16)

    def proj(c0, width):
        return jnp.dot(hb, w_ref[:, c0:c0 + width], preferred_element_type=F32)

    zu = proj(3 * ATT_WIDTH, SGU_WIDTH)
    zg = proj(3 * ATT_WIDTH + SGU_WIDTH, SGU_WIDTH)
    zq = proj(0, ATT_WIDTH) * (HEAD_DIM ** -0.5 * LOG2_E)
    zk = proj(ATT_WIDTH, ATT_WIDTH)
    zv = proj(2 * ATT_WIDTH, ATT_WIDTH)
    in_a = lax.broadcasted_iota(jnp.int32, (tm, HEAD_SLAB), 1) < HEAD_DIM

    def max_sq_norm(slab):
        sq = slab.astype(BF16).astype(F32) ** 2
        return jnp.max(jnp.sum(sq, axis=-1, keepdims=True), axis=0, keepdims=True)

    stat_row = lax.broadcasted_iota(jnp.int32, (SUBLANES, HEAD_SLAB), 0)
    stat_lane = lax.broadcasted_iota(jnp.int32, (SUBLANES, HEAD_SLAB), 1)
    n_tiles = tm // t
    norms = [jnp.zeros((SUBLANES, HEAD_SLAB), F32)] * n_tiles
    k_aug = [jnp.concatenate([kaug_ref[mp]] * n_tiles, axis=0) for mp in range(N_MAPS)]
    for h in range(N_HEADS):
        lo, hi = h * HEAD_SLAB, (h + 1) * HEAD_SLAB
        qs, ks = zq[:, lo:hi], zk[:, lo:hi]
        ka_ref[:, lo:hi] = jnp.where(in_a, ks, k_aug[0]).astype(BF16)
        kb_ref[:, lo:hi] = jnp.where(in_a, k_aug[1], ks).astype(BF16)
        q_rows = qaug_ref[N_MAPS * h:N_MAPS * (h + 1), :]
        qat = jnp.where(in_a, qs, q_rows[0:1, :]).T.astype(BF16)
        qbt = jnp.where(in_a, q_rows[1:2, :], qs).T.astype(BF16)
        vt = zv[:, lo:hi].T.astype(BF16)
        for c in range(n_tiles):
            r0, r1 = c * t, (c + 1) * t
            norms[c] = jnp.where((stat_row == 0) & (stat_lane == h), max_sq_norm(qs[r0:r1]), norms[c])
            norms[c] = jnp.where((stat_row == 1) & (stat_lane == h), max_sq_norm(ks[r0:r1]), norms[c])
            qat_ref[c, lo:hi, :] = qat[:, r0:r1]
            qbt_ref[c, lo:hi, :] = qbt[:, r0:r1]
            vt_ref[c, h * V_ROWS:h * V_ROWS + HEAD_SLAB, :] = vt[:, r0:r1]
            vt_ref[c, h * V_ROWS + HEAD_SLAB:(h + 1) * V_ROWS, :] = jnp.ones((BF16_ROWS, t), BF16)
    for c in range(n_tiles):
        norm_ref[c] = norms[c]

    u = _gelu_tanh(zu)
    vg = _gelu_tanh(zg)
    row = lax.broadcasted_iota(jnp.int32, (SGU_CHUNK, SGU_CHUNK), 0)
    col = lax.broadcasted_iota(jnp.int32, (SGU_CHUNK, SGU_CHUNK), 1)
    tril = col <= row
    for g in range(N_GROUPS):
        lo, hi = g * GROUP_DIM, (g + 1) * GROUP_DIM
        y = vg[:, lo:hi]
        mu = jnp.mean(y, axis=-1, keepdims=True)
        yc = y - mu
        var = jnp.mean(yc * yc, axis=-1, keepdims=True)
        yn = (yc * lax.rsqrt(var + LN_EPS) * lng_ref[:, lo:hi] + lnb_ref[:, lo:hi]).astype(BF16)
        wm = jnp.where(tril, sw_ref[g], 0.0).astype(BF16)
        bias = sb_ref[g]
        n_chunks = tm // SGU_CHUNK
        chunks = jnp.concatenate([yn[c * SGU_CHUNK:(c + 1) * SGU_CHUNK, :]
                                  for c in range(n_chunks)], axis=1)
        vmix = jnp.dot(wm, chunks, preferred_element_type=F32)
        for c in range(n_chunks):
            r0, r1 = c * SGU_CHUNK, (c + 1) * SGU_CHUNK
            mixed = vmix[:, c * GROUP_DIM:(c + 1) * GROUP_DIM] + bias
            sgu_ref[r0:r1, lo:hi] = (u[r0:r1, lo:hi] * mixed).astype(BF16)


def _mix_in(layer, x2d, g, w_in, ln_g, ln_b, sgu_w, sgu_b_full):
    rows = x2d.shape[0]
    tm, t = MIX_TM, ATT_T
    assert tm % t == 0
    k_aug, q_aug = _alibi_operand_constants(t)
    grid = (rows // tm,)
    row_spec = lambda width: pl.BlockSpec((tm, width), lambda i: (i, 0))
    tr_spec = pl.BlockSpec((tm // t, ATT_WIDTH, t), lambda i: (i, 0, 0))
    whole = pl.BlockSpec(memory_space=pltpu.VMEM)
    row_sds = jax.ShapeDtypeStruct((rows, ATT_WIDTH), BF16)
    tr_sds = jax.ShapeDtypeStruct((rows // t, ATT_WIDTH, t), BF16)
    vt_spec = pl.BlockSpec((tm // t, N_HEADS * V_ROWS, t), lambda i: (i, 0, 0))
    vt_sds = jax.ShapeDtypeStruct((rows // t, N_HEADS * V_ROWS, t), BF16)
    norm_spec = pl.BlockSpec((tm // t, SUBLANES, HEAD_SLAB), lambda i: (i, 0, 0))
    norm_sds = jax.ShapeDtypeStruct((rows // t, SUBLANES, HEAD_SLAB), F32)
    return pl.pallas_call(
        functools.partial(_mixin_kernel, tm=tm, t=t),
        grid=grid,
        in_specs=[row_spec(D_MODEL), whole, _layer_spec(layer, w_in.shape[1:]), whole, whole,
                  _layer_spec(layer, sgu_w.shape[1:]), whole, whole, whole],
        out_specs=[tr_spec, tr_spec, row_spec(ATT_WIDTH), row_spec(ATT_WIDTH), vt_spec, norm_spec,
                   row_spec(SGU_WIDTH)],
        out_shape=[tr_sds, tr_sds, row_sds, row_sds, vt_sds, norm_sds, row_sds],
        compiler_params=pltpu.CompilerParams(
            dimension_semantics=("arbitrary",), vmem_limit_bytes=VMEM_LIMIT_BYTES),
        name="mix_in",
    )(x2d, g, w_in, ln_g, ln_b, sgu_w, sgu_b_full, k_aug, q_aug)


def _score_block(qi, kj, masked, q_refs, k_refs, s_ref, bm_ref, *, t):
    k0 = pl.multiple_of(kj * t, t)
    if masked:
        key = lax.broadcasted_iota(jnp.int32, (t, t), 0)
        qry = lax.broadcasted_iota(jnp.int32, (t, t), 1)
        causal = key <= qry
    for mp in range(N_MAPS):
        st = jnp.dot(k_refs[mp][pl.ds(k0, t), :], q_refs[mp][qi], preferred_element_type=F32)
        if masked:
            st = jnp.where(causal, st, NEG_INF)
        s_ref[mp] = st
        bm_ref[mp] = jnp.max(st, axis=0, keepdims=True)


def _softmax_block(qi, c, vt_blk, s_ref, bm_ref, m_ref, l_ref, acc_ref, first):
    for mp in range(N_MAPS):
        bm = bm_ref[mp] + c
        m_new = bm if first else jnp.maximum(m_ref[qi, mp], bm)
        p = jnp.exp2(s_ref[mp] - (m_new - c))
        pv_ps = jnp.dot(vt_blk, p.astype(BF16), preferred_element_type=F32)
        pv, ps = pv_ps[0:HEAD_SLAB, :], pv_ps[HEAD_SLAB:HEAD_SLAB + 1, :]
        if first:
            l_ref[qi, mp] = ps
            acc_ref[qi, mp] = pv
        else:
            alpha = jnp.exp2(m_ref[qi, mp] - m_new)
            l_ref[qi, mp] = alpha * l_ref[qi, mp] + ps
            acc_ref[qi, mp] = alpha * acc_ref[qi, mp] + pv
        m_ref[qi, mp] = m_new


def _attn_kernel(pair_q_ref, pair_k_ref, pair_n_ref,
                 slopes_ref, lq1_ref, lk1_ref, lq2_ref, lk2_ref, sg_ref,
                 qat_ref, qbt_ref, ka_ref, kb_ref, vt_ref, o_ref,
                 m_ref, l_ref, acc_ref, sa_ref, bma_ref, sb_ref, bmb_ref, *, t, nq, lam_init):
    bh = pl.program_id(0) * N_HEADS + pl.program_id(1)
    slope = slopes_ref[pl.program_id(1)]
    n_pairs = pair_n_ref[bh]
    score = functools.partial(_score_block, q_refs=(qat_ref, qbt_ref), k_refs=(ka_ref, kb_ref), t=t)
    softmax = functools.partial(_softmax_block, m_ref=m_ref, l_ref=l_ref, acc_ref=acc_ref)
    buf_a = dict(s_ref=sa_ref, bm_ref=bma_ref)
    buf_b = dict(s_ref=sb_ref, bm_ref=bmb_ref)

    bufs = (buf_a, buf_b)

    def entry(e):
        qi, kj = pair_q_ref[bh, e], pair_k_ref[bh, e]
        c = jnp.where(e < n_pairs, ((kj - qi) * t).astype(F32) * slope, NEG_INF)
        return qi, kj, c

    def score_entry(e, buf):
        qi, kj, _ = entry(e)
        score(qi, kj, False, **buf)

    def softmax_entry(e, buf):
        qi, kj, c = entry(e)
        softmax(qi, c, vt_ref[kj], first=False, **buf)

    score(0, 0, True, **buf_a)

    def diag_group(n, carry):
        for u in range(ATT_UNROLL):
            qi = ATT_UNROLL * n + u
            if isinstance(qi, int) and qi == nq - 1:
                score_entry(0, bufs[(u + 1) % 2])
            else:
                score(qi + 1, qi + 1, True, **bufs[(u + 1) % 2])
            softmax(qi, 0.0, vt_ref[qi], first=True, **bufs[u % 2])
        return carry

    lax.fori_loop(0, nq // ATT_UNROLL - 1, diag_group, 0)
    diag_group(nq // ATT_UNROLL - 1, 0)

    def off_group(first, size):
        for u in range(size):
            e = first + u
            score_entry(e + 1, bufs[(u + 1) % 2])
            softmax_entry(e, bufs[u % 2])

    n_groups = n_pairs // ATT_UNROLL
    lax.fori_loop(0, n_groups, lambda n, carry: off_group(ATT_UNROLL * n, ATT_UNROLL), None)
    done = n_groups * ATT_UNROLL
    lax.fori_loop(0, (n_pairs - done + 1) // 2, lambda n, carry: off_group(done + 2 * n, 2), None)

    lam = (jnp.exp(jnp.sum(lq1_ref[...] * lk1_ref[...], axis=-1, keepdims=True))
           - jnp.exp(jnp.sum(lq2_ref[...] * lk2_ref[...], axis=-1, keepdims=True))
           + lam_init)

    def finish(qi, carry):
        o = (acc_ref[qi, 0] * (1.0 / l_ref[qi, 0])
             - lam * (acc_ref[qi, 1] * (1.0 / l_ref[qi, 1])))
        ms = jnp.mean(o * o, axis=0, keepdims=True)
        o = o * lax.rsqrt(ms + SUBLN_EPS) * (1.0 - lam_init)
        o_ref[pl.ds(pl.multiple_of(qi * t, t), t), :] = (o.T * sg_ref[...]).astype(BF16)
        return carry

    lax.fori_loop(0, nq, finish, 0, unroll=4)


def _attention(pair_q, pair_k, pair_n, slopes, lq1, lk1, lq2, lk2, subln_g, qat, qbt, ka, kb, vt,
               *, batch, seq, lam_init):
    t = ATT_T
    nq = seq // t
    qt_spec = pl.BlockSpec((nq, HEAD_SLAB, t), lambda b, h, *_: (b, h, 0))
    vt_spec = pl.BlockSpec((nq, V_ROWS, t), lambda b, h, *_: (b, h, 0))
    row_spec = pl.BlockSpec((seq, HEAD_SLAB), lambda b, h, *_: (b, h))
    whole = pl.BlockSpec(memory_space=pltpu.VMEM)
    smem = pl.BlockSpec(memory_space=pltpu.SMEM)
    stat = pltpu.VMEM((nq, N_MAPS, 1, t), F32)
    acc = pltpu.VMEM((nq, N_MAPS, HEAD_SLAB, t), F32)
    score_buf = pltpu.VMEM((N_MAPS, t, t), F32)
    colmax_buf = pltpu.VMEM((N_MAPS, 1, t), F32)
    return pl.pallas_call(
        functools.partial(_attn_kernel, t=t, nq=nq, lam_init=lam_init),
        grid_spec=pltpu.PrefetchScalarGridSpec(
            num_scalar_prefetch=3,
            grid=(batch, N_HEADS),
            in_specs=[smem, whole, whole, whole, whole, whole,
                      qt_spec, qt_spec, row_spec, row_spec, vt_spec],
            out_specs=row_spec,
            scratch_shapes=[stat, stat, acc, score_buf, colmax_buf, score_buf, colmax_buf]),
        out_shape=jax.ShapeDtypeStruct((batch * seq, ATT_WIDTH), BF16),
        compiler_params=pltpu.CompilerParams(
            dimension_semantics=("arbitrary", "arbitrary"),
            vmem_limit_bytes=VMEM_LIMIT_BYTES),
        name="diff_attn",
    )(pair_q, pair_k, pair_n, slopes, lq1, lk1, lq2, lk2, subln_g, qat, qbt, ka, kb, vt)


def _off_diagonal_pairs(norms, slopes, batch, seq):
    nq = seq // ATT_T
    tile = np.arange(nq)
    below = tile[:, None] > tile[None, :]
    min_dist = ((tile[:, None] - tile[None, :] - 1) * ATT_T + 1).astype(np.float32)
    norms = norms.reshape(batch, nq, SUBLANES, HEAD_SLAB)
    qn = (jnp.sqrt(norms[:, :, 0, :N_HEADS]) * NORM_SAFETY).transpose(0, 2, 1)
    kn = (jnp.sqrt(norms[:, :, 1, :N_HEADS]) * NORM_SAFETY).transpose(0, 2, 1)
    bound = (qn[..., :, None] * (kn[..., None, :] + kn[..., :, None])
             - slopes[None, :, None, None] * min_dist)
    keep = below & ~(bound < -UNDERFLOW_LOG2)
    keep = keep.reshape(batch * N_HEADS, nq * nq)
    n_entries = nq * (nq - 1) // 2 + ATT_UNROLL
    order = jnp.argsort(~keep, axis=-1, stable=True)[:, :n_entries].astype(jnp.int32)
    return order // nq, order % nq, jnp.sum(keep, axis=-1, dtype=jnp.int32)


def _ffn_kernel(x_ref, att_ref, sgu_ref, wo_ref, g2_ref, wup_ref, cw_ref, cb_ref, wdn_ref, fg_ref,
                o_ref, carry_ref, act_ref, *, tm, tiles_per_batch, final):
    fc = FFN_FC
    halo = SUBLANES

    @pl.when(pl.program_id(0) % tiles_per_batch == 0)
    def _():
        carry_ref[...] = jnp.zeros_like(carry_ref)

    x1_parts, h2_parts = [], []
    for r0 in range(0, tm, tm // 2):
        r1 = r0 + tm // 2
        part = (x_ref[r0:r1, :]
                + jnp.dot(att_ref[r0:r1, :], wo_ref[0:ATT_WIDTH, :], preferred_element_type=F32)
                + jnp.dot(sgu_ref[r0:r1, :], wo_ref[ATT_WIDTH:ATT_WIDTH + SGU_WIDTH, :],
                          preferred_element_type=F32))
        ms = jnp.mean(part * part, axis=-1, keepdims=True)
        x1_parts.append(part)
        h2_parts.append((part * lax.rsqrt(ms + NORM_EPS) * g2_ref[...]).astype(BF16))
    x1 = jnp.concatenate(x1_parts, axis=0)
    h2 = jnp.concatenate(h2_parts, axis=0)

    rows8 = lax.broadcasted_iota(jnp.int32, (halo, fc), 0)
    for c in range(D_FF // fc):
        convs = []
        for half in range(2):
            col0 = half * D_FF + c * fc
            up = jnp.dot(h2, wup_ref[:, col0:col0 + fc], preferred_element_type=F32)
            prev = carry_ref[half, c]
            carry_ref[half, c] = up[tm - halo:tm, :]
            scale = 0.5 if half == 1 else 1.0
            w0 = cw_ref[0:1, col0:col0 + fc] * scale
            w1 = cw_ref[1:2, col0:col0 + fc] * scale
            w2 = cw_ref[2:3, col0:col0 + fc] * scale
            bias = cb_ref[:, col0:col0 + fc] * scale

            def conv(cur, back1, back2):
                return bias + w0 * back2 + w1 * back1 + w2 * cur

            body = conv(up, pltpu.roll(up, 1, axis=0), pltpu.roll(up, 2, axis=0))
            head = up[0:halo, :]
            head1 = jnp.where(rows8 < 1, pltpu.roll(prev, 1, axis=0), pltpu.roll(head, 1, axis=0))
            head2 = jnp.where(rows8 < 2, pltpu.roll(prev, 2, axis=0), pltpu.roll(head, 2, axis=0))
            convs.append(jnp.concatenate([conv(head, head1, head2), body[halo:, :]], axis=0))
        act_ref[:, c * fc:(c + 1) * fc] = (_gelu_tanh_x2(convs[0]) * convs[1]).astype(BF16)

    out = x1 + jnp.dot(act_ref[...], wdn_ref[...], preferred_element_type=F32)
    if final:
        ms = jnp.mean(out * out, axis=-1, keepdims=True)
        out = out * lax.rsqrt(ms + NORM_EPS) * fg_ref[...]
    o_ref[...] = out


def _ffn(layer, x2d, att, sgu, w_out, g2, w_up, conv_w, conv_b, w_down, final_g, *, seq, final):
    rows = x2d.shape[0]
    tm = FFN_TM
    grid = (rows // tm,)
    row_spec = lambda width: pl.BlockSpec((tm, width), lambda i: (i, 0))
    whole = pl.BlockSpec(memory_space=pltpu.VMEM)
    return pl.pallas_call(
        functools.partial(_ffn_kernel, tm=tm, tiles_per_batch=seq // tm, final=final),
        grid=grid,
        in_specs=[row_spec(D_MODEL), row_spec(ATT_WIDTH), row_spec(SGU_WIDTH),
                  _layer_spec(layer, w_out.shape[1:]), whole, _layer_spec(layer, w_up.shape[1:]),
                  _layer_spec(layer, conv_w.shape[1:]), whole, _layer_spec(layer, w_down.shape[1:]),
                  whole],
        out_specs=row_spec(D_MODEL),
        out_shape=jax.ShapeDtypeStruct((rows, D_MODEL), F32),
        scratch_shapes=[
            pltpu.VMEM((2, D_FF // FFN_FC, SUBLANES, FFN_FC), F32),
            pltpu.VMEM((tm, D_FF), BF16),
        ],
        compiler_params=pltpu.CompilerParams(
            dimension_semantics=("arbitrary",), vmem_limit_bytes=VMEM_LIMIT_BYTES),
        name="ffn",
    )(x2d, att, sgu, w_out, g2, w_up, conv_w, conv_b, w_down, final_g)


def kernel(x, norm1_g, w_in, lam_q1, lam_k1, lam_q2, lam_k2, subln_g, sgu_ln_g, sgu_ln_b,
           sgu_w, sgu_b, w_out, norm2_g, ffn_w_up, ffn_conv_w, ffn_conv_b, ffn_w_down, final_g):
    batch, seq, _ = x.shape
    rows = batch * seq
    x2d = x.reshape(rows, D_MODEL)
    slopes = jnp.asarray([_alibi_slope(h) * LOG2_E for h in range(N_HEADS)], dtype=F32)
    row1 = lambda a: a.reshape(1, -1).astype(F32)
    w_in, w_out, w_up, w_down = (w.astype(BF16) for w in (w_in, w_out, ffn_w_up, ffn_w_down))
    sgu_w, conv_w = sgu_w.astype(F32), ffn_conv_w.astype(F32)
    for l in range(DEPTH):
        lam_init = _lambda_init(l)
        sgu_b_full = jnp.broadcast_to(sgu_b[l].astype(F32)[:, :, None],
                                      (N_GROUPS, SGU_CHUNK, GROUP_DIM))
        qat, qbt, ka, kb, vt, norms, sgu = _mix_in(l, x2d, row1(norm1_g[l]), w_in,
                                                   row1(sgu_ln_g[l]), row1(sgu_ln_b[l]),
                                                   sgu_w, sgu_b_full)
        pair_q, pair_k, pair_n = _off_diagonal_pairs(norms, slopes, batch, seq)
        att = _attention(pair_q, pair_k, pair_n, slopes,
                         row1(lam_q1[l]), row1(lam_k1[l]), row1(lam_q2[l]), row1(lam_k2[l]),
                         row1(subln_g[l]), qat, qbt, ka, kb, vt,
                         batch=batch, seq=seq, lam_init=lam_init)
        x2d = _ffn(l, x2d, att, sgu, w_out, row1(norm2_g[l]), w_up, conv_w, row1(ffn_conv_b[l]),
                   w_down, row1(final_g), seq=seq, final=(l == DEPTH - 1))
    return x2d.reshape(batch, seq, D_MODEL)
```

```python
import functools
import math

import jax
import jax.numpy as jnp
import numpy as np
from jax import lax
from jax.experimental import pallas as pl
from jax.experimental.pallas import tpu as pltpu

F32 = jnp.float32
BF16 = jnp.bfloat16

D_MODEL = 1024
DEPTH = 2
N_HEADS = 4
N_MAPS = 2
HEAD_DIM = 64
HEAD_SLAB = 2 * HEAD_DIM
BF16_ROWS = 16
V_ROWS = HEAD_SLAB + BF16_ROWS
ATT_WIDTH = N_HEADS * HEAD_SLAB
N_GROUPS = 4
SGU_CHUNK = 128
GROUP_DIM = 128
SGU_WIDTH = N_GROUPS * GROUP_DIM
D_FF = 2816
CONV_WIDTH = 3
NORM_EPS = 1e-6
SUBLN_EPS = 1e-5
LN_EPS = 1e-5
NEG_INF = -1e30
LOG2_E = math.log2(math.e)

SUBLANES = 8
BF16_EXACT_INT = 256
UNDERFLOW_LOG2 = 150.0
NORM_SAFETY = 1.001
VMEM_LIMIT_BYTES = 56 * 1024 * 1024

MIX_TM = 1024
ATT_T = 512
ATT_UNROLL = 4
FFN_TM = 1024
FFN_FC = 256


def _gelu_tanh(x):
    c = math.sqrt(2.0 / math.pi)
    return 0.5 * x * (1.0 + jnp.tanh(c * (x + 0.044715 * (x * x * x))))


def _gelu_tanh_x2(x):
    c = math.sqrt(2.0 / math.pi)
    return x * (1.0 + jnp.tanh(x * (c + (c * 0.044715) * (x * x))))


def _alibi_slope(head):
    return 2.0 ** (-8.0 * (head + 1) / N_HEADS)


def _bf16_terms(value, n=3):
    terms = []
    rest = value
    for _ in range(n):
        term = float(np.asarray(rest, dtype=BF16).astype(np.float32))
        terms.append(term)
        rest -= term
    return terms


def _lambda_init(layer_idx):
    return 0.8 - 0.6 * math.exp(-0.3 * layer_idx)


def _layer_spec(layer, stack, single_buffer=True):
    shape = tuple(stack.shape[1:])
    zeros = (0,) * len(shape)
    mode = dict(pipeline_mode=pl.Buffered(1)) if single_buffer else {}
    return pl.BlockSpec((None,) + shape, lambda *_: (layer,) + zeros, **mode)


def _alibi_operand_constants(t):
    pos = np.arange(t)
    n_terms = len(_bf16_terms(1.0))
    parts = [pos % BF16_EXACT_INT] * n_terms + [pos - pos % BF16_EXACT_INT] * n_terms
    parts = np.stack(parts, axis=1).astype(np.float32)
    first_lane = (HEAD_DIM, 0)
    k_aug = np.zeros((N_MAPS, t, HEAD_SLAB), np.float32)
    q_aug = np.zeros((N_HEADS * N_MAPS, HEAD_SLAB), np.float32)
    for mp, lane0 in enumerate(first_lane):
        k_aug[mp, :, lane0:lane0 + 2 * n_terms] = parts
        for h in range(N_HEADS):
            factor = _bf16_terms(_alibi_slope(h) * LOG2_E)
            q_aug[N_MAPS * h + mp, lane0:lane0 + 2 * n_terms] = factor + factor
    return jnp.asarray(k_aug), jnp.asarray(q_aug)


def _mixin_kernel(x_ref, g_ref, w_ref, lng_ref, lnb_ref, sw_ref, sb_ref, kaug_ref, qaug_ref,
                  qat_ref, qbt_ref, ka_ref, kb_ref, vt_ref, norm_ref, sgu_ref, *, tm, t):
    x = x_ref[...]
    ms = jnp.mean(x * x, axis=-1, keepdims=True)
    hb = (x * lax.rsqrt(ms + NORM_EPS) * g_ref[...]).astype(BF16)

    def proj(c0, width):
        return jnp.dot(hb, w_ref[:, c0:c0 + width], preferred_element_type=F32)

    zu = proj(3 * ATT_WIDTH, SGU_WIDTH)
    zg = proj(3 * ATT_WIDTH + SGU_WIDTH, SGU_WIDTH)
    zq = proj(0, ATT_WIDTH) * (HEAD_DIM ** -0.5 * LOG2_E)
    zk = proj(ATT_WIDTH, ATT_WIDTH)
    zv = proj(2 * ATT_WIDTH, ATT_WIDTH)
    in_a = lax.broadcasted_iota(jnp.int32, (tm, HEAD_SLAB), 1) < HEAD_DIM

    def max_sq_norm(slab):
        sq = slab.astype(BF16).astype(F32) ** 2
        return jnp.max(jnp.sum(sq, axis=-1, keepdims=True), axis=0, keepdims=True)

    stat_row = lax.broadcasted_iota(jnp.int32, (SUBLANES, HEAD_SLAB), 0)
    stat_lane = lax.broadcasted_iota(jnp.int32, (SUBLANES, HEAD_SLAB), 1)
    n_tiles = tm // t
    norms = [jnp.zeros((SUBLANES, HEAD_SLAB), F32)] * n_tiles
    k_aug = [jnp.concatenate([kaug_ref[mp]] * n_tiles, axis=0) for mp in range(N_MAPS)]
    for h in range(N_HEADS):
        lo, hi = h * HEAD_SLAB, (h + 1) * HEAD_SLAB
        qs, ks = zq[:, lo:hi], zk[:, lo:hi]
        ka_ref[:, lo:hi] = jnp.where(in_a, ks, k_aug[0]).astype(BF16)
        kb_ref[:, lo:hi] = jnp.where(in_a, k_aug[1], ks).astype(BF16)
        q_rows = qaug_ref[N_MAPS * h:N_MAPS * (h + 1), :]
        qat = jnp.where(in_a, qs, q_rows[0:1, :]).T.astype(BF16)
        qbt = jnp.where(in_a, q_rows[1:2, :], qs).T.astype(BF16)
        vt = zv[:, lo:hi].T.astype(BF16)
        for c in range(n_tiles):
            r0, r1 = c * t, (c + 1) * t
            norms[c] = jnp.where((stat_row == 0) & (stat_lane == h), max_sq_norm(qs[r0:r1]), norms[c])
            norms[c] = jnp.where((stat_row == 1) & (stat_lane == h), max_sq_norm(ks[r0:r1]), norms[c])
            qat_ref[c, lo:hi, :] = qat[:, r0:r1]
            qbt_ref[c, lo:hi, :] = qbt[:, r0:r1]
            vt_ref[c, h * V_ROWS:h * V_ROWS + HEAD_SLAB, :] = vt[:, r0:r1]
            vt_ref[c, h * V_ROWS + HEAD_SLAB:(h + 1) * V_ROWS, :] = jnp.ones((BF16_ROWS, t), BF16)
    for c in range(n_tiles):
        norm_ref[c] = norms[c]

    u = _gelu_tanh(zu)
    vg = _gelu_tanh(zg)
    row = lax.broadcasted_iota(jnp.int32, (SGU_CHUNK, SGU_CHUNK), 0)
    col = lax.broadcasted_iota(jnp.int32, (SGU_CHUNK, SGU_CHUNK), 1)
    tril = col <= row
    for g in range(N_GROUPS):
        lo, hi = g * GROUP_DIM, (g + 1) * GROUP_DIM
        y = vg[:, lo:hi]
        mu = jnp.mean(y, axis=-1, keepdims=True)
        yc = y - mu
        var = jnp.mean(yc * yc, axis=-1, keepdims=True)
        yn = (yc * lax.rsqrt(var + LN_EPS) * lng_ref[:, lo:hi] + lnb_ref[:, lo:hi]).astype(BF16)
        wm = jnp.where(tril, sw_ref[g], 0.0).astype(BF16)
        bias = sb_ref[g]
        n_chunks = tm // SGU_CHUNK
        chunks = jnp.concatenate([yn[c * SGU_CHUNK:(c + 1) * SGU_CHUNK, :]
                                  for c in range(n_chunks)], axis=1)
        vmix = jnp.dot(wm, chunks, preferred_element_type=F32)
        for c in range(n_chunks):
            r0, r1 = c * SGU_CHUNK, (c + 1) * SGU_CHUNK
            mixed = vmix[:, c * GROUP_DIM:(c + 1) * GROUP_DIM] + bias
            sgu_ref[r0:r1, lo:hi] = (u[r0:r1, lo:hi] * mixed).astype(BF16)


def _mix_in(layer, x2d, g, w_in, ln_g, ln_b, sgu_w, sgu_b_full):
    rows = x2d.shape[0]
    tm, t = MIX_TM, ATT_T
    assert tm % t == 0
    small = functools.partial(_layer_spec, layer, single_buffer=False)
    k_aug, q_aug = _alibi_operand_constants(t)
    grid = (rows // tm,)
    row_spec = lambda width: pl.BlockSpec((tm, width), lambda i: (i, 0))
    tr_spec = pl.BlockSpec((tm // t, ATT_WIDTH, t), lambda i: (i, 0, 0))
    whole = pl.BlockSpec(memory_space=pltpu.VMEM)
    row_sds = jax.ShapeDtypeStruct((rows, ATT_WIDTH), BF16)
    tr_sds = jax.ShapeDtypeStruct((rows // t, ATT_WIDTH, t), BF16)
    vt_spec = pl.BlockSpec((tm // t, N_HEADS * V_ROWS, t), lambda i: (i, 0, 0))
    vt_sds = jax.ShapeDtypeStruct((rows // t, N_HEADS * V_ROWS, t), BF16)
    norm_spec = pl.BlockSpec((tm // t, SUBLANES, HEAD_SLAB), lambda i: (i, 0, 0))
    norm_sds = jax.ShapeDtypeStruct((rows // t, SUBLANES, HEAD_SLAB), F32)
    return pl.pallas_call(
        functools.partial(_mixin_kernel, tm=tm, t=t),
        grid=grid,
        in_specs=[row_spec(D_MODEL), small(g), _layer_spec(layer, w_in), small(ln_g), small(ln_b),
                  _layer_spec(layer, sgu_w), small(sgu_b_full), whole, whole],
        out_specs=[tr_spec, tr_spec, row_spec(ATT_WIDTH), row_spec(ATT_WIDTH), vt_spec, norm_spec,
                   row_spec(SGU_WIDTH)],
        out_shape=[tr_sds, tr_sds, row_sds, row_sds, vt_sds, norm_sds, row_sds],
        compiler_params=pltpu.CompilerParams(
            dimension_semantics=("arbitrary",), vmem_limit_bytes=VMEM_LIMIT_BYTES),
        name="mix_in",
    )(x2d, g, w_in, ln_g, ln_b, sgu_w, sgu_b_full, k_aug, q_aug)


def _score_block(qi, kj, masked, q_refs, k_refs, s_ref, bm_ref, *, t):
    k0 = pl.multiple_of(kj * t, t)
    if masked:
        key = lax.broadcasted_iota(jnp.int32, (t, t), 0)
        qry = lax.broadcasted_iota(jnp.int32, (t, t), 1)
        causal = key <= qry
    for mp in range(N_MAPS):
        st = jnp.dot(k_refs[mp][pl.ds(k0, t), :], q_refs[mp][qi], preferred_element_type=F32)
        if masked:
            st = jnp.where(causal, st, NEG_INF)
        s_ref[mp] = st
        bm_ref[mp] = jnp.max(st, axis=0, keepdims=True)


def _softmax_block(qi, c, vt_blk, s_ref, bm_ref, m_ref, l_ref, acc_ref, first):
    for mp in range(N_MAPS):
        bm = bm_ref[mp] + c
        m_new = bm if first else jnp.maximum(m_ref[qi, mp], bm)
        p = jnp.exp2(s_ref[mp] - (m_new - c))
        pv_ps = jnp.dot(vt_blk, p.astype(BF16), preferred_element_type=F32)
        pv, ps = pv_ps[0:HEAD_SLAB, :], pv_ps[HEAD_SLAB:HEAD_SLAB + 1, :]
        if first:
            l_ref[qi, mp] = ps
            acc_ref[qi, mp] = pv
        else:
            alpha = jnp.exp2(m_ref[qi, mp] - m_new)
            l_ref[qi, mp] = alpha * l_ref[qi, mp] + ps
            acc_ref[qi, mp] = alpha * acc_ref[qi, mp] + pv
        m_ref[qi, mp] = m_new


def _attn_kernel(pair_q_ref, pair_k_ref, pair_n_ref,
                 slopes_ref, lq1_ref, lk1_ref, lq2_ref, lk2_ref, sg_ref,
                 qat_ref, qbt_ref, ka_ref, kb_ref, vt_ref, o_ref,
                 m_ref, l_ref, acc_ref, sa_ref, bma_ref, sb_ref, bmb_ref, *, t, nq, lam_init):
    bh = pl.program_id(0) * N_HEADS + pl.program_id(1)
    slope = slopes_ref[pl.program_id(1)]
    n_pairs = pair_n_ref[bh]
    score = functools.partial(_score_block, q_refs=(qat_ref, qbt_ref), k_refs=(ka_ref, kb_ref), t=t)
    softmax = functools.partial(_softmax_block, m_ref=m_ref, l_ref=l_ref, acc_ref=acc_ref)
    buf_a = dict(s_ref=sa_ref, bm_ref=bma_ref)
    buf_b = dict(s_ref=sb_ref, bm_ref=bmb_ref)

    bufs = (buf_a, buf_b)

    def entry(e):
        qi, kj = pair_q_ref[bh, e], pair_k_ref[bh, e]
        c = jnp.where(e < n_pairs, ((kj - qi) * t).astype(F32) * slope, NEG_INF)
        return qi, kj, c

    def score_entry(e, buf):
        qi, kj, _ = entry(e)
        score(qi, kj, False, **buf)

    def softmax_entry(e, buf):
        qi, kj, c = entry(e)
        softmax(qi, c, vt_ref[kj], first=False, **buf)

    score(0, 0, True, **buf_a)

    def diag_group(n, carry):
        for u in range(ATT_UNROLL):
            qi = ATT_UNROLL * n + u
            if isinstance(qi, int) and qi == nq - 1:
                score_entry(0, bufs[(u + 1) % 2])
            else:
                score(qi + 1, qi + 1, True, **bufs[(u + 1) % 2])
            softmax(qi, 0.0, vt_ref[qi], first=True, **bufs[u % 2])
        return carry

    lax.fori_loop(0, nq // ATT_UNROLL - 1, diag_group, 0)
    diag_group(nq // ATT_UNROLL - 1, 0)

    def off_group(first, size):
        for u in range(size):
            e = first + u
            score_entry(e + 1, bufs[(u + 1) % 2])
            softmax_entry(e, bufs[u % 2])

    n_groups = n_pairs // ATT_UNROLL
    lax.fori_loop(0, n_groups, lambda n, carry: off_group(ATT_UNROLL * n, ATT_UNROLL), None)
    done = n_groups * ATT_UNROLL
    lax.fori_loop(0, (n_pairs - done + 1) // 2, lambda n, carry: off_group(done + 2 * n, 2), None)

    lam = (jnp.exp(jnp.sum(lq1_ref[...] * lk1_ref[...], axis=-1, keepdims=True))
           - jnp.exp(jnp.sum(lq2_ref[...] * lk2_ref[...], axis=-1, keepdims=True))
           + lam_init)

    def finish(qi, carry):
        o = (acc_ref[qi, 0] * (1.0 / l_ref[qi, 0])
             - lam * (acc_ref[qi, 1] * (1.0 / l_ref[qi, 1])))
        ms = jnp.mean(o * o, axis=0, keepdims=True)
        o = o * lax.rsqrt(ms + SUBLN_EPS) * (1.0 - lam_init)
        o_ref[pl.ds(pl.multiple_of(qi * t, t), t), :] = (o.T * sg_ref[...]).astype(BF16)
        return carry

    lax.fori_loop(0, nq, finish, 0, unroll=4)


def _attention(layer, pair_q, pair_k, pair_n, slopes, lq1, lk1, lq2, lk2, subln_g,
               qat, qbt, ka, kb, vt, *, batch, seq, lam_init):
    t = ATT_T
    nq = seq // t
    small = functools.partial(_layer_spec, layer, single_buffer=False)
    qt_spec = pl.BlockSpec((nq, HEAD_SLAB, t), lambda b, h, *_: (b, h, 0))
    vt_spec = pl.BlockSpec((nq, V_ROWS, t), lambda b, h, *_: (b, h, 0))
    row_spec = pl.BlockSpec((seq, HEAD_SLAB), lambda b, h, *_: (b, h))
    whole = pl.BlockSpec(memory_space=pltpu.VMEM)
    smem = pl.BlockSpec(memory_space=pltpu.SMEM)
    stat = pltpu.VMEM((nq, N_MAPS, 1, t), F32)
    acc = pltpu.VMEM((nq, N_MAPS, HEAD_SLAB, t), F32)
    score_buf = pltpu.VMEM((N_MAPS, t, t), F32)
    colmax_buf = pltpu.VMEM((N_MAPS, 1, t), F32)
    return pl.pallas_call(
        functools.partial(_attn_kernel, t=t, nq=nq, lam_init=lam_init),
        grid_spec=pltpu.PrefetchScalarGridSpec(
            num_scalar_prefetch=3,
            grid=(batch, N_HEADS),
            in_specs=[smem, small(lq1), small(lk1), small(lq2), small(lk2), small(subln_g),
                      qt_spec, qt_spec, row_spec, row_spec, vt_spec],
            out_specs=row_spec,
            scratch_shapes=[stat, stat, acc, score_buf, colmax_buf, score_buf, colmax_buf]),
        out_shape=jax.ShapeDtypeStruct((batch * seq, ATT_WIDTH), BF16),
        compiler_params=pltpu.CompilerParams(
            dimension_semantics=("arbitrary", "arbitrary"),
            vmem_limit_bytes=VMEM_LIMIT_BYTES),
        name="diff_attn",
    )(pair_q, pair_k, pair_n, slopes, lq1, lk1, lq2, lk2, subln_g, qat, qbt, ka, kb, vt)


def _off_diagonal_pairs(norms, slopes, batch, seq):
    nq = seq // ATT_T
    tile = np.arange(nq)
    below = tile[:, None] > tile[None, :]
    min_dist = ((tile[:, None] - tile[None, :] - 1) * ATT_T + 1).astype(np.float32)
    norms = norms.reshape(batch, nq, SUBLANES, HEAD_SLAB)
    qn = (jnp.sqrt(norms[:, :, 0, :N_HEADS]) * NORM_SAFETY).transpose(0, 2, 1)
    kn = (jnp.sqrt(norms[:, :, 1, :N_HEADS]) * NORM_SAFETY).transpose(0, 2, 1)
    bound = (qn[..., :, None] * (kn[..., None, :] + kn[..., :, None])
             - slopes[None, :, None, None] * min_dist)
    keep = below & ~(bound < -UNDERFLOW_LOG2)
    keep = keep.reshape(batch * N_HEADS, nq * nq)
    n_entries = nq * (nq - 1) // 2 + ATT_UNROLL
    order = jnp.argsort(~keep, axis=-1, stable=True)[:, :n_entries].astype(jnp.int32)
    return order // nq, order % nq, jnp.sum(keep, axis=-1, dtype=jnp.int32)


def _ffn_kernel(x_ref, att_ref, sgu_ref, wo_ref, g2_ref, wup_ref, cw_ref, cb_ref, wdn_ref, fg_ref,
                o_ref, carry_ref, act_ref, *, tm, tiles_per_batch, final):
    fc = FFN_FC
    halo = SUBLANES

    @pl.when(pl.program_id(0) % tiles_per_batch == 0)
    def _():
        carry_ref[...] = jnp.zeros_like(carry_ref)

    x1_parts, h2_parts = [], []
    for r0 in range(0, tm, tm // 2):
        r1 = r0 + tm // 2
        part = (x_ref[r0:r1, :]
                + jnp.dot(att_ref[r0:r1, :], wo_ref[0:ATT_WIDTH, :], preferred_element_type=F32)
                + jnp.dot(sgu_ref[r0:r1, :], wo_ref[ATT_WIDTH:ATT_WIDTH + SGU_WIDTH, :],
                          preferred_element_type=F32))
        ms = jnp.mean(part * part, axis=-1, keepdims=True)
        x1_parts.append(part)
        h2_parts.append((part * lax.rsqrt(ms + NORM_EPS) * g2_ref[...]).astype(BF16))
    x1 = jnp.concatenate(x1_parts, axis=0)
    h2 = jnp.concatenate(h2_parts, axis=0)

    rows8 = lax.broadcasted_iota(jnp.int32, (halo, fc), 0)
    for c in range(D_FF // fc):
        convs = []
        for half in range(2):
            col0 = half * D_FF + c * fc
            up = jnp.dot(h2, wup_ref[:, col0:col0 + fc], preferred_element_type=F32)
            prev = carry_ref[half, c]
            carry_ref[half, c] = up[tm - halo:tm, :]
            scale = 0.5 if half == 1 else 1.0
            w0 = cw_ref[0:1, col0:col0 + fc] * scale
            w1 = cw_ref[1:2, col0:col0 + fc] * scale
            w2 = cw_ref[2:3, col0:col0 + fc] * scale
            bias = cb_ref[:, col0:col0 + fc] * scale

            def conv(cur, back1, back2):
                return bias + w0 * back2 + w1 * back1 + w2 * cur

            body = conv(up, pltpu.roll(up, 1, axis=0), pltpu.roll(up, 2, axis=0))
            head = up[0:halo, :]
            head1 = jnp.where(rows8 < 1, pltpu.roll(prev, 1, axis=0), pltpu.roll(head, 1, axis=0))
            head2 = jnp.where(rows8 < 2, pltpu.roll(prev, 2, axis=0), pltpu.roll(head, 2, axis=0))
            convs.append(jnp.concatenate([conv(head, head1, head2), body[halo:, :]], axis=0))
        act_ref[:, c * fc:(c + 1) * fc] = (_gelu_tanh_x2(convs[0]) * convs[1]).astype(BF16)

    out = x1 + jnp.dot(act_ref[...], wdn_ref[...], preferred_element_type=F32)
    if final:
        ms = jnp.mean(out * out, axis=-1, keepdims=True)
        out = out * lax.rsqrt(ms + NORM_EPS) * fg_ref[...]
    o_ref[...] = out


def _ffn(layer, x2d, att, sgu, w_out, g2, w_up, conv_w, conv_b, w_down, final_g, *, seq, final):
    rows = x2d.shape[0]
    tm = FFN_TM
    grid = (rows // tm,)
    row_spec = lambda width: pl.BlockSpec((tm, width), lambda i: (i, 0))
    whole = pl.BlockSpec(memory_space=pltpu.VMEM)
    small = functools.partial(_layer_spec, layer, single_buffer=False)
    return pl.pallas_call(
        functools.partial(_ffn_kernel, tm=tm, tiles_per_batch=seq // tm, final=final),
        grid=grid,
        in_specs=[row_spec(D_MODEL), row_spec(ATT_WIDTH), row_spec(SGU_WIDTH),
                  _layer_spec(layer, w_out), small(g2), _layer_spec(layer, w_up),
                  _layer_spec(layer, conv_w), small(conv_b), _layer_spec(layer, w_down),
                  whole],
        out_specs=row_spec(D_MODEL),
        out_shape=jax.ShapeDtypeStruct((rows, D_MODEL), F32),
        scratch_shapes=[
            pltpu.VMEM((2, D_FF // FFN_FC, SUBLANES, FFN_FC), F32),
            pltpu.VMEM((tm, D_FF), BF16),
        ],
        compiler_params=pltpu.CompilerParams(
            dimension_semantics=("arbitrary",), vmem_limit_bytes=VMEM_LIMIT_BYTES),
        name="ffn",
    )(x2d, att, sgu, w_out, g2, w_up, conv_w, conv_b, w_down, final_g)


def kernel(x, norm1_g, w_in, lam_q1, lam_k1, lam_q2, lam_k2, subln_g, sgu_ln_g, sgu_ln_b,
           sgu_w, sgu_b, w_out, norm2_g, ffn_w_up, ffn_conv_w, ffn_conv_b, ffn_w_down, final_g):
    batch, seq, _ = x.shape
    rows = batch * seq
    x2d = x.reshape(rows, D_MODEL)
    slopes = jnp.asarray([_alibi_slope(h) * LOG2_E for h in range(N_HEADS)], dtype=F32)
    rows_of = lambda a: a.reshape(DEPTH, 1, -1).astype(F32)
    norm1_g, norm2_g, subln_g, sgu_ln_g, sgu_ln_b, conv_b = map(
        rows_of, (norm1_g, norm2_g, subln_g, sgu_ln_g, sgu_ln_b, ffn_conv_b))
    lam_q1, lam_k1, lam_q2, lam_k2 = map(rows_of, (lam_q1, lam_k1, lam_q2, lam_k2))
    final_g = final_g.reshape(1, -1).astype(F32)
    w_in, w_out, w_up, w_down = (w.astype(BF16) for w in (w_in, w_out, ffn_w_up, ffn_w_down))
    sgu_w, conv_w = sgu_w.astype(F32), ffn_conv_w.astype(F32)
    sgu_b_full = jnp.broadcast_to(sgu_b.astype(F32)[..., None],
                                  (DEPTH, N_GROUPS, SGU_CHUNK, GROUP_DIM))
    for l in range(DEPTH):
        lam_init = _lambda_init(l)
        qat, qbt, ka, kb, vt, norms, sgu = _mix_in(l, x2d, norm1_g, w_in, sgu_ln_g, sgu_ln_b,
                                                   sgu_w, sgu_b_full)
        pair_q, pair_k, pair_n = _off_diagonal_pairs(norms, slopes, batch, seq)
        att = _attention(l, pair_q, pair_k, pair_n, slopes, lam_q1, lam_k1, lam_q2, lam_k2,
                         subln_g, qat, qbt, ka, kb, vt, batch=batch, seq=seq, lam_init=lam_init)
        x2d = _ffn(l, x2d, att, sgu, w_out, norm2_g, w_up, conv_w, conv_b, w_down, final_g,
                   seq=seq, final=(l == DEPTH - 1))
    return x2d.reshape(batch, seq, D_MODEL)
```

```python
import functools
import math

import jax
import jax.numpy as jnp
import numpy as np
from jax import lax
from jax.experimental import pallas as pl
from jax.experimental.pallas import tpu as pltpu

F32 = jnp.float32
BF16 = jnp.bfloat16

D_MODEL = 1024
DEPTH = 2
N_HEADS = 4
N_MAPS = 2
HEAD_DIM = 64
HEAD_SLAB = 2 * HEAD_DIM
BF16_ROWS = 16
V_ROWS = HEAD_SLAB + BF16_ROWS
ATT_WIDTH = N_HEADS * HEAD_SLAB
N_GROUPS = 4
SGU_CHUNK = 128
GROUP_DIM = 128
SGU_WIDTH = N_GROUPS * GROUP_DIM
D_FF = 2816
CONV_WIDTH = 3
NORM_EPS = 1e-6
SUBLN_EPS = 1e-5
LN_EPS = 1e-5
NEG_INF = -1e30
LOG2_E = math.log2(math.e)

SUBLANES = 8
BF16_EXACT_INT = 256
UNDERFLOW_LOG2 = 150.0
NORM_SAFETY = 1.001
VMEM_LIMIT_BYTES = 56 * 1024 * 1024

MIX_TM = 1024
ATT_T = 512
ATT_UNROLL = 8
FFN_TM = 1024
FFN_FC = 256


def _gelu_tanh(x):
    c = math.sqrt(2.0 / math.pi)
    return 0.5 * x * (1.0 + jnp.tanh(c * (x + 0.044715 * (x * x * x))))


def _gelu_tanh_x2(x):
    c = math.sqrt(2.0 / math.pi)
    return x * (1.0 + jnp.tanh(x * (c + (c * 0.044715) * (x * x))))


def _alibi_slope(head):
    return 2.0 ** (-8.0 * (head + 1) / N_HEADS)


def _bf16_terms(value, n=3):
    terms = []
    rest = value
    for _ in range(n):
        term = float(np.asarray(rest, dtype=BF16).astype(np.float32))
        terms.append(term)
        rest -= term
    return terms


def _lambda_init(layer_idx):
    return 0.8 - 0.6 * math.exp(-0.3 * layer_idx)


def _layer_spec(layer, stack, single_buffer=True):
    shape = tuple(stack.shape[1:])
    zeros = (0,) * len(shape)
    mode = dict(pipeline_mode=pl.Buffered(1)) if single_buffer else {}
    return pl.BlockSpec((None,) + shape, lambda *_: (layer,) + zeros, **mode)


def _alibi_operand_constants(t):
    pos = np.arange(t)
    n_terms = len(_bf16_terms(1.0))
    parts = [pos % BF16_EXACT_INT] * n_terms + [pos - pos % BF16_EXACT_INT] * n_terms
    parts = np.stack(parts, axis=1).astype(np.float32)
    first_lane = (HEAD_DIM, 0)
    k_aug = np.zeros((N_MAPS, t, HEAD_SLAB), np.float32)
    q_aug = np.zeros((N_HEADS * N_MAPS, HEAD_SLAB), np.float32)
    for mp, lane0 in enumerate(first_lane):
        k_aug[mp, :, lane0:lane0 + 2 * n_terms] = parts
        for h in range(N_HEADS):
            factor = _bf16_terms(_alibi_slope(h) * LOG2_E)
            q_aug[N_MAPS * h + mp, lane0:lane0 + 2 * n_terms] = factor + factor
    return jnp.asarray(k_aug), jnp.asarray(q_aug)


def _mixin_kernel(x_ref, g_ref, w_ref, lng_ref, lnb_ref, sw_ref, sb_ref, kaug_ref, qaug_ref,
                  qat_ref, qbt_ref, ka_ref, kb_ref, vt_ref, norm_ref, sgu_ref, *, tm, t):
    x = x_ref[...]
    ms = jnp.mean(x * x, axis=-1, keepdims=True)
    hb = (x * lax.rsqrt(ms + NORM_EPS) * g_ref[...]).astype(BF16)

    def proj(c0, width):
        return jnp.dot(hb, w_ref[:, c0:c0 + width], preferred_element_type=F32)

    zu = proj(3 * ATT_WIDTH, SGU_WIDTH)
    zg = proj(3 * ATT_WIDTH + SGU_WIDTH, SGU_WIDTH)
    zq = proj(0, ATT_WIDTH) * (HEAD_DIM ** -0.5 * LOG2_E)
    zk = proj(ATT_WIDTH, ATT_WIDTH)
    zv = proj(2 * ATT_WIDTH, ATT_WIDTH)
    in_a = lax.broadcasted_iota(jnp.int32, (tm, HEAD_SLAB), 1) < HEAD_DIM

    def max_sq_norm(slab):
        sq = slab.astype(BF16).astype(F32) ** 2
        return jnp.max(jnp.sum(sq, axis=-1, keepdims=True), axis=0, keepdims=True)

    stat_row = lax.broadcasted_iota(jnp.int32, (SUBLANES, HEAD_SLAB), 0)
    stat_lane = lax.broadcasted_iota(jnp.int32, (SUBLANES, HEAD_SLAB), 1)
    n_tiles = tm // t
    norms = [jnp.zeros((SUBLANES, HEAD_SLAB), F32)] * n_tiles
    k_aug = [jnp.concatenate([kaug_ref[mp]] * n_tiles, axis=0) for mp in range(N_MAPS)]
    for h in range(N_HEADS):
        lo, hi = h * HEAD_SLAB, (h + 1) * HEAD_SLAB
        qs, ks = zq[:, lo:hi], zk[:, lo:hi]
        ka_ref[:, lo:hi] = jnp.where(in_a, ks, k_aug[0]).astype(BF16)
        kb_ref[:, lo:hi] = jnp.where(in_a, k_aug[1], ks).astype(BF16)
        q_rows = qaug_ref[N_MAPS * h:N_MAPS * (h + 1), :]
        qat = jnp.where(in_a, qs, q_rows[0:1, :]).T.astype(BF16)
        qbt = jnp.where(in_a, q_rows[1:2, :], qs).T.astype(BF16)
        vt = zv[:, lo:hi].T.astype(BF16)
        for c in range(n_tiles):
            r0, r1 = c * t, (c + 1) * t
            norms[c] = jnp.where((stat_row == 0) & (stat_lane == h), max_sq_norm(qs[r0:r1]), norms[c])
            norms[c] = jnp.where((stat_row == 1) & (stat_lane == h), max_sq_norm(ks[r0:r1]), norms[c])
            qat_ref[c, lo:hi, :] = qat[:, r0:r1]
            qbt_ref[c, lo:hi, :] = qbt[:, r0:r1]
            vt_ref[c, h * V_ROWS:h * V_ROWS + HEAD_SLAB, :] = vt[:, r0:r1]
            vt_ref[c, h * V_ROWS + HEAD_SLAB:(h + 1) * V_ROWS, :] = jnp.ones((BF16_ROWS, t), BF16)
    for c in range(n_tiles):
        norm_ref[c] = norms[c]

    u = _gelu_tanh(zu)
    vg = _gelu_tanh(zg)
    row = lax.broadcasted_iota(jnp.int32, (SGU_CHUNK, SGU_CHUNK), 0)
    col = lax.broadcasted_iota(jnp.int32, (SGU_CHUNK, SGU_CHUNK), 1)
    tril = col <= row
    for g in range(N_GROUPS):
        lo, hi = g * GROUP_DIM, (g + 1) * GROUP_DIM
        y = vg[:, lo:hi]
        mu = jnp.mean(y, axis=-1, keepdims=True)
        yc = y - mu
        var = jnp.mean(yc * yc, axis=-1, keepdims=True)
        yn = (yc * lax.rsqrt(var + LN_EPS) * lng_ref[:, lo:hi] + lnb_ref[:, lo:hi]).astype(BF16)
        wm = jnp.where(tril, sw_ref[g], 0.0).astype(BF16)
        bias = sb_ref[g]
        n_chunks = tm // SGU_CHUNK
        chunks = jnp.concatenate([yn[c * SGU_CHUNK:(c + 1) * SGU_CHUNK, :]
                                  for c in range(n_chunks)], axis=1)
        vmix = jnp.dot(wm, chunks, preferred_element_type=F32)
        for c in range(n_chunks):
            r0, r1 = c * SGU_CHUNK, (c + 1) * SGU_CHUNK
            mixed = vmix[:, c * GROUP_DIM:(c + 1) * GROUP_DIM] + bias
            sgu_ref[r0:r1, lo:hi] = (u[r0:r1, lo:hi] * mixed).astype(BF16)


def _mix_in(layer, x2d, g, w_in, ln_g, ln_b, sgu_w, sgu_b_full):
    rows = x2d.shape[0]
    tm, t = MIX_TM, ATT_T
    assert tm % t == 0
    small = functools.partial(_layer_spec, layer, single_buffer=False)
    k_aug, q_aug = _alibi_operand_constants(t)
    grid = (rows // tm,)
    row_spec = lambda width: pl.BlockSpec((tm, width), lambda i: (i, 0))
    tr_spec = pl.BlockSpec((tm // t, ATT_WIDTH, t), lambda i: (i, 0, 0))
    whole = pl.BlockSpec(memory_space=pltpu.VMEM)
    row_sds = jax.ShapeDtypeStruct((rows, ATT_WIDTH), BF16)
    tr_sds = jax.ShapeDtypeStruct((rows // t, ATT_WIDTH, t), BF16)
    vt_spec = pl.BlockSpec((tm // t, N_HEADS * V_ROWS, t), lambda i: (i, 0, 0))
    vt_sds = jax.ShapeDtypeStruct((rows // t, N_HEADS * V_ROWS, t), BF16)
    norm_spec = pl.BlockSpec((tm // t, SUBLANES, HEAD_SLAB), lambda i: (i, 0, 0))
    norm_sds = jax.ShapeDtypeStruct((rows // t, SUBLANES, HEAD_SLAB), F32)
    return pl.pallas_call(
        functools.partial(_mixin_kernel, tm=tm, t=t),
        grid=grid,
        in_specs=[row_spec(D_MODEL), small(g), _layer_spec(layer, w_in), small(ln_g), small(ln_b),
                  _layer_spec(layer, sgu_w), small(sgu_b_full), whole, whole],
        out_specs=[tr_spec, tr_spec, row_spec(ATT_WIDTH), row_spec(ATT_WIDTH), vt_spec, norm_spec,
                   row_spec(SGU_WIDTH)],
        out_shape=[tr_sds, tr_sds, row_sds, row_sds, vt_sds, norm_sds, row_sds],
        compiler_params=pltpu.CompilerParams(
            dimension_semantics=("arbitrary",), vmem_limit_bytes=VMEM_LIMIT_BYTES),
        name="mix_in",
    )(x2d, g, w_in, ln_g, ln_b, sgu_w, sgu_b_full, k_aug, q_aug)


def _score_block(qi, kj, masked, q_refs, k_refs, s_ref, bm_ref, *, t):
    k0 = pl.multiple_of(kj * t, t)
    if masked:
        key = lax.broadcasted_iota(jnp.int32, (t, t), 0)
        qry = lax.broadcasted_iota(jnp.int32, (t, t), 1)
        causal = key <= qry
    for mp in range(N_MAPS):
        st = jnp.dot(k_refs[mp][pl.ds(k0, t), :], q_refs[mp][qi], preferred_element_type=F32)
        if masked:
            st = jnp.where(causal, st, NEG_INF)
        s_ref[mp] = st
        bm_ref[mp] = jnp.max(st, axis=0, keepdims=True)


def _softmax_block(qi, c, vt_blk, s_ref, bm_ref, m_ref, l_ref, acc_ref, first):
    for mp in range(N_MAPS):
        bm = bm_ref[mp] + c
        m_new = bm if first else jnp.maximum(m_ref[qi, mp], bm)
        p = jnp.exp2(s_ref[mp] - (m_new - c))
        pv_ps = jnp.dot(vt_blk, p.astype(BF16), preferred_element_type=F32)
        pv, ps = pv_ps[0:HEAD_SLAB, :], pv_ps[HEAD_SLAB:HEAD_SLAB + 1, :]
        if first:
            l_ref[qi, mp] = ps
            acc_ref[qi, mp] = pv
        else:
            alpha = jnp.exp2(m_ref[qi, mp] - m_new)
            l_ref[qi, mp] = alpha * l_ref[qi, mp] + ps
            acc_ref[qi, mp] = alpha * acc_ref[qi, mp] + pv
        m_ref[qi, mp] = m_new


def _attn_kernel(pair_q_ref, pair_k_ref, pair_n_ref,
                 slopes_ref, lq1_ref, lk1_ref, lq2_ref, lk2_ref, sg_ref,
                 qat_ref, qbt_ref, ka_ref, kb_ref, vt_ref, o_ref,
                 m_ref, l_ref, acc_ref, sa_ref, bma_ref, sb_ref, bmb_ref, *, t, nq, lam_init):
    bh = pl.program_id(0) * N_HEADS + pl.program_id(1)
    slope = slopes_ref[pl.program_id(1)]
    n_pairs = pair_n_ref[bh]
    score = functools.partial(_score_block, q_refs=(qat_ref, qbt_ref), k_refs=(ka_ref, kb_ref), t=t)
    softmax = functools.partial(_softmax_block, m_ref=m_ref, l_ref=l_ref, acc_ref=acc_ref)
    buf_a = dict(s_ref=sa_ref, bm_ref=bma_ref)
    buf_b = dict(s_ref=sb_ref, bm_ref=bmb_ref)

    bufs = (buf_a, buf_b)

    def entry(e):
        qi, kj = pair_q_ref[bh, e], pair_k_ref[bh, e]
        c = jnp.where(e < n_pairs, ((kj - qi) * t).astype(F32) * slope, NEG_INF)
        return qi, kj, c

    def score_entry(e, buf):
        qi, kj, _ = entry(e)
        score(qi, kj, False, **buf)

    def softmax_entry(e, buf):
        qi, kj, c = entry(e)
        softmax(qi, c, vt_ref[kj], first=False, **buf)

    score(0, 0, True, **buf_a)

    def diag_group(n, carry):
        for u in range(ATT_UNROLL):
            qi = ATT_UNROLL * n + u
            if isinstance(qi, int) and qi == nq - 1:
                score_entry(0, bufs[(u + 1) % 2])
            else:
                score(qi + 1, qi + 1, True, **bufs[(u + 1) % 2])
            softmax(qi, 0.0, vt_ref[qi], first=True, **bufs[u % 2])
        return carry

    lax.fori_loop(0, nq // ATT_UNROLL - 1, diag_group, 0)
    diag_group(nq // ATT_UNROLL - 1, 0)

    def off_group(first, size):
        for u in range(size):
            e = first + u
            score_entry(e + 1, bufs[(u + 1) % 2])
            softmax_entry(e, bufs[u % 2])

    n_groups = n_pairs // ATT_UNROLL
    lax.fori_loop(0, n_groups, lambda n, carry: off_group(ATT_UNROLL * n, ATT_UNROLL), None)
    done = n_groups * ATT_UNROLL
    lax.fori_loop(0, (n_pairs - done + 1) // 2, lambda n, carry: off_group(done + 2 * n, 2), None)

    lam = (jnp.exp(jnp.sum(lq1_ref[...] * lk1_ref[...], axis=-1, keepdims=True))
           - jnp.exp(jnp.sum(lq2_ref[...] * lk2_ref[...], axis=-1, keepdims=True))
           + lam_init)

    def finish(qi, carry):
        o = (acc_ref[qi, 0] * (1.0 / l_ref[qi, 0])
             - lam * (acc_ref[qi, 1] * (1.0 / l_ref[qi, 1])))
        ms = jnp.mean(o * o, axis=0, keepdims=True)
        o = o * lax.rsqrt(ms + SUBLN_EPS) * (1.0 - lam_init)
        o_ref[pl.ds(pl.multiple_of(qi * t, t), t), :] = (o.T * sg_ref[...]).astype(BF16)
        return carry

    lax.fori_loop(0, nq, finish, 0, unroll=4)


def _attention(layer, pair_q, pair_k, pair_n, slopes, lq1, lk1, lq2, lk2, subln_g,
               qat, qbt, ka, kb, vt, *, batch, seq, lam_init):
    t = ATT_T
    nq = seq // t
    small = functools.partial(_layer_spec, layer, single_buffer=False)
    qt_spec = pl.BlockSpec((nq, HEAD_SLAB, t), lambda b, h, *_: (b, h, 0))
    vt_spec = pl.BlockSpec((nq, V_ROWS, t), lambda b, h, *_: (b, h, 0))
    row_spec = pl.BlockSpec((seq, HEAD_SLAB), lambda b, h, *_: (b, h))
    whole = pl.BlockSpec(memory_space=pltpu.VMEM)
    smem = pl.BlockSpec(memory_space=pltpu.SMEM)
    stat = pltpu.VMEM((nq, N_MAPS, 1, t), F32)
    acc = pltpu.VMEM((nq, N_MAPS, HEAD_SLAB, t), F32)
    score_buf = pltpu.VMEM((N_MAPS, t, t), F32)
    colmax_buf = pltpu.VMEM((N_MAPS, 1, t), F32)
    return pl.pallas_call(
        functools.partial(_attn_kernel, t=t, nq=nq, lam_init=lam_init),
        grid_spec=pltpu.PrefetchScalarGridSpec(
            num_scalar_prefetch=3,
            grid=(batch, N_HEADS),
            in_specs=[smem, small(lq1), small(lk1), small(lq2), small(lk2), small(subln_g),
                      qt_spec, qt_spec, row_spec, row_spec, vt_spec],
            out_specs=row_spec,
            scratch_shapes=[stat, stat, acc, score_buf, colmax_buf, score_buf, colmax_buf]),
        out_shape=jax.ShapeDtypeStruct((batch * seq, ATT_WIDTH), BF16),
        compiler_params=pltpu.CompilerParams(
            dimension_semantics=("arbitrary", "arbitrary"),
            vmem_limit_bytes=VMEM_LIMIT_BYTES),
        name="diff_attn",
    )(pair_q, pair_k, pair_n, slopes, lq1, lk1, lq2, lk2, subln_g, qat, qbt, ka, kb, vt)


def _off_diagonal_pairs(norms, slopes, batch, seq):
    nq = seq // ATT_T
    tile = np.arange(nq)
    below = tile[:, None] > tile[None, :]
    min_dist = ((tile[:, None] - tile[None, :] - 1) * ATT_T + 1).astype(np.float32)
    norms = norms.reshape(batch, nq, SUBLANES, HEAD_SLAB)
    qn = (jnp.sqrt(norms[:, :, 0, :N_HEADS]) * NORM_SAFETY).transpose(0, 2, 1)
    kn = (jnp.sqrt(norms[:, :, 1, :N_HEADS]) * NORM_SAFETY).transpose(0, 2, 1)
    bound = (qn[..., :, None] * (kn[..., None, :] + kn[..., :, None])
             - slopes[None, :, None, None] * min_dist)
    keep = below & ~(bound < -UNDERFLOW_LOG2)
    keep = keep.reshape(batch * N_HEADS, nq * nq)
    n_entries = nq * (nq - 1) // 2 + ATT_UNROLL
    order = jnp.argsort(~keep, axis=-1, stable=True)[:, :n_entries].astype(jnp.int32)
    return order // nq, order % nq, jnp.sum(keep, axis=-1, dtype=jnp.int32)


def _ffn_kernel(x_ref, att_ref, sgu_ref, wo_ref, g2_ref, wup_ref, cw_ref, cb_ref, wdn_ref, fg_ref,
                o_ref, carry_ref, act_ref, *, tm, tiles_per_batch, final):
    fc = FFN_FC
    halo = SUBLANES

    @pl.when(pl.program_id(0) % tiles_per_batch == 0)
    def _():
        carry_ref[...] = jnp.zeros_like(carry_ref)

    x1_parts, h2_parts = [], []
    for r0 in range(0, tm, tm // 2):
        r1 = r0 + tm // 2
        part = (x_ref[r0:r1, :]
                + jnp.dot(att_ref[r0:r1, :], wo_ref[0:ATT_WIDTH, :], preferred_element_type=F32)
                + jnp.dot(sgu_ref[r0:r1, :], wo_ref[ATT_WIDTH:ATT_WIDTH + SGU_WIDTH, :],
                          preferred_element_type=F32))
        ms = jnp.mean(part * part, axis=-1, keepdims=True)
        x1_parts.append(part)
        h2_parts.append((part * lax.rsqrt(ms + NORM_EPS) * g2_ref[...]).astype(BF16))
    x1 = jnp.concatenate(x1_parts, axis=0)
    h2 = jnp.concatenate(h2_parts, axis=0)

    rows8 = lax.broadcasted_iota(jnp.int32, (halo, fc), 0)
    for c in range(D_FF // fc):
        convs = []
        for half in range(2):
            col0 = half * D_FF + c * fc
            up = jnp.dot(h2, wup_ref[:, col0:col0 + fc], preferred_element_type=F32)
            prev = carry_ref[half, c]
            carry_ref[half, c] = up[tm - halo:tm, :]
            scale = 0.5 if half == 1 else 1.0
            w0 = cw_ref[0:1, col0:col0 + fc] * scale
            w1 = cw_ref[1:2, col0:col0 + fc] * scale
            w2 = cw_ref[2:3, col0:col0 + fc] * scale
            bias = cb_ref[:, col0:col0 + fc] * scale

            def conv(cur, back1, back2):
                return bias + w0 * back2 + w1 * back1 + w2 * cur

            body = conv(up, pltpu.roll(up, 1, axis=0), pltpu.roll(up, 2, axis=0))
            head = up[0:halo, :]
            head1 = jnp.where(rows8 < 1, pltpu.roll(prev, 1, axis=0), pltpu.roll(head, 1, axis=0))
            head2 = jnp.where(rows8 < 2, pltpu.roll(prev, 2, axis=0), pltpu.roll(head, 2, axis=0))
            convs.append(jnp.concatenate([conv(head, head1, head2), body[halo:, :]], axis=0))
        act_ref[:, c * fc:(c + 1) * fc] = (_gelu_tanh_x2(convs[0]) * convs[1]).astype(BF16)

    out = x1 + jnp.dot(act_ref[...], wdn_ref[...], preferred_element_type=F32)
    if final:
        ms = jnp.mean(out * out, axis=-1, keepdims=True)
        out = out * lax.rsqrt(ms + NORM_EPS) * fg_ref[...]
    o_ref[...] = out


def _ffn(layer, x2d, att, sgu, w_out, g2, w_up, conv_w, conv_b, w_down, final_g, *, seq, final):
    rows = x2d.shape[0]
    tm = FFN_TM
    grid = (rows // tm,)
    row_spec = lambda width: pl.BlockSpec((tm, width), lambda i: (i, 0))
    whole = pl.BlockSpec(memory_space=pltpu.VMEM)
    small = functools.partial(_layer_spec, layer, single_buffer=False)
    return pl.pallas_call(
        functools.partial(_ffn_kernel, tm=tm, tiles_per_batch=seq // tm, final=final),
        grid=grid,
        in_specs=[row_spec(D_MODEL), row_spec(ATT_WIDTH), row_spec(SGU_WIDTH),
                  _layer_spec(layer, w_out), small(g2), _layer_spec(layer, w_up),
                  _layer_spec(layer, conv_w), small(conv_b), _layer_spec(layer, w_down),
                  whole],
        out_specs=row_spec(D_MODEL),
        out_shape=jax.ShapeDtypeStruct((rows, D_MODEL), F32),
        scratch_shapes=[
            pltpu.VMEM((2, D_FF // FFN_FC, SUBLANES, FFN_FC), F32),
            pltpu.VMEM((tm, D_FF), BF16),
        ],
        compiler_params=pltpu.CompilerParams(
            dimension_semantics=("arbitrary",), vmem_limit_bytes=VMEM_LIMIT_BYTES),
        name="ffn",
    )(x2d, att, sgu, w_out, g2, w_up, conv_w, conv_b, w_down, final_g)


def kernel(x, norm1_g, w_in, lam_q1, lam_k1, lam_q2, lam_k2, subln_g, sgu_ln_g, sgu_ln_b,
           sgu_w, sgu_b, w_out, norm2_g, ffn_w_up, ffn_conv_w, ffn_conv_b, ffn_w_down, final_g):
    batch, seq, _ = x.shape
    rows = batch * seq
    x2d = x.reshape(rows, D_MODEL)
    slopes = jnp.asarray([_alibi_slope(h) * LOG2_E for h in range(N_HEADS)], dtype=F32)
    rows_of = lambda a: a.reshape(DEPTH, 1, -1).astype(F32)
    norm1_g, norm2_g, subln_g, sgu_ln_g, sgu_ln_b, conv_b = map(
        rows_of, (norm1_g, norm2_g, subln_g, sgu_ln_g, sgu_ln_b, ffn_conv_b))
    lam_q1, lam_k1, lam_q2, lam_k2 = map(rows_of, (lam_q1, lam_k1, lam_q2, lam_k2))
    final_g = final_g.reshape(1, -1).astype(F32)
    w_in, w_out, w_up, w_down = (w.astype(BF16) for w in (w_in, w_out, ffn_w_up, ffn_w_down))
    sgu_w, conv_w = sgu_w.astype(F32), ffn_conv_w.astype(F32)
    sgu_b_full = jnp.broadcast_to(sgu_b.astype(F32)[..., None],
                                  (DEPTH, N_GROUPS, SGU_CHUNK, GROUP_DIM))
    for l in range(DEPTH):
        lam_init = _lambda_init(l)
        qat, qbt, ka, kb, vt, norms, sgu = _mix_in(l, x2d, norm1_g, w_in, sgu_ln_g, sgu_ln_b,
                                                   sgu_w, sgu_b_full)
        pair_q, pair_k, pair_n = _off_diagonal_pairs(norms, slopes, batch, seq)
        att = _attention(l, pair_q, pair_k, pair_n, slopes, lam_q1, lam_k1, lam_q2, lam_k2,
                         subln_g, qat, qbt, ka, kb, vt, batch=batch, seq=seq, lam_init=lam_init)
        x2d = _ffn(l, x2d, att, sgu, w_out, norm2_g, w_up, conv_w, conv_b, w_down, final_g,
                   seq=seq, final=(l == DEPTH - 1))
    return x2d.reshape(batch, seq, D_MODEL)
```

```python
import functools
import math

import jax
import jax.numpy as jnp
import numpy as np
from jax import lax
from jax.experimental import pallas as pl
from jax.experimental.pallas import tpu as pltpu

F32 = jnp.float32
BF16 = jnp.bfloat16

D_MODEL = 1024
DEPTH = 2
N_HEADS = 4
N_MAPS = 2
HEAD_DIM = 64
HEAD_SLAB = 2 * HEAD_DIM
BF16_ROWS = 16
V_ROWS = HEAD_SLAB + BF16_ROWS
ATT_WIDTH = N_HEADS * HEAD_SLAB
N_GROUPS = 4
SGU_CHUNK = 128
GROUP_DIM = 128
SGU_WIDTH = N_GROUPS * GROUP_DIM
D_FF = 2816
CONV_WIDTH = 3
NORM_EPS = 1e-6
SUBLN_EPS = 1e-5
LN_EPS = 1e-5
NEG_INF = -1e30
LOG2_E = math.log2(math.e)

SUBLANES = 8
BF16_EXACT_INT = 256
UNDERFLOW_LOG2 = 150.0
NORM_SAFETY = 1.001
VMEM_LIMIT_BYTES = 56 * 1024 * 1024

MIX_TM = 1024
ATT_T = 512
ATT_UNROLL = 16
FFN_TM = 1024
FFN_FC = 256


def _gelu_tanh(x):
    c = math.sqrt(2.0 / math.pi)
    return 0.5 * x * (1.0 + jnp.tanh(c * (x + 0.044715 * (x * x * x))))


def _gelu_tanh_x2(x):
    c = math.sqrt(2.0 / math.pi)
    return x * (1.0 + jnp.tanh(x * (c + (c * 0.044715) * (x * x))))


def _alibi_slope(head):
    return 2.0 ** (-8.0 * (head + 1) / N_HEADS)


def _bf16_terms(value, n=3):
    terms = []
    rest = value
    for _ in range(n):
        term = float(np.asarray(rest, dtype=BF16).astype(np.float32))
        terms.append(term)
        rest -= term
    return terms


def _lambda_init(layer_idx):
    return 0.8 - 0.6 * math.exp(-0.3 * layer_idx)


def _layer_spec(layer, stack, single_buffer=True):
    shape = tuple(stack.shape[1:])
    zeros = (0,) * len(shape)
    mode = dict(pipeline_mode=pl.Buffered(1)) if single_buffer else {}
    return pl.BlockSpec((None,) + shape, lambda *_: (layer,) + zeros, **mode)


def _alibi_operand_constants(t):
    pos = np.arange(t)
    n_terms = len(_bf16_terms(1.0))
    parts = [pos % BF16_EXACT_INT] * n_terms + [pos - pos % BF16_EXACT_INT] * n_terms
    parts = np.stack(parts, axis=1).astype(np.float32)
    first_lane = (HEAD_DIM, 0)
    k_aug = np.zeros((N_MAPS, t, HEAD_SLAB), np.float32)
    q_aug = np.zeros((N_HEADS * N_MAPS, HEAD_SLAB), np.float32)
    for mp, lane0 in enumerate(first_lane):
        k_aug[mp, :, lane0:lane0 + 2 * n_terms] = parts
        for h in range(N_HEADS):
            factor = _bf16_terms(_alibi_slope(h) * LOG2_E)
            q_aug[N_MAPS * h + mp, lane0:lane0 + 2 * n_terms] = factor + factor
    return jnp.asarray(k_aug), jnp.asarray(q_aug)


def _mixin_kernel(x_ref, g_ref, w_ref, lng_ref, lnb_ref, sw_ref, sb_ref, kaug_ref, qaug_ref,
                  qat_ref, qbt_ref, ka_ref, kb_ref, vt_ref, norm_ref, sgu_ref, *, tm, t):
    x = x_ref[...]
    ms = jnp.mean(x * x, axis=-1, keepdims=True)
    hb = (x * lax.rsqrt(ms + NORM_EPS) * g_ref[...]).astype(BF16)

    def proj(c0, width):
        return jnp.dot(hb, w_ref[:, c0:c0 + width], preferred_element_type=F32)

    zu = proj(3 * ATT_WIDTH, SGU_WIDTH)
    zg = proj(3 * ATT_WIDTH + SGU_WIDTH, SGU_WIDTH)
    zq = proj(0, ATT_WIDTH) * (HEAD_DIM ** -0.5 * LOG2_E)
    zk = proj(ATT_WIDTH, ATT_WIDTH)
    zv = proj(2 * ATT_WIDTH, ATT_WIDTH)
    in_a = lax.broadcasted_iota(jnp.int32, (tm, HEAD_SLAB), 1) < HEAD_DIM

    def max_sq_norm(slab):
        sq = slab.astype(BF16).astype(F32) ** 2
        return jnp.max(jnp.sum(sq, axis=-1, keepdims=True), axis=0, keepdims=True)

    stat_row = lax.broadcasted_iota(jnp.int32, (SUBLANES, HEAD_SLAB), 0)
    stat_lane = lax.broadcasted_iota(jnp.int32, (SUBLANES, HEAD_SLAB), 1)
    n_tiles = tm // t
    norms = [jnp.zeros((SUBLANES, HEAD_SLAB), F32)] * n_tiles
    k_aug = [jnp.concatenate([kaug_ref[mp]] * n_tiles, axis=0) for mp in range(N_MAPS)]
    for h in range(N_HEADS):
        lo, hi = h * HEAD_SLAB, (h + 1) * HEAD_SLAB
        qs, ks = zq[:, lo:hi], zk[:, lo:hi]
        ka_ref[:, lo:hi] = jnp.where(in_a, ks, k_aug[0]).astype(BF16)
        kb_ref[:, lo:hi] = jnp.where(in_a, k_aug[1], ks).astype(BF16)
        q_rows = qaug_ref[N_MAPS * h:N_MAPS * (h + 1), :]
        qat = jnp.where(in_a, qs, q_rows[0:1, :]).T.astype(BF16)
        qbt = jnp.where(in_a, q_rows[1:2, :], qs).T.astype(BF16)
        vt = zv[:, lo:hi].T.astype(BF16)
        for c in range(n_tiles):
            r0, r1 = c * t, (c + 1) * t
            norms[c] = jnp.where((stat_row == 0) & (stat_lane == h), max_sq_norm(qs[r0:r1]), norms[c])
            norms[c] = jnp.where((stat_row == 1) & (stat_lane == h), max_sq_norm(ks[r0:r1]), norms[c])
            qat_ref[c, lo:hi, :] = qat[:, r0:r1]
            qbt_ref[c, lo:hi, :] = qbt[:, r0:r1]
            vt_ref[c, h * V_ROWS:h * V_ROWS + HEAD_SLAB, :] = vt[:, r0:r1]
            vt_ref[c, h * V_ROWS + HEAD_SLAB:(h + 1) * V_ROWS, :] = jnp.ones((BF16_ROWS, t), BF16)
    for c in range(n_tiles):
        norm_ref[c] = norms[c]

    u = _gelu_tanh(zu)
    vg = _gelu_tanh(zg)
    row = lax.broadcasted_iota(jnp.int32, (SGU_CHUNK, SGU_CHUNK), 0)
    col = lax.broadcasted_iota(jnp.int32, (SGU_CHUNK, SGU_CHUNK), 1)
    tril = col <= row
    for g in range(N_GROUPS):
        lo, hi = g * GROUP_DIM, (g + 1) * GROUP_DIM
        y = vg[:, lo:hi]
        mu = jnp.mean(y, axis=-1, keepdims=True)
        yc = y - mu
        var = jnp.mean(yc * yc, axis=-1, keepdims=True)
        yn = (yc * lax.rsqrt(var + LN_EPS) * lng_ref[:, lo:hi] + lnb_ref[:, lo:hi]).astype(BF16)
        wm = jnp.where(tril, sw_ref[g], 0.0).astype(BF16)
        bias = sb_ref[g]
        n_chunks = tm // SGU_CHUNK
        chunks = jnp.concatenate([yn[c * SGU_CHUNK:(c + 1) * SGU_CHUNK, :]
                                  for c in range(n_chunks)], axis=1)
        vmix = jnp.dot(wm, chunks, preferred_element_type=F32)
        for c in range(n_chunks):
            r0, r1 = c * SGU_CHUNK, (c + 1) * SGU_CHUNK
            mixed = vmix[:, c * GROUP_DIM:(c + 1) * GROUP_DIM] + bias
            sgu_ref[r0:r1, lo:hi] = (u[r0:r1, lo:hi] * mixed).astype(BF16)


def _mix_in(layer, x2d, g, w_in, ln_g, ln_b, sgu_w, sgu_b_full):
    rows = x2d.shape[0]
    tm, t = MIX_TM, ATT_T
    assert tm % t == 0
    small = functools.partial(_layer_spec, layer, single_buffer=False)
    k_aug, q_aug = _alibi_operand_constants(t)
    grid = (rows // tm,)
    row_spec = lambda width: pl.BlockSpec((tm, width), lambda i: (i, 0))
    tr_spec = pl.BlockSpec((tm // t, ATT_WIDTH, t), lambda i: (i, 0, 0))
    whole = pl.BlockSpec(memory_space=pltpu.VMEM)
    row_sds = jax.ShapeDtypeStruct((rows, ATT_WIDTH), BF16)
    tr_sds = jax.ShapeDtypeStruct((rows // t, ATT_WIDTH, t), BF16)
    vt_spec = pl.BlockSpec((tm // t, N_HEADS * V_ROWS, t), lambda i: (i, 0, 0))
    vt_sds = jax.ShapeDtypeStruct((rows // t, N_HEADS * V_ROWS, t), BF16)
    norm_spec = pl.BlockSpec((tm // t, SUBLANES, HEAD_SLAB), lambda i: (i, 0, 0))
    norm_sds = jax.ShapeDtypeStruct((rows // t, SUBLANES, HEAD_SLAB), F32)
    return pl.pallas_call(
        functools.partial(_mixin_kernel, tm=tm, t=t),
        grid=grid,
        in_specs=[row_spec(D_MODEL), small(g), _layer_spec(layer, w_in), small(ln_g), small(ln_b),
                  _layer_spec(layer, sgu_w), small(sgu_b_full), whole, whole],
        out_specs=[tr_spec, tr_spec, row_spec(ATT_WIDTH), row_spec(ATT_WIDTH), vt_spec, norm_spec,
                   row_spec(SGU_WIDTH)],
        out_shape=[tr_sds, tr_sds, row_sds, row_sds, vt_sds, norm_sds, row_sds],
        compiler_params=pltpu.CompilerParams(
            dimension_semantics=("arbitrary",), vmem_limit_bytes=VMEM_LIMIT_BYTES),
        name="mix_in",
    )(x2d, g, w_in, ln_g, ln_b, sgu_w, sgu_b_full, k_aug, q_aug)


def _score_block(qi, kj, masked, q_refs, k_refs, s_ref, bm_ref, *, t):
    k0 = pl.multiple_of(kj * t, t)
    if masked:
        key = lax.broadcasted_iota(jnp.int32, (t, t), 0)
        qry = lax.broadcasted_iota(jnp.int32, (t, t), 1)
        causal = key <= qry
    for mp in range(N_MAPS):
        st = jnp.dot(k_refs[mp][pl.ds(k0, t), :], q_refs[mp][qi], preferred_element_type=F32)
        if masked:
            st = jnp.where(causal, st, NEG_INF)
        s_ref[mp] = st
        bm_ref[mp] = jnp.max(st, axis=0, keepdims=True)


def _softmax_block(qi, c, vt_blk, s_ref, bm_ref, m_ref, l_ref, acc_ref, first):
    for mp in range(N_MAPS):
        bm = bm_ref[mp] + c
        m_new = bm if first else jnp.maximum(m_ref[qi, mp], bm)
        p = jnp.exp2(s_ref[mp] - (m_new - c))
        pv_ps = jnp.dot(vt_blk, p.astype(BF16), preferred_element_type=F32)
        pv, ps = pv_ps[0:HEAD_SLAB, :], pv_ps[HEAD_SLAB:HEAD_SLAB + 1, :]
        if first:
            l_ref[qi, mp] = ps
            acc_ref[qi, mp] = pv
        else:
            alpha = jnp.exp2(m_ref[qi, mp] - m_new)
            l_ref[qi, mp] = alpha * l_ref[qi, mp] + ps
            acc_ref[qi, mp] = alpha * acc_ref[qi, mp] + pv
        m_ref[qi, mp] = m_new


def _attn_kernel(pair_q_ref, pair_k_ref, pair_n_ref,
                 slopes_ref, lq1_ref, lk1_ref, lq2_ref, lk2_ref, sg_ref,
                 qat_ref, qbt_ref, ka_ref, kb_ref, vt_ref, o_ref,
                 m_ref, l_ref, acc_ref, sa_ref, bma_ref, sb_ref, bmb_ref, *, t, nq, lam_init):
    bh = pl.program_id(0) * N_HEADS + pl.program_id(1)
    slope = slopes_ref[pl.program_id(1)]
    n_pairs = pair_n_ref[bh]
    score = functools.partial(_score_block, q_refs=(qat_ref, qbt_ref), k_refs=(ka_ref, kb_ref), t=t)
    softmax = functools.partial(_softmax_block, m_ref=m_ref, l_ref=l_ref, acc_ref=acc_ref)
    buf_a = dict(s_ref=sa_ref, bm_ref=bma_ref)
    buf_b = dict(s_ref=sb_ref, bm_ref=bmb_ref)

    bufs = (buf_a, buf_b)

    def entry(e):
        qi, kj = pair_q_ref[bh, e], pair_k_ref[bh, e]
        c = jnp.where(e < n_pairs, ((kj - qi) * t).astype(F32) * slope, NEG_INF)
        return qi, kj, c

    def score_entry(e, buf):
        qi, kj, _ = entry(e)
        score(qi, kj, False, **buf)

    def softmax_entry(e, buf):
        qi, kj, c = entry(e)
        softmax(qi, c, vt_ref[kj], first=False, **buf)

    score(0, 0, True, **buf_a)

    def diag_group(n, carry):
        for u in range(ATT_UNROLL):
            qi = ATT_UNROLL * n + u
            if isinstance(qi, int) and qi == nq - 1:
                score_entry(0, bufs[(u + 1) % 2])
            else:
                score(qi + 1, qi + 1, True, **bufs[(u + 1) % 2])
            softmax(qi, 0.0, vt_ref[qi], first=True, **bufs[u % 2])
        return carry

    lax.fori_loop(0, nq // ATT_UNROLL - 1, diag_group, 0)
    diag_group(nq // ATT_UNROLL - 1, 0)

    def off_group(first, size):
        for u in range(size):
            e = first + u
            score_entry(e + 1, bufs[(u + 1) % 2])
            softmax_entry(e, bufs[u % 2])

    n_groups = n_pairs // ATT_UNROLL
    lax.fori_loop(0, n_groups, lambda n, carry: off_group(ATT_UNROLL * n, ATT_UNROLL), None)
    done = n_groups * ATT_UNROLL
    lax.fori_loop(0, (n_pairs - done + 1) // 2, lambda n, carry: off_group(done + 2 * n, 2), None)

    lam = (jnp.exp(jnp.sum(lq1_ref[...] * lk1_ref[...], axis=-1, keepdims=True))
           - jnp.exp(jnp.sum(lq2_ref[...] * lk2_ref[...], axis=-1, keepdims=True))
           + lam_init)

    def finish(qi, carry):
        o = (acc_ref[qi, 0] * (1.0 / l_ref[qi, 0])
             - lam * (acc_ref[qi, 1] * (1.0 / l_ref[qi, 1])))
        ms = jnp.mean(o * o, axis=0, keepdims=True)
        o = o * lax.rsqrt(ms + SUBLN_EPS) * (1.0 - lam_init)
        o_ref[pl.ds(pl.multiple_of(qi * t, t), t), :] = (o.T * sg_ref[...]).astype(BF16)
        return carry

    lax.fori_loop(0, nq, finish, 0, unroll=4)


def _attention(layer, pair_q, pair_k, pair_n, slopes, lq1, lk1, lq2, lk2, subln_g,
               qat, qbt, ka, kb, vt, *, batch, seq, lam_init):
    t = ATT_T
    nq = seq // t
    small = functools.partial(_layer_spec, layer, single_buffer=False)
    qt_spec = pl.BlockSpec((nq, HEAD_SLAB, t), lambda b, h, *_: (b, h, 0))
    vt_spec = pl.BlockSpec((nq, V_ROWS, t), lambda b, h, *_: (b, h, 0))
    row_spec = pl.BlockSpec((seq, HEAD_SLAB), lambda b, h, *_: (b, h))
    whole = pl.BlockSpec(memory_space=pltpu.VMEM)
    smem = pl.BlockSpec(memory_space=pltpu.SMEM)
    stat = pltpu.VMEM((nq, N_MAPS, 1, t), F32)
    acc = pltpu.VMEM((nq, N_MAPS, HEAD_SLAB, t), F32)
    score_buf = pltpu.VMEM((N_MAPS, t, t), F32)
    colmax_buf = pltpu.VMEM((N_MAPS, 1, t), F32)
    return pl.pallas_call(
        functools.partial(_attn_kernel, t=t, nq=nq, lam_init=lam_init),
        grid_spec=pltpu.PrefetchScalarGridSpec(
            num_scalar_prefetch=3,
            grid=(batch, N_HEADS),
            in_specs=[smem, small(lq1), small(lk1), small(lq2), small(lk2), small(subln_g),
                      qt_spec, qt_spec, row_spec, row_spec, vt_spec],
            out_specs=row_spec,
            scratch_shapes=[stat, stat, acc, score_buf, colmax_buf, score_buf, colmax_buf]),
        out_shape=jax.ShapeDtypeStruct((batch * seq, ATT_WIDTH), BF16),
        compiler_params=pltpu.CompilerParams(
            dimension_semantics=("arbitrary", "arbitrary"),
            vmem_limit_bytes=VMEM_LIMIT_BYTES),
        name="diff_attn",
    )(pair_q, pair_k, pair_n, slopes, lq1, lk1, lq2, lk2, subln_g, qat, qbt, ka, kb, vt)


def _off_diagonal_pairs(norms, slopes, batch, seq):
    nq = seq // ATT_T
    tile = np.arange(nq)
    below = tile[:, None] > tile[None, :]
    min_dist = ((tile[:, None] - tile[None, :] - 1) * ATT_T + 1).astype(np.float32)
    norms = norms.reshape(batch, nq, SUBLANES, HEAD_SLAB)
    qn = (jnp.sqrt(norms[:, :, 0, :N_HEADS]) * NORM_SAFETY).transpose(0, 2, 1)
    kn = (jnp.sqrt(norms[:, :, 1, :N_HEADS]) * NORM_SAFETY).transpose(0, 2, 1)
    bound = (qn[..., :, None] * (kn[..., None, :] + kn[..., :, None])
             - slopes[None, :, None, None] * min_dist)
    keep = below & ~(bound < -UNDERFLOW_LOG2)
    keep = keep.reshape(batch * N_HEADS, nq * nq)
    n_entries = nq * (nq - 1) // 2 + ATT_UNROLL
    order = jnp.argsort(~keep, axis=-1, stable=True)[:, :n_entries].astype(jnp.int32)
    return order // nq, order % nq, jnp.sum(keep, axis=-1, dtype=jnp.int32)


def _ffn_kernel(x_ref, att_ref, sgu_ref, wo_ref, g2_ref, wup_ref, cw_ref, cb_ref, wdn_ref, fg_ref,
                o_ref, carry_ref, act_ref, *, tm, tiles_per_batch, final):
    fc = FFN_FC
    halo = SUBLANES

    @pl.when(pl.program_id(0) % tiles_per_batch == 0)
    def _():
        carry_ref[...] = jnp.zeros_like(carry_ref)

    x1_parts, h2_parts = [], []
    for r0 in range(0, tm, tm // 2):
        r1 = r0 + tm // 2
        part = (x_ref[r0:r1, :]
                + jnp.dot(att_ref[r0:r1, :], wo_ref[0:ATT_WIDTH, :], preferred_element_type=F32)
                + jnp.dot(sgu_ref[r0:r1, :], wo_ref[ATT_WIDTH:ATT_WIDTH + SGU_WIDTH, :],
                          preferred_element_type=F32))
        ms = jnp.mean(part * part, axis=-1, keepdims=True)
        x1_parts.append(part)
        h2_parts.append((part * lax.rsqrt(ms + NORM_EPS) * g2_ref[...]).astype(BF16))
    x1 = jnp.concatenate(x1_parts, axis=0)
    h2 = jnp.concatenate(h2_parts, axis=0)

    rows8 = lax.broadcasted_iota(jnp.int32, (halo, fc), 0)
    for c in range(D_FF // fc):
        convs = []
        for half in range(2):
            col0 = half * D_FF + c * fc
            up = jnp.dot(h2, wup_ref[:, col0:col0 + fc], preferred_element_type=F32)
            prev = carry_ref[half, c]
            carry_ref[half, c] = up[tm - halo:tm, :]
            scale = 0.5 if half == 1 else 1.0
            w0 = cw_ref[0:1, col0:col0 + fc] * scale
            w1 = cw_ref[1:2, col0:col0 + fc] * scale
            w2 = cw_ref[2:3, col0:col0 + fc] * scale
            bias = cb_ref[:, col0:col0 + fc] * scale

            def conv(cur, back1, back2):
                return bias + w0 * back2 + w1 * back1 + w2 * cur

            body = conv(up, pltpu.roll(up, 1, axis=0), pltpu.roll(up, 2, axis=0))
            head = up[0:halo, :]
            head1 = jnp.where(rows8 < 1, pltpu.roll(prev, 1, axis=0), pltpu.roll(head, 1, axis=0))
            head2 = jnp.where(rows8 < 2, pltpu.roll(prev, 2, axis=0), pltpu.roll(head, 2, axis=0))
            convs.append(jnp.concatenate([conv(head, head1, head2), body[halo:, :]], axis=0))
        act_ref[:, c * fc:(c + 1) * fc] = (_gelu_tanh_x2(convs[0]) * convs[1]).astype(BF16)

    out = x1 + jnp.dot(act_ref[...], wdn_ref[...], preferred_element_type=F32)
    if final:
        ms = jnp.mean(out * out, axis=-1, keepdims=True)
        out = out * lax.rsqrt(ms + NORM_EPS) * fg_ref[...]
    o_ref[...] = out


def _ffn(layer, x2d, att, sgu, w_out, g2, w_up, conv_w, conv_b, w_down, final_g, *, seq, final):
    rows = x2d.shape[0]
    tm = FFN_TM
    grid = (rows // tm,)
    row_spec = lambda width: pl.BlockSpec((tm, width), lambda i: (i, 0))
    whole = pl.BlockSpec(memory_space=pltpu.VMEM)
    small = functools.partial(_layer_spec, layer, single_buffer=False)
    return pl.pallas_call(
        functools.partial(_ffn_kernel, tm=tm, tiles_per_batch=seq // tm, final=final),
        grid=grid,
        in_specs=[row_spec(D_MODEL), row_spec(ATT_WIDTH), row_spec(SGU_WIDTH),
                  _layer_spec(layer, w_out), small(g2), _layer_spec(layer, w_up),
                  _layer_spec(layer, conv_w), small(conv_b), _layer_spec(layer, w_down),
                  whole],
        out_specs=row_spec(D_MODEL),
        out_shape=jax.ShapeDtypeStruct((rows, D_MODEL), F32),
        scratch_shapes=[
            pltpu.VMEM((2, D_FF // FFN_FC, SUBLANES, FFN_FC), F32),
            pltpu.VMEM((tm, D_FF), BF16),
        ],
        compiler_params=pltpu.CompilerParams(
            dimension_semantics=("arbitrary",), vmem_limit_bytes=VMEM_LIMIT_BYTES),
        name="ffn",
    )(x2d, att, sgu, w_out, g2, w_up, conv_w, conv_b, w_down, final_g)


def kernel(x, norm1_g, w_in, lam_q1, lam_k1, lam_q2, lam_k2, subln_g, sgu_ln_g, sgu_ln_b,
           sgu_w, sgu_b, w_out, norm2_g, ffn_w_up, ffn_conv_w, ffn_conv_b, ffn_w_down, final_g):
    batch, seq, _ = x.shape
    rows = batch * seq
    x2d = x.reshape(rows, D_MODEL)
    slopes = jnp.asarray([_alibi_slope(h) * LOG2_E for h in range(N_HEADS)], dtype=F32)
    rows_of = lambda a: a.reshape(DEPTH, 1, -1).astype(F32)
    norm1_g, norm2_g, subln_g, sgu_ln_g, sgu_ln_b, conv_b = map(
        rows_of, (norm1_g, norm2_g, subln_g, sgu_ln_g, sgu_ln_b, ffn_conv_b))
    lam_q1, lam_k1, lam_q2, lam_k2 = map(rows_of, (lam_q1, lam_k1, lam_q2, lam_k2))
    final_g = final_g.reshape(1, -1).astype(F32)
    w_in, w_out, w_up, w_down = (w.astype(BF16) for w in (w_in, w_out, ffn_w_up, ffn_w_down))
    sgu_w, conv_w = sgu_w.astype(F32), ffn_conv_w.astype(F32)
    sgu_b_full = jnp.broadcast_to(sgu_b.astype(F32)[..., None],
                                  (DEPTH, N_GROUPS, SGU_CHUNK, GROUP_DIM))
    for l in range(DEPTH):
        lam_init = _lambda_init(l)
        qat, qbt, ka, kb, vt, norms, sgu = _mix_in(l, x2d, norm1_g, w_in, sgu_ln_g, sgu_ln_b,
                                                   sgu_w, sgu_b_full)
        pair_q, pair_k, pair_n = _off_diagonal_pairs(norms, slopes, batch, seq)
        att = _attention(l, pair_q, pair_k, pair_n, slopes, lam_q1, lam_k1, lam_q2, lam_k2,
                         subln_g, qat, qbt, ka, kb, vt, batch=batch, seq=seq, lam_init=lam_init)
        x2d = _ffn(l, x2d, att, sgu, w_out, norm2_g, w_up, conv_w, conv_b, w_down, final_g,
                   seq=seq, final=(l == DEPTH - 1))
    return x2d.reshape(batch, seq, D_MODEL)
```

```python
import functools
import math

import jax
import jax.numpy as jnp
import numpy as np
from jax import lax
from jax.experimental import pallas as pl
from jax.experimental.pallas import tpu as pltpu

F32 = jnp.float32
BF16 = jnp.bfloat16

D_MODEL = 1024
DEPTH = 2
N_HEADS = 4
N_MAPS = 2
HEAD_DIM = 64
HEAD_SLAB = 2 * HEAD_DIM
BF16_ROWS = 16
V_ROWS = HEAD_SLAB + BF16_ROWS
ATT_WIDTH = N_HEADS * HEAD_SLAB
N_GROUPS = 4
SGU_CHUNK = 128
GROUP_DIM = 128
SGU_WIDTH = N_GROUPS * GROUP_DIM
D_FF = 2816
CONV_WIDTH = 3
NORM_EPS = 1e-6
SUBLN_EPS = 1e-5
LN_EPS = 1e-5
NEG_INF = -1e30
LOG2_E = math.log2(math.e)

SUBLANES = 8
BF16_EXACT_INT = 256
UNDERFLOW_LOG2 = math.inf
NORM_SAFETY = 1.001
VMEM_LIMIT_BYTES = 56 * 1024 * 1024

MIX_TM = 1024
ATT_T = 512
ATT_UNROLL = 16
FFN_TM = 1024
FFN_FC = 256


def _gelu_tanh(x):
    c = math.sqrt(2.0 / math.pi)
    return 0.5 * x * (1.0 + jnp.tanh(c * (x + 0.044715 * (x * x * x))))


def _gelu_tanh_x2(x):
    c = math.sqrt(2.0 / math.pi)
    return x * (1.0 + jnp.tanh(x * (c + (c * 0.044715) * (x * x))))


def _alibi_slope(head):
    return 2.0 ** (-8.0 * (head + 1) / N_HEADS)


def _bf16_terms(value, n=3):
    terms = []
    rest = value
    for _ in range(n):
        term = float(np.asarray(rest, dtype=BF16).astype(np.float32))
        terms.append(term)
        rest -= term
    return terms


def _lambda_init(layer_idx):
    return 0.8 - 0.6 * math.exp(-0.3 * layer_idx)


def _layer_spec(layer, stack, single_buffer=True):
    shape = tuple(stack.shape[1:])
    zeros = (0,) * len(shape)
    mode = dict(pipeline_mode=pl.Buffered(1)) if single_buffer else {}
    return pl.BlockSpec((None,) + shape, lambda *_: (layer,) + zeros, **mode)


def _alibi_operand_constants(t):
    pos = np.arange(t)
    n_terms = len(_bf16_terms(1.0))
    parts = [pos % BF16_EXACT_INT] * n_terms + [pos - pos % BF16_EXACT_INT] * n_terms
    parts = np.stack(parts, axis=1).astype(np.float32)
    first_lane = (HEAD_DIM, 0)
    k_aug = np.zeros((N_MAPS, t, HEAD_SLAB), np.float32)
    q_aug = np.zeros((N_HEADS * N_MAPS, HEAD_SLAB), np.float32)
    for mp, lane0 in enumerate(first_lane):
        k_aug[mp, :, lane0:lane0 + 2 * n_terms] = parts
        for h in range(N_HEADS):
            factor = _bf16_terms(_alibi_slope(h) * LOG2_E)
            q_aug[N_MAPS * h + mp, lane0:lane0 + 2 * n_terms] = factor + factor
    return jnp.asarray(k_aug), jnp.asarray(q_aug)


def _mixin_kernel(x_ref, g_ref, w_ref, lng_ref, lnb_ref, sw_ref, sb_ref, kaug_ref, qaug_ref,
                  qat_ref, qbt_ref, ka_ref, kb_ref, vt_ref, norm_ref, sgu_ref, *, tm, t):
    x = x_ref[...]
    ms = jnp.mean(x * x, axis=-1, keepdims=True)
    hb = (x * lax.rsqrt(ms + NORM_EPS) * g_ref[...]).astype(BF16)

    def proj(c0, width):
        return jnp.dot(hb, w_ref[:, c0:c0 + width], preferred_element_type=F32)

    zu = proj(3 * ATT_WIDTH, SGU_WIDTH)
    zg = proj(3 * ATT_WIDTH + SGU_WIDTH, SGU_WIDTH)
    zq = proj(0, ATT_WIDTH) * (HEAD_DIM ** -0.5 * LOG2_E)
    zk = proj(ATT_WIDTH, ATT_WIDTH)
    zv = proj(2 * ATT_WIDTH, ATT_WIDTH)
    in_a = lax.broadcasted_iota(jnp.int32, (tm, HEAD_SLAB), 1) < HEAD_DIM

    def max_sq_norm(slab):
        sq = slab.astype(BF16).astype(F32) ** 2
        return jnp.max(jnp.sum(sq, axis=-1, keepdims=True), axis=0, keepdims=True)

    stat_row = lax.broadcasted_iota(jnp.int32, (SUBLANES, HEAD_SLAB), 0)
    stat_lane = lax.broadcasted_iota(jnp.int32, (SUBLANES, HEAD_SLAB), 1)
    n_tiles = tm // t
    norms = [jnp.zeros((SUBLANES, HEAD_SLAB), F32)] * n_tiles
    k_aug = [jnp.concatenate([kaug_ref[mp]] * n_tiles, axis=0) for mp in range(N_MAPS)]
    for h in range(N_HEADS):
        lo, hi = h * HEAD_SLAB, (h + 1) * HEAD_SLAB
        qs, ks = zq[:, lo:hi], zk[:, lo:hi]
        ka_ref[:, lo:hi] = jnp.where(in_a, ks, k_aug[0]).astype(BF16)
        kb_ref[:, lo:hi] = jnp.where(in_a, k_aug[1], ks).astype(BF16)
        q_rows = qaug_ref[N_MAPS * h:N_MAPS * (h + 1), :]
        qat = jnp.where(in_a, qs, q_rows[0:1, :]).T.astype(BF16)
        qbt = jnp.where(in_a, q_rows[1:2, :], qs).T.astype(BF16)
        vt = zv[:, lo:hi].T.astype(BF16)
        for c in range(n_tiles):
            r0, r1 = c * t, (c + 1) * t
            norms[c] = jnp.where((stat_row == 0) & (stat_lane == h), max_sq_norm(qs[r0:r1]), norms[c])
            norms[c] = jnp.where((stat_row == 1) & (stat_lane == h), max_sq_norm(ks[r0:r1]), norms[c])
            qat_ref[c, lo:hi, :] = qat[:, r0:r1]
            qbt_ref[c, lo:hi, :] = qbt[:, r0:r1]
            vt_ref[c, h * V_ROWS:h * V_ROWS + HEAD_SLAB, :] = vt[:, r0:r1]
            vt_ref[c, h * V_ROWS + HEAD_SLAB:(h + 1) * V_ROWS, :] = jnp.ones((BF16_ROWS, t), BF16)
    for c in range(n_tiles):
        norm_ref[c] = norms[c]

    u = _gelu_tanh(zu)
    vg = _gelu_tanh(zg)
    row = lax.broadcasted_iota(jnp.int32, (SGU_CHUNK, SGU_CHUNK), 0)
    col = lax.broadcasted_iota(jnp.int32, (SGU_CHUNK, SGU_CHUNK), 1)
    tril = col <= row
    for g in range(N_GROUPS):
        lo, hi = g * GROUP_DIM, (g + 1) * GROUP_DIM
        y = vg[:, lo:hi]
        mu = jnp.mean(y, axis=-1, keepdims=True)
        yc = y - mu
        var = jnp.mean(yc * yc, axis=-1, keepdims=True)
        yn = (yc * lax.rsqrt(var + LN_EPS) * lng_ref[:, lo:hi] + lnb_ref[:, lo:hi]).astype(BF16)
        wm = jnp.where(tril, sw_ref[g], 0.0).astype(BF16)
        bias = sb_ref[g]
        n_chunks = tm // SGU_CHUNK
        chunks = jnp.concatenate([yn[c * SGU_CHUNK:(c + 1) * SGU_CHUNK, :]
                                  for c in range(n_chunks)], axis=1)
        vmix = jnp.dot(wm, chunks, preferred_element_type=F32)
        for c in range(n_chunks):
            r0, r1 = c * SGU_CHUNK, (c + 1) * SGU_CHUNK
            mixed = vmix[:, c * GROUP_DIM:(c + 1) * GROUP_DIM] + bias
            sgu_ref[r0:r1, lo:hi] = (u[r0:r1, lo:hi] * mixed).astype(BF16)


def _mix_in(layer, x2d, g, w_in, ln_g, ln_b, sgu_w, sgu_b_full):
    rows = x2d.shape[0]
    tm, t = MIX_TM, ATT_T
    assert tm % t == 0
    small = functools.partial(_layer_spec, layer, single_buffer=False)
    k_aug, q_aug = _alibi_operand_constants(t)
    grid = (rows // tm,)
    row_spec = lambda width: pl.BlockSpec((tm, width), lambda i: (i, 0))
    tr_spec = pl.BlockSpec((tm // t, ATT_WIDTH, t), lambda i: (i, 0, 0))
    whole = pl.BlockSpec(memory_space=pltpu.VMEM)
    row_sds = jax.ShapeDtypeStruct((rows, ATT_WIDTH), BF16)
    tr_sds = jax.ShapeDtypeStruct((rows // t, ATT_WIDTH, t), BF16)
    vt_spec = pl.BlockSpec((tm // t, N_HEADS * V_ROWS, t), lambda i: (i, 0, 0))
    vt_sds = jax.ShapeDtypeStruct((rows // t, N_HEADS * V_ROWS, t), BF16)
    norm_spec = pl.BlockSpec((tm // t, SUBLANES, HEAD_SLAB), lambda i: (i, 0, 0))
    norm_sds = jax.ShapeDtypeStruct((rows // t, SUBLANES, HEAD_SLAB), F32)
    return pl.pallas_call(
        functools.partial(_mixin_kernel, tm=tm, t=t),
        grid=grid,
        in_specs=[row_spec(D_MODEL), small(g), _layer_spec(layer, w_in), small(ln_g), small(ln_b),
                  _layer_spec(layer, sgu_w), small(sgu_b_full), whole, whole],
        out_specs=[tr_spec, tr_spec, row_spec(ATT_WIDTH), row_spec(ATT_WIDTH), vt_spec, norm_spec,
                   row_spec(SGU_WIDTH)],
        out_shape=[tr_sds, tr_sds, row_sds, row_sds, vt_sds, norm_sds, row_sds],
        compiler_params=pltpu.CompilerParams(
            dimension_semantics=("arbitrary",), vmem_limit_bytes=VMEM_LIMIT_BYTES),
        name="mix_in",
    )(x2d, g, w_in, ln_g, ln_b, sgu_w, sgu_b_full, k_aug, q_aug)


def _score_block(qi, kj, masked, q_refs, k_refs, s_ref, bm_ref, *, t):
    k0 = pl.multiple_of(kj * t, t)
    if masked:
        key = lax.broadcasted_iota(jnp.int32, (t, t), 0)
        qry = lax.broadcasted_iota(jnp.int32, (t, t), 1)
        causal = key <= qry
    for mp in range(N_MAPS):
        st = jnp.dot(k_refs[mp][pl.ds(k0, t), :], q_refs[mp][qi], preferred_element_type=F32)
        if masked:
            st = jnp.where(causal, st, NEG_INF)
        s_ref[mp] = st
        bm_ref[mp] = jnp.max(st, axis=0, keepdims=True)


def _softmax_block(qi, c, vt_blk, s_ref, bm_ref, m_ref, l_ref, acc_ref, first):
    for mp in range(N_MAPS):
        bm = bm_ref[mp] + c
        m_new = bm if first else jnp.maximum(m_ref[qi, mp], bm)
        p = jnp.exp2(s_ref[mp] - (m_new - c))
        pv_ps = jnp.dot(vt_blk, p.astype(BF16), preferred_element_type=F32)
        pv, ps = pv_ps[0:HEAD_SLAB, :], pv_ps[HEAD_SLAB:HEAD_SLAB + 1, :]
        if first:
            l_ref[qi, mp] = ps
            acc_ref[qi, mp] = pv
        else:
            alpha = jnp.exp2(m_ref[qi, mp] - m_new)
            l_ref[qi, mp] = alpha * l_ref[qi, mp] + ps
            acc_ref[qi, mp] = alpha * acc_ref[qi, mp] + pv
        m_ref[qi, mp] = m_new


def _attn_kernel(pair_q_ref, pair_k_ref, pair_n_ref,
                 slopes_ref, lq1_ref, lk1_ref, lq2_ref, lk2_ref, sg_ref,
                 qat_ref, qbt_ref, ka_ref, kb_ref, vt_ref, o_ref,
                 m_ref, l_ref, acc_ref, sa_ref, bma_ref, sb_ref, bmb_ref, *, t, nq, lam_init):
    bh = pl.program_id(0) * N_HEADS + pl.program_id(1)
    slope = slopes_ref[pl.program_id(1)]
    n_pairs = pair_n_ref[bh]
    score = functools.partial(_score_block, q_refs=(qat_ref, qbt_ref), k_refs=(ka_ref, kb_ref), t=t)
    softmax = functools.partial(_softmax_block, m_ref=m_ref, l_ref=l_ref, acc_ref=acc_ref)
    buf_a = dict(s_ref=sa_ref, bm_ref=bma_ref)
    buf_b = dict(s_ref=sb_ref, bm_ref=bmb_ref)

    bufs = (buf_a, buf_b)

    def entry(e):
        qi, kj = pair_q_ref[bh, e], pair_k_ref[bh, e]
        c = jnp.where(e < n_pairs, ((kj - qi) * t).astype(F32) * slope, NEG_INF)
        return qi, kj, c

    def score_entry(e, buf):
        qi, kj, _ = entry(e)
        score(qi, kj, False, **buf)

    def softmax_entry(e, buf):
        qi, kj, c = entry(e)
        softmax(qi, c, vt_ref[kj], first=False, **buf)

    score(0, 0, True, **buf_a)

    def diag_group(n, carry):
        for u in range(ATT_UNROLL):
            qi = ATT_UNROLL * n + u
            if isinstance(qi, int) and qi == nq - 1:
                score_entry(0, bufs[(u + 1) % 2])
            else:
                score(qi + 1, qi + 1, True, **bufs[(u + 1) % 2])
            softmax(qi, 0.0, vt_ref[qi], first=True, **bufs[u % 2])
        return carry

    lax.fori_loop(0, nq // ATT_UNROLL - 1, diag_group, 0)
    diag_group(nq // ATT_UNROLL - 1, 0)

    def off_group(first, size):
        for u in range(size):
            e = first + u
            score_entry(e + 1, bufs[(u + 1) % 2])
            softmax_entry(e, bufs[u % 2])

    n_groups = n_pairs // ATT_UNROLL
    lax.fori_loop(0, n_groups, lambda n, carry: off_group(ATT_UNROLL * n, ATT_UNROLL), None)
    done = n_groups * ATT_UNROLL
    lax.fori_loop(0, (n_pairs - done + 1) // 2, lambda n, carry: off_group(done + 2 * n, 2), None)

    lam = (jnp.exp(jnp.sum(lq1_ref[...] * lk1_ref[...], axis=-1, keepdims=True))
           - jnp.exp(jnp.sum(lq2_ref[...] * lk2_ref[...], axis=-1, keepdims=True))
           + lam_init)

    def finish(qi, carry):
        o = (acc_ref[qi, 0] * (1.0 / l_ref[qi, 0])
             - lam * (acc_ref[qi, 1] * (1.0 / l_ref[qi, 1])))
        ms = jnp.mean(o * o, axis=0, keepdims=True)
        o = o * lax.rsqrt(ms + SUBLN_EPS) * (1.0 - lam_init)
        o_ref[pl.ds(pl.multiple_of(qi * t, t), t), :] = (o.T * sg_ref[...]).astype(BF16)
        return carry

    lax.fori_loop(0, nq, finish, 0, unroll=4)


def _attention(layer, pair_q, pair_k, pair_n, slopes, lq1, lk1, lq2, lk2, subln_g,
               qat, qbt, ka, kb, vt, *, batch, seq, lam_init):
    t = ATT_T
    nq = seq // t
    small = functools.partial(_layer_spec, layer, single_buffer=False)
    qt_spec = pl.BlockSpec((nq, HEAD_SLAB, t), lambda b, h, *_: (b, h, 0))
    vt_spec = pl.BlockSpec((nq, V_ROWS, t), lambda b, h, *_: (b, h, 0))
    row_spec = pl.BlockSpec((seq, HEAD_SLAB), lambda b, h, *_: (b, h))
    whole = pl.BlockSpec(memory_space=pltpu.VMEM)
    smem = pl.BlockSpec(memory_space=pltpu.SMEM)
    stat = pltpu.VMEM((nq, N_MAPS, 1, t), F32)
    acc = pltpu.VMEM((nq, N_MAPS, HEAD_SLAB, t), F32)
    score_buf = pltpu.VMEM((N_MAPS, t, t), F32)
    colmax_buf = pltpu.VMEM((N_MAPS, 1, t), F32)
    return pl.pallas_call(
        functools.partial(_attn_kernel, t=t, nq=nq, lam_init=lam_init),
        grid_spec=pltpu.PrefetchScalarGridSpec(
            num_scalar_prefetch=3,
            grid=(batch, N_HEADS),
            in_specs=[smem, small(lq1), small(lk1), small(lq2), small(lk2), small(subln_g),
                      qt_spec, qt_spec, row_spec, row_spec, vt_spec],
            out_specs=row_spec,
            scratch_shapes=[stat, stat, acc, score_buf, colmax_buf, score_buf, colmax_buf]),
        out_shape=jax.ShapeDtypeStruct((batch * seq, ATT_WIDTH), BF16),
        compiler_params=pltpu.CompilerParams(
            dimension_semantics=("arbitrary", "arbitrary"),
            vmem_limit_bytes=VMEM_LIMIT_BYTES),
        name="diff_attn",
    )(pair_q, pair_k, pair_n, slopes, lq1, lk1, lq2, lk2, subln_g, qat, qbt, ka, kb, vt)


def _off_diagonal_pairs(norms, slopes, batch, seq):
    nq = seq // ATT_T
    tile = np.arange(nq)
    below = tile[:, None] > tile[None, :]
    min_dist = ((tile[:, None] - tile[None, :] - 1) * ATT_T + 1).astype(np.float32)
    norms = norms.reshape(batch, nq, SUBLANES, HEAD_SLAB)
    qn = (jnp.sqrt(norms[:, :, 0, :N_HEADS]) * NORM_SAFETY).transpose(0, 2, 1)
    kn = (jnp.sqrt(norms[:, :, 1, :N_HEADS]) * NORM_SAFETY).transpose(0, 2, 1)
    bound = (qn[..., :, None] * (kn[..., None, :] + kn[..., :, None])
             - slopes[None, :, None, None] * min_dist)
    keep = below & ~(bound < -UNDERFLOW_LOG2)
    keep = keep.reshape(batch * N_HEADS, nq * nq)
    n_entries = nq * (nq - 1) // 2 + ATT_UNROLL
    order = jnp.argsort(~keep, axis=-1, stable=True)[:, :n_entries].astype(jnp.int32)
    return order // nq, order % nq, jnp.sum(keep, axis=-1, dtype=jnp.int32)


def _ffn_kernel(x_ref, att_ref, sgu_ref, wo_ref, g2_ref, wup_ref, cw_ref, cb_ref, wdn_ref, fg_ref,
                o_ref, carry_ref, act_ref, *, tm, tiles_per_batch, final):
    fc = FFN_FC
    halo = SUBLANES

    @pl.when(pl.program_id(0) % tiles_per_batch == 0)
    def _():
        carry_ref[...] = jnp.zeros_like(carry_ref)

    x1_parts, h2_parts = [], []
    for r0 in range(0, tm, tm // 2):
        r1 = r0 + tm // 2
        part = (x_ref[r0:r1, :]
                + jnp.dot(att_ref[r0:r1, :], wo_ref[0:ATT_WIDTH, :], preferred_element_type=F32)
                + jnp.dot(sgu_ref[r0:r1, :], wo_ref[ATT_WIDTH:ATT_WIDTH + SGU_WIDTH, :],
                          preferred_element_type=F32))
        ms = jnp.mean(part * part, axis=-1, keepdims=True)
        x1_parts.append(part)
        h2_parts.append((part * lax.rsqrt(ms + NORM_EPS) * g2_ref[...]).astype(BF16))
    x1 = jnp.concatenate(x1_parts, axis=0)
    h2 = jnp.concatenate(h2_parts, axis=0)

    rows8 = lax.broadcasted_iota(jnp.int32, (halo, fc), 0)
    for c in range(D_FF // fc):
        convs = []
        for half in range(2):
            col0 = half * D_FF + c * fc
            up = jnp.dot(h2, wup_ref[:, col0:col0 + fc], preferred_element_type=F32)
            prev = carry_ref[half, c]
            carry_ref[half, c] = up[tm - halo:tm, :]
            scale = 0.5 if half == 1 else 1.0
            w0 = cw_ref[0:1, col0:col0 + fc] * scale
            w1 = cw_ref[1:2, col0:col0 + fc] * scale
            w2 = cw_ref[2:3, col0:col0 + fc] * scale
            bias = cb_ref[:, col0:col0 + fc] * scale

            def conv(cur, back1, back2):
                return bias + w0 * back2 + w1 * back1 + w2 * cur

            body = conv(up, pltpu.roll(up, 1, axis=0), pltpu.roll(up, 2, axis=0))
            head = up[0:halo, :]
            head1 = jnp.where(rows8 < 1, pltpu.roll(prev, 1, axis=0), pltpu.roll(head, 1, axis=0))
            head2 = jnp.where(rows8 < 2, pltpu.roll(prev, 2, axis=0), pltpu.roll(head, 2, axis=0))
            convs.append(jnp.concatenate([conv(head, head1, head2), body[halo:, :]], axis=0))
        act_ref[:, c * fc:(c + 1) * fc] = (_gelu_tanh_x2(convs[0]) * convs[1]).astype(BF16)

    out = x1 + jnp.dot(act_ref[...], wdn_ref[...], preferred_element_type=F32)
    if final:
        ms = jnp.mean(out * out, axis=-1, keepdims=True)
        out = out * lax.rsqrt(ms + NORM_EPS) * fg_ref[...]
    o_ref[...] = out


def _ffn(layer, x2d, att, sgu, w_out, g2, w_up, conv_w, conv_b, w_down, final_g, *, seq, final):
    rows = x2d.shape[0]
    tm = FFN_TM
    grid = (rows // tm,)
    row_spec = lambda width: pl.BlockSpec((tm, width), lambda i: (i, 0))
    whole = pl.BlockSpec(memory_space=pltpu.VMEM)
    small = functools.partial(_layer_spec, layer, single_buffer=False)
    return pl.pallas_call(
        functools.partial(_ffn_kernel, tm=tm, tiles_per_batch=seq // tm, final=final),
        grid=grid,
        in_specs=[row_spec(D_MODEL), row_spec(ATT_WIDTH), row_spec(SGU_WIDTH),
                  _layer_spec(layer, w_out), small(g2), _layer_spec(layer, w_up),
                  _layer_spec(layer, conv_w), small(conv_b), _layer_spec(layer, w_down),
                  whole],
        out_specs=row_spec(D_MODEL),
        out_shape=jax.ShapeDtypeStruct((rows, D_MODEL), F32),
        scratch_shapes=[
            pltpu.VMEM((2, D_FF // FFN_FC, SUBLANES, FFN_FC), F32),
            pltpu.VMEM((tm, D_FF), BF16),
        ],
        compiler_params=pltpu.CompilerParams(
            dimension_semantics=("arbitrary",), vmem_limit_bytes=VMEM_LIMIT_BYTES),
        name="ffn",
    )(x2d, att, sgu, w_out, g2, w_up, conv_w, conv_b, w_down, final_g)


def kernel(x, norm1_g, w_in, lam_q1, lam_k1, lam_q2, lam_k2, subln_g, sgu_ln_g, sgu_ln_b,
           sgu_w, sgu_b, w_out, norm2_g, ffn_w_up, ffn_conv_w, ffn_conv_b, ffn_w_down, final_g):
    batch, seq, _ = x.shape
    rows = batch * seq
    x2d = x.reshape(rows, D_MODEL)
    slopes = jnp.asarray([_alibi_slope(h) * LOG2_E for h in range(N_HEADS)], dtype=F32)
    rows_of = lambda a: a.reshape(DEPTH, 1, -1).astype(F32)
    norm1_g, norm2_g, subln_g, sgu_ln_g, sgu_ln_b, conv_b = map(
        rows_of, (norm1_g, norm2_g, subln_g, sgu_ln_g, sgu_ln_b, ffn_conv_b))
    lam_q1, lam_k1, lam_q2, lam_k2 = map(rows_of, (lam_q1, lam_k1, lam_q2, lam_k2))
    final_g = final_g.reshape(1, -1).astype(F32)
    w_in, w_out, w_up, w_down = (w.astype(BF16) for w in (w_in, w_out, ffn_w_up, ffn_w_down))
    sgu_w, conv_w = sgu_w.astype(F32), ffn_conv_w.astype(F32)
    sgu_b_full = jnp.broadcast_to(sgu_b.astype(F32)[..., None],
                                  (DEPTH, N_GROUPS, SGU_CHUNK, GROUP_DIM))
    for l in range(DEPTH):
        lam_init = _lambda_init(l)
        qat, qbt, ka, kb, vt, norms, sgu = _mix_in(l, x2d, norm1_g, w_in, sgu_ln_g, sgu_ln_b,
                                                   sgu_w, sgu_b_full)
        pair_q, pair_k, pair_n = _off_diagonal_pairs(norms, slopes, batch, seq)
        att = _attention(l, pair_q, pair_k, pair_n, slopes, lam_q1, lam_k1, lam_q2, lam_k2,
                         subln_g, qat, qbt, ka, kb, vt, batch=batch, seq=seq, lam_init=lam_init)
        x2d = _ffn(l, x2d, att, sgu, w_out, norm2_g, w_up, conv_w, conv_b, w_down, final_g,
                   seq=seq, final=(l == DEPTH - 1))
    return x2d.reshape(batch, seq, D_MODEL)
```

```python
import functools
import math

import jax
import jax.numpy as jnp
import numpy as np
from jax import lax
from jax.experimental import pallas as pl
from jax.experimental.pallas import tpu as pltpu

F32 = jnp.float32
BF16 = jnp.bfloat16

D_MODEL = 1024
DEPTH = 2
N_HEADS = 4
N_MAPS = 2
HEAD_DIM = 64
HEAD_SLAB = 2 * HEAD_DIM
BF16_ROWS = 16
V_ROWS = HEAD_SLAB + BF16_ROWS
ATT_WIDTH = N_HEADS * HEAD_SLAB
N_GROUPS = 4
SGU_CHUNK = 128
GROUP_DIM = 128
SGU_WIDTH = N_GROUPS * GROUP_DIM
D_FF = 2816
CONV_WIDTH = 3
NORM_EPS = 1e-6
SUBLN_EPS = 1e-5
LN_EPS = 1e-5
NEG_INF = -1e30
LOG2_E = math.log2(math.e)

SUBLANES = 8
BF16_EXACT_INT = 256
UNDERFLOW_LOG2 = 150.0
NORM_SAFETY = 1.001
VMEM_LIMIT_BYTES = 56 * 1024 * 1024

MIX_TM = 1024
ATT_T = 512
ATT_UNROLL = 16
FFN_TM = 1024
FFN_FC = 256


def _gelu_tanh(x):
    c = math.sqrt(2.0 / math.pi)
    return 0.5 * x * (1.0 + jnp.tanh(c * (x + 0.044715 * (x * x * x))))


def _gelu_tanh_x2(x):
    c = math.sqrt(2.0 / math.pi)
    return x * (1.0 + jnp.tanh(x * (c + (c * 0.044715) * (x * x))))


def _alibi_slope(head):
    return 2.0 ** (-8.0 * (head + 1) / N_HEADS)


def _bf16_terms(value, n=3):
    terms = []
    rest = value
    for _ in range(n):
        term = float(np.asarray(rest, dtype=BF16).astype(np.float32))
        terms.append(term)
        rest -= term
    return terms


def _lambda_init(layer_idx):
    return 0.8 - 0.6 * math.exp(-0.3 * layer_idx)


def _layer_spec(layer, stack, single_buffer=True):
    shape = tuple(stack.shape[1:])
    zeros = (0,) * len(shape)
    mode = dict(pipeline_mode=pl.Buffered(1)) if single_buffer else {}
    return pl.BlockSpec((None,) + shape, lambda *_: (layer,) + zeros, **mode)


def _alibi_operand_constants(t):
    pos = np.arange(t)
    n_terms = len(_bf16_terms(1.0))
    parts = [pos % BF16_EXACT_INT] * n_terms + [pos - pos % BF16_EXACT_INT] * n_terms
    parts = np.stack(parts, axis=1).astype(np.float32)
    first_lane = (HEAD_DIM, 0)
    k_aug = np.zeros((N_MAPS, t, HEAD_SLAB), np.float32)
    q_aug = np.zeros((N_HEADS * N_MAPS, HEAD_SLAB), np.float32)
    for mp, lane0 in enumerate(first_lane):
        k_aug[mp, :, lane0:lane0 + 2 * n_terms] = parts
        for h in range(N_HEADS):
            factor = _bf16_terms(_alibi_slope(h) * LOG2_E)
            q_aug[N_MAPS * h + mp, lane0:lane0 + 2 * n_terms] = factor + factor
    return jnp.asarray(k_aug), jnp.asarray(q_aug)


def _mixin_kernel(x_ref, g_ref, w_ref, lng_ref, lnb_ref, sw_ref, sb_ref, kaug_ref, qaug_ref,
                  qat_ref, qbt_ref, ka_ref, kb_ref, vt_ref, norm_ref, sgu_ref, *, tm, t):
    x = x_ref[...]
    ms = jnp.mean(x * x, axis=-1, keepdims=True)
    hb = (x * lax.rsqrt(ms + NORM_EPS) * g_ref[...]).astype(BF16)

    def proj(c0, width):
        return jnp.dot(hb, w_ref[:, c0:c0 + width], preferred_element_type=F32)

    zu = proj(3 * ATT_WIDTH, SGU_WIDTH)
    zg = proj(3 * ATT_WIDTH + SGU_WIDTH, SGU_WIDTH)
    zq = proj(0, ATT_WIDTH) * (HEAD_DIM ** -0.5 * LOG2_E)
    zk = proj(ATT_WIDTH, ATT_WIDTH)
    zv = proj(2 * ATT_WIDTH, ATT_WIDTH)
    in_a = lax.broadcasted_iota(jnp.int32, (tm, HEAD_SLAB), 1) < HEAD_DIM

    def max_sq_norm(slab):
        sq = slab.astype(BF16).astype(F32) ** 2
        return jnp.max(jnp.sum(sq, axis=-1, keepdims=True), axis=0, keepdims=True)

    stat_row = lax.broadcasted_iota(jnp.int32, (SUBLANES, HEAD_SLAB), 0)
    stat_lane = lax.broadcasted_iota(jnp.int32, (SUBLANES, HEAD_SLAB), 1)
    n_tiles = tm // t
    norms = [jnp.zeros((SUBLANES, HEAD_SLAB), F32)] * n_tiles
    k_aug = [jnp.concatenate([kaug_ref[mp]] * n_tiles, axis=0) for mp in range(N_MAPS)]
    for h in range(N_HEADS):
        lo, hi = h * HEAD_SLAB, (h + 1) * HEAD_SLAB
        qs, ks = zq[:, lo:hi], zk[:, lo:hi]
        ka_ref[:, lo:hi] = jnp.where(in_a, ks, k_aug[0]).astype(BF16)
        kb_ref[:, lo:hi] = jnp.where(in_a, k_aug[1], ks).astype(BF16)
        q_rows = qaug_ref[N_MAPS * h:N_MAPS * (h + 1), :]
        qat = jnp.where(in_a, qs, q_rows[0:1, :]).T.astype(BF16)
        qbt = jnp.where(in_a, q_rows[1:2, :], qs).T.astype(BF16)
        vt = zv[:, lo:hi].T.astype(BF16)
        for c in range(n_tiles):
            r0, r1 = c * t, (c + 1) * t
            norms[c] = jnp.where((stat_row == 0) & (stat_lane == h), max_sq_norm(qs[r0:r1]), norms[c])
            norms[c] = jnp.where((stat_row == 1) & (stat_lane == h), max_sq_norm(ks[r0:r1]), norms[c])
            qat_ref[c, lo:hi, :] = qat[:, r0:r1]
            qbt_ref[c, lo:hi, :] = qbt[:, r0:r1]
            vt_ref[c, h * V_ROWS:h * V_ROWS + HEAD_SLAB, :] = vt[:, r0:r1]
            vt_ref[c, h * V_ROWS + HEAD_SLAB:(h + 1) * V_ROWS, :] = jnp.ones((BF16_ROWS, t), BF16)
    for c in range(n_tiles):
        norm_ref[c] = norms[c]

    u = _gelu_tanh(zu)
    vg = _gelu_tanh(zg)
    row = lax.broadcasted_iota(jnp.int32, (SGU_CHUNK, SGU_CHUNK), 0)
    col = lax.broadcasted_iota(jnp.int32, (SGU_CHUNK, SGU_CHUNK), 1)
    tril = col <= row
    for g in range(N_GROUPS):
        lo, hi = g * GROUP_DIM, (g + 1) * GROUP_DIM
        y = vg[:, lo:hi]
        mu = jnp.mean(y, axis=-1, keepdims=True)
        yc = y - mu
        var = jnp.mean(yc * yc, axis=-1, keepdims=True)
        yn = (yc * lax.rsqrt(var + LN_EPS) * lng_ref[:, lo:hi] + lnb_ref[:, lo:hi]).astype(BF16)
        wm = jnp.where(tril, sw_ref[g], 0.0).astype(BF16)
        bias = sb_ref[g]
        n_chunks = tm // SGU_CHUNK
        chunks = jnp.concatenate([yn[c * SGU_CHUNK:(c + 1) * SGU_CHUNK, :]
                                  for c in range(n_chunks)], axis=1)
        vmix = jnp.dot(wm, chunks, preferred_element_type=F32)
        for c in range(n_chunks):
            r0, r1 = c * SGU_CHUNK, (c + 1) * SGU_CHUNK
            mixed = vmix[:, c * GROUP_DIM:(c + 1) * GROUP_DIM] + bias
            sgu_ref[r0:r1, lo:hi] = (u[r0:r1, lo:hi] * mixed).astype(BF16)


def _mix_in(layer, x2d, g, w_in, ln_g, ln_b, sgu_w, sgu_b_full):
    rows = x2d.shape[0]
    tm, t = MIX_TM, ATT_T
    assert tm % t == 0
    small = functools.partial(_layer_spec, layer, single_buffer=False)
    k_aug, q_aug = _alibi_operand_constants(t)
    grid = (rows // tm,)
    row_spec = lambda width: pl.BlockSpec((tm, width), lambda i: (i, 0))
    tr_spec = pl.BlockSpec((tm // t, ATT_WIDTH, t), lambda i: (i, 0, 0))
    whole = pl.BlockSpec(memory_space=pltpu.VMEM)
    row_sds = jax.ShapeDtypeStruct((rows, ATT_WIDTH), BF16)
    tr_sds = jax.ShapeDtypeStruct((rows // t, ATT_WIDTH, t), BF16)
    vt_spec = pl.BlockSpec((tm // t, N_HEADS * V_ROWS, t), lambda i: (i, 0, 0))
    vt_sds = jax.ShapeDtypeStruct((rows // t, N_HEADS * V_ROWS, t), BF16)
    norm_spec = pl.BlockSpec((tm // t, SUBLANES, HEAD_SLAB), lambda i: (i, 0, 0))
    norm_sds = jax.ShapeDtypeStruct((rows // t, SUBLANES, HEAD_SLAB), F32)
    return pl.pallas_call(
        functools.partial(_mixin_kernel, tm=tm, t=t),
        grid=grid,
        in_specs=[row_spec(D_MODEL), small(g), _layer_spec(layer, w_in), small(ln_g), small(ln_b),
                  _layer_spec(layer, sgu_w), small(sgu_b_full), whole, whole],
        out_specs=[tr_spec, tr_spec, row_spec(ATT_WIDTH), row_spec(ATT_WIDTH), vt_spec, norm_spec,
                   row_spec(SGU_WIDTH)],
        out_shape=[tr_sds, tr_sds, row_sds, row_sds, vt_sds, norm_sds, row_sds],
        compiler_params=pltpu.CompilerParams(
            dimension_semantics=("arbitrary",), vmem_limit_bytes=VMEM_LIMIT_BYTES),
        name="mix_in",
    )(x2d, g, w_in, ln_g, ln_b, sgu_w, sgu_b_full, k_aug, q_aug)


def _score_block(qi, kj, masked, q_refs, k_refs, s_ref, bm_ref, *, t):
    k0 = pl.multiple_of(kj * t, t)
    if masked:
        key = lax.broadcasted_iota(jnp.int32, (t, t), 0)
        qry = lax.broadcasted_iota(jnp.int32, (t, t), 1)
        causal = key <= qry
    for mp in range(N_MAPS):
        st = jnp.dot(k_refs[mp][pl.ds(k0, t), :], q_refs[mp][qi], preferred_element_type=F32)
        if masked:
            st = jnp.where(causal, st, NEG_INF)
        s_ref[mp] = st
        bm_ref[mp] = jnp.max(st, axis=0, keepdims=True)


def _softmax_block(qi, c, vt_blk, s_ref, bm_ref, m_ref, l_ref, acc_ref, first):
    for mp in range(N_MAPS):
        bm = bm_ref[mp] + c
        m_new = bm if first else jnp.maximum(m_ref[qi, mp], bm)
        p = jnp.exp2(s_ref[mp] - (m_new - c))
        pv_ps = jnp.dot(vt_blk, p.astype(BF16), preferred_element_type=F32)
        pv, ps = pv_ps[0:HEAD_SLAB, :], pv_ps[HEAD_SLAB:HEAD_SLAB + 1, :]
        if first:
            l_ref[qi, mp] = ps
            acc_ref[qi, mp] = pv
        else:
            alpha = jnp.exp2(m_ref[qi, mp] - m_new)
            l_ref[qi, mp] = alpha * l_ref[qi, mp] + ps
            acc_ref[qi, mp] = alpha * acc_ref[qi, mp] + pv
        m_ref[qi, mp] = m_new


def _attn_kernel(pair_q_ref, pair_k_ref, pair_n_ref,
                 slopes_ref, lq1_ref, lk1_ref, lq2_ref, lk2_ref, sg_ref,
                 qat_ref, qbt_ref, ka_ref, kb_ref, vt_ref, o_ref,
                 m_ref, l_ref, acc_ref, sa_ref, bma_ref, sb_ref, bmb_ref, *, t, nq, lam_init):
    bh = pl.program_id(0) * N_HEADS + pl.program_id(1)
    slope = slopes_ref[pl.program_id(1)]
    n_pairs = pair_n_ref[bh]
    score = functools.partial(_score_block, q_refs=(qat_ref, qbt_ref), k_refs=(ka_ref, kb_ref), t=t)
    softmax = functools.partial(_softmax_block, m_ref=m_ref, l_ref=l_ref, acc_ref=acc_ref)
    buf_a = dict(s_ref=sa_ref, bm_ref=bma_ref)
    buf_b = dict(s_ref=sb_ref, bm_ref=bmb_ref)

    bufs = (buf_a, buf_b)

    def entry(e):
        qi, kj = pair_q_ref[bh, e], pair_k_ref[bh, e]
        c = jnp.where(e < n_pairs, ((kj - qi) * t).astype(F32) * slope, NEG_INF)
        return qi, kj, c

    def score_entry(e, buf):
        qi, kj, _ = entry(e)
        score(qi, kj, False, **buf)

    def softmax_entry(e, buf):
        qi, kj, c = entry(e)
        softmax(qi, c, vt_ref[kj], first=False, **buf)

    score(0, 0, True, **buf_a)

    def diag_group(n, carry):
        for u in range(ATT_UNROLL):
            qi = ATT_UNROLL * n + u
            if isinstance(qi, int) and qi == nq - 1:
                score_entry(0, bufs[(u + 1) % 2])
            else:
                score(qi + 1, qi + 1, True, **bufs[(u + 1) % 2])
            softmax(qi, 0.0, vt_ref[qi], first=True, **bufs[u % 2])
        return carry

    lax.fori_loop(0, nq // ATT_UNROLL - 1, diag_group, 0)
    diag_group(nq // ATT_UNROLL - 1, 0)

    def off_group(first, size):
        for u in range(size):
            e = first + u
            score_entry(e + 1, bufs[(u + 1) % 2])
            softmax_entry(e, bufs[u % 2])

    n_groups = n_pairs // ATT_UNROLL
    lax.fori_loop(0, n_groups, lambda n, carry: off_group(ATT_UNROLL * n, ATT_UNROLL), None)
    done = n_groups * ATT_UNROLL
    lax.fori_loop(0, (n_pairs - done + 1) // 2, lambda n, carry: off_group(done + 2 * n, 2), None)

    lam = (jnp.exp(jnp.sum(lq1_ref[...] * lk1_ref[...], axis=-1, keepdims=True))
           - jnp.exp(jnp.sum(lq2_ref[...] * lk2_ref[...], axis=-1, keepdims=True))
           + lam_init)

    def finish(qi, carry):
        o = (acc_ref[qi, 0] * (1.0 / l_ref[qi, 0])
             - lam * (acc_ref[qi, 1] * (1.0 / l_ref[qi, 1])))
        ms = jnp.mean(o * o, axis=0, keepdims=True)
        o = o * lax.rsqrt(ms + SUBLN_EPS) * (1.0 - lam_init)
        o_ref[pl.ds(pl.multiple_of(qi * t, t), t), :] = (o.T * sg_ref[...]).astype(BF16)
        return carry

    lax.fori_loop(0, nq, finish, 0, unroll=4)


def _attention(layer, pair_q, pair_k, pair_n, slopes, lq1, lk1, lq2, lk2, subln_g,
               qat, qbt, ka, kb, vt, *, batch, seq, lam_init):
    t = ATT_T
    nq = seq // t
    small = functools.partial(_layer_spec, layer, single_buffer=False)
    qt_spec = pl.BlockSpec((nq, HEAD_SLAB, t), lambda b, h, *_: (b, h, 0))
    vt_spec = pl.BlockSpec((nq, V_ROWS, t), lambda b, h, *_: (b, h, 0))
    row_spec = pl.BlockSpec((seq, HEAD_SLAB), lambda b, h, *_: (b, h))
    whole = pl.BlockSpec(memory_space=pltpu.VMEM)
    smem = pl.BlockSpec(memory_space=pltpu.SMEM)
    stat = pltpu.VMEM((nq, N_MAPS, 1, t), F32)
    acc = pltpu.VMEM((nq, N_MAPS, HEAD_SLAB, t), F32)
    score_buf = pltpu.VMEM((N_MAPS, t, t), F32)
    colmax_buf = pltpu.VMEM((N_MAPS, 1, t), F32)
    return pl.pallas_call(
        functools.partial(_attn_kernel, t=t, nq=nq, lam_init=lam_init),
        grid_spec=pltpu.PrefetchScalarGridSpec(
            num_scalar_prefetch=3,
            grid=(batch, N_HEADS),
            in_specs=[smem, small(lq1), small(lk1), small(lq2), small(lk2), small(subln_g),
                      qt_spec, qt_spec, row_spec, row_spec, vt_spec],
            out_specs=row_spec,
            scratch_shapes=[stat, stat, acc, score_buf, colmax_buf, score_buf, colmax_buf]),
        out_shape=jax.ShapeDtypeStruct((batch * seq, ATT_WIDTH), BF16),
        compiler_params=pltpu.CompilerParams(
            dimension_semantics=("arbitrary", "arbitrary"),
            vmem_limit_bytes=VMEM_LIMIT_BYTES),
        name="diff_attn",
    )(pair_q, pair_k, pair_n, slopes, lq1, lk1, lq2, lk2, subln_g, qat, qbt, ka, kb, vt)


def _off_diagonal_pairs(norms, slopes, batch, seq):
    nq = seq // ATT_T
    tile = np.arange(nq)
    below = tile[:, None] > tile[None, :]
    min_dist = ((tile[:, None] - tile[None, :] - 1) * ATT_T + 1).astype(np.float32)
    norms = norms.reshape(batch, nq, SUBLANES, HEAD_SLAB)
    qn = (jnp.sqrt(norms[:, :, 0, :N_HEADS]) * NORM_SAFETY).transpose(0, 2, 1)
    kn = (jnp.sqrt(norms[:, :, 1, :N_HEADS]) * NORM_SAFETY).transpose(0, 2, 1)
    bound = (qn[..., :, None] * (kn[..., None, :] + kn[..., :, None])
             - slopes[None, :, None, None] * min_dist)
    keep = below & ~(bound < -UNDERFLOW_LOG2)
    keep = keep.reshape(batch * N_HEADS, nq * nq)
    n_entries = nq * (nq - 1) // 2 + ATT_UNROLL
    order = jnp.argsort(~keep, axis=-1, stable=True)[:, :n_entries].astype(jnp.int32)
    return order // nq, order % nq, jnp.sum(keep, axis=-1, dtype=jnp.int32)


def _ffn_kernel(x_ref, att_ref, sgu_ref, wo_ref, g2_ref, wup_ref, cw_ref, cb_ref, wdn_ref, fg_ref,
                o_ref, carry_ref, act_ref, *, tm, tiles_per_batch, final):
    fc = FFN_FC
    halo = SUBLANES

    @pl.when(pl.program_id(0) % tiles_per_batch == 0)
    def _():
        carry_ref[...] = jnp.zeros_like(carry_ref)

    x1_parts, h2_parts = [], []
    for r0 in range(0, tm, tm // 2):
        r1 = r0 + tm // 2
        part = (x_ref[r0:r1, :]
                + jnp.dot(att_ref[r0:r1, :], wo_ref[0:ATT_WIDTH, :], preferred_element_type=F32)
                + jnp.dot(sgu_ref[r0:r1, :], wo_ref[ATT_WIDTH:ATT_WIDTH + SGU_WIDTH, :],
                          preferred_element_type=F32))
        ms = jnp.mean(part * part, axis=-1, keepdims=True)
        x1_parts.append(part)
        h2_parts.append((part * lax.rsqrt(ms + NORM_EPS) * g2_ref[...]).astype(BF16))
    o_ref[...] = jnp.concatenate(x1_parts, axis=0)
    h2 = jnp.concatenate(h2_parts, axis=0)

    rows8 = lax.broadcasted_iota(jnp.int32, (halo, fc), 0)
    for c in range(D_FF // fc):
        convs = []
        for half in range(2):
            col0 = half * D_FF + c * fc
            up = jnp.dot(h2, wup_ref[:, col0:col0 + fc], preferred_element_type=F32)
            prev = carry_ref[half, c]
            carry_ref[half, c] = up[tm - halo:tm, :]
            scale = 0.5 if half == 1 else 1.0
            w0 = cw_ref[0:1, col0:col0 + fc] * scale
            w1 = cw_ref[1:2, col0:col0 + fc] * scale
            w2 = cw_ref[2:3, col0:col0 + fc] * scale
            bias = cb_ref[:, col0:col0 + fc] * scale

            def conv(cur, back1, back2):
                return bias + w0 * back2 + w1 * back1 + w2 * cur

            body = conv(up, pltpu.roll(up, 1, axis=0), pltpu.roll(up, 2, axis=0))
            head = up[0:halo, :]
            head1 = jnp.where(rows8 < 1, pltpu.roll(prev, 1, axis=0), pltpu.roll(head, 1, axis=0))
            head2 = jnp.where(rows8 < 2, pltpu.roll(prev, 2, axis=0), pltpu.roll(head, 2, axis=0))
            convs.append(jnp.concatenate([conv(head, head1, head2), body[halo:, :]], axis=0))
        act_ref[:, c * fc:(c + 1) * fc] = (_gelu_tanh_x2(convs[0]) * convs[1]).astype(BF16)

    out = o_ref[...] + jnp.dot(act_ref[...], wdn_ref[...], preferred_element_type=F32)
    if final:
        ms = jnp.mean(out * out, axis=-1, keepdims=True)
        out = out * lax.rsqrt(ms + NORM_EPS) * fg_ref[...]
    o_ref[...] = out


def _ffn(layer, x2d, att, sgu, w_out, g2, w_up, conv_w, conv_b, w_down, final_g, *, seq, final):
    rows = x2d.shape[0]
    tm = FFN_TM
    grid = (rows // tm,)
    row_spec = lambda width: pl.BlockSpec((tm, width), lambda i: (i, 0))
    whole = pl.BlockSpec(memory_space=pltpu.VMEM)
    small = functools.partial(_layer_spec, layer, single_buffer=False)
    return pl.pallas_call(
        functools.partial(_ffn_kernel, tm=tm, tiles_per_batch=seq // tm, final=final),
        grid=grid,
        in_specs=[row_spec(D_MODEL), row_spec(ATT_WIDTH), row_spec(SGU_WIDTH),
                  _layer_spec(layer, w_out), small(g2), _layer_spec(layer, w_up),
                  _layer_spec(layer, conv_w), small(conv_b), _layer_spec(layer, w_down),
                  whole],
        out_specs=row_spec(D_MODEL),
        out_shape=jax.ShapeDtypeStruct((rows, D_MODEL), F32),
        scratch_shapes=[
            pltpu.VMEM((2, D_FF // FFN_FC, SUBLANES, FFN_FC), F32),
            pltpu.VMEM((tm, D_FF), BF16),
        ],
        compiler_params=pltpu.CompilerParams(
            dimension_semantics=("arbitrary",), vmem_limit_bytes=VMEM_LIMIT_BYTES),
        name="ffn",
    )(x2d, att, sgu, w_out, g2, w_up, conv_w, conv_b, w_down, final_g)


def kernel(x, norm1_g, w_in, lam_q1, lam_k1, lam_q2, lam_k2, subln_g, sgu_ln_g, sgu_ln_b,
           sgu_w, sgu_b, w_out, norm2_g, ffn_w_up, ffn_conv_w, ffn_conv_b, ffn_w_down, final_g):
    batch, seq, _ = x.shape
    rows = batch * seq
    x2d = x.reshape(rows, D_MODEL)
    slopes = jnp.asarray([_alibi_slope(h) * LOG2_E for h in range(N_HEADS)], dtype=F32)
    rows_of = lambda a: a.reshape(DEPTH, 1, -1).astype(F32)
    norm1_g, norm2_g, subln_g, sgu_ln_g, sgu_ln_b, conv_b = map(
        rows_of, (norm1_g, norm2_g, subln_g, sgu_ln_g, sgu_ln_b, ffn_conv_b))
    lam_q1, lam_k1, lam_q2, lam_k2 = map(rows_of, (lam_q1, lam_k1, lam_q2, lam_k2))
    final_g = final_g.reshape(1, -1).astype(F32)
    w_in, w_out, w_up, w_down = (w.astype(BF16) for w in (w_in, w_out, ffn_w_up, ffn_w_down))
    sgu_w, conv_w = sgu_w.astype(F32), ffn_conv_w.astype(F32)
    sgu_b_full = jnp.broadcast_to(sgu_b.astype(F32)[..., None],
                                  (DEPTH, N_GROUPS, SGU_CHUNK, GROUP_DIM))
    for l in range(DEPTH):
        lam_init = _lambda_init(l)
        qat, qbt, ka, kb, vt, norms, sgu = _mix_in(l, x2d, norm1_g, w_in, sgu_ln_g, sgu_ln_b,
                                                   sgu_w, sgu_b_full)
        pair_q, pair_k, pair_n = _off_diagonal_pairs(norms, slopes, batch, seq)
        att = _attention(l, pair_q, pair_k, pair_n, slopes, lam_q1, lam_k1, lam_q2, lam_k2,
                         subln_g, qat, qbt, ka, kb, vt, batch=batch, seq=seq, lam_init=lam_init)
        x2d = _ffn(l, x2d, att, sgu, w_out, norm2_g, w_up, conv_w, conv_b, w_down, final_g,
                   seq=seq, final=(l == DEPTH - 1))
    return x2d.reshape(batch, seq, D_MODEL)
```
